```python
import math
import jax
import jax.numpy as jnp
from jax import lax
import numpy as np

D_MODEL = 1024
BATCH = 16
SEQ = 2048
DEPTH = 2
DEC_BATCH = 16
DEC_SEQ = 32
PAST_LEN = 2048

CHUNK = 64
HEAD_DIM = 64
ATTN_SCALE = HEAD_DIM ** -0.5
A_HEADS = 4
A_WIDTH = A_HEADS * HEAD_DIM
A_PREV_CHUNKS = 8
A_REACH = A_PREV_CHUNKS * CHUNK
A_REL_CLIP = 128
B_HEADS = 8
B_KV_HEADS = 2
B_GROUP = B_HEADS // B_KV_HEADS
B_WIDTH = B_HEADS * HEAD_DIM
B_KV_WIDTH = B_KV_HEADS * HEAD_DIM
B_WINDOW = 128
B_PREV_CHUNKS = B_WINDOW // CHUNK
T5_BUCKETS = 32
T5_MAX_DISTANCE = 128
C_CH = D_MODEL // 4
CONV_W = 31
MIX_WIDTH = A_WIDTH + B_WIDTH + C_CH
IN_PROJ_WIDTH = 3 * A_WIDTH + B_WIDTH + 2 * B_KV_WIDTH + 2 * C_CH
SPLIT_POINTS = (A_WIDTH, 2 * A_WIDTH, 3 * A_WIDTH, 3 * A_WIDTH + B_WIDTH,
                3 * A_WIDTH + B_WIDTH + B_KV_WIDTH, 3 * A_WIDTH + B_WIDTH + 2 * B_KV_WIDTH)
N_EXPERTS = 32
TOP_K = 4
D_FF = D_MODEL
SWIGLU_LIMIT = 7.0
SWIGLU_ALPHA = 1.702
EXPERT_BLOCK = 128
DEEPNORM_ALPHA = (2 * DEPTH) ** 0.25
DEEPNORM_BETA = (8 * DEPTH) ** -0.25
LN_EPS = 1e-5
NEG_INF = -1e30

kernel_name = 'hybrid_streaming_encoder_step'


def layer_norm(x, g, b):
    xf = x.astype(jnp.float32)
    mu = jnp.mean(xf, axis=-1, keepdims=True)
    var = jnp.mean(jnp.square(xf - mu), axis=-1, keepdims=True)
    y = (xf - mu) * lax.rsqrt(var + LN_EPS) * g.astype(jnp.float32) + b.astype(jnp.float32)
    return y.astype(x.dtype)


def chunk_band(x, n_prev):
    b, t = x.shape[:2]
    nc = t // CHUNK
    xc = x.reshape(b, nc, CHUNK, *x.shape[2:])
    xp = jnp.concatenate([jnp.zeros((b, n_prev, CHUNK) + x.shape[2:], x.dtype), xc], axis=1)
    idx = np.arange(nc)[:, None] + np.arange(n_prev + 1)[None, :]
    return xp[:, idx].reshape(b, nc, (n_prev + 1) * CHUNK, *x.shape[2:])


def band_valid(nc, n_prev):
    return (np.arange(nc)[:, None] - n_prev + np.arange((n_prev + 1) * CHUNK)[None, :] // CHUNK) >= 0


def band_rel(n_prev):
    return np.arange((n_prev + 1) * CHUNK)[None, :] - n_prev * CHUNK - np.arange(CHUNK)[:, None]


def cache_rel(n_cache, s):
    return np.concatenate([np.arange(n_cache) - n_cache, np.arange(s)])[None, :] - np.arange(s)[:, None]


def clipped_rel_bias(rel, table):
    idx = np.clip(rel, -A_REL_CLIP, A_REL_CLIP) + A_REL_CLIP
    return jnp.moveaxis(table[idx].astype(jnp.float32), -1, 0)


def t5_bucket(rel):
    nb = T5_BUCKETS // 2
    max_exact = nb // 2
    n = np.abs(rel)
    nf = np.maximum(n, 1).astype(np.float32)
    large = max_exact + (np.log(nf / max_exact) / math.log(T5_MAX_DISTANCE / max_exact)
                         * (nb - max_exact)).astype(np.int32)
    large = np.minimum(large, nb - 1)
    return np.where(rel > 0, nb, 0) + np.where(n < max_exact, n, large)


def t5_rel_bias(rel, table):
    return jnp.moveaxis(table[t5_bucket(rel)].astype(jnp.float32), -1, 0)


def sink_softmax(logits, sink):
    m = jnp.maximum(jnp.max(logits, axis=-1, keepdims=True), sink)
    e = jnp.exp(logits - m)
    return e / (jnp.sum(e, axis=-1, keepdims=True) + jnp.exp(sink - m))


def glu(g):
    a, gate = jnp.split(g, 2, axis=-1)
    return a * jax.nn.sigmoid(gate)


def project_heads(h, w_in):
    b, t = h.shape[:2]
    qa, ka, va, qb, kb, vb, g = jnp.split(h @ w_in, list(SPLIT_POINTS), axis=-1)
    heads = lambda a, n: a.reshape(b, t, n, HEAD_DIM)
    return (heads(qa, A_HEADS), heads(ka, A_HEADS), heads(va, A_HEADS),
            heads(qb, B_HEADS), heads(kb, B_KV_HEADS), heads(vb, B_KV_HEADS), glu(g))


def band_attention_a_prompt(q, k, v, table):
    b, t = q.shape[:2]
    nc = t // CHUNK
    qc = q.reshape(b, nc, CHUNK, A_HEADS, HEAD_DIM)
    kb, vb = chunk_band(k, A_PREV_CHUNKS), chunk_band(v, A_PREV_CHUNKS)
    logits = jnp.einsum('bcqhd,bckhd->bhcqk', qc, kb).astype(jnp.float32) * ATTN_SCALE
    logits = logits + clipped_rel_bias(band_rel(A_PREV_CHUNKS), table)[None, :, None]
    logits = jnp.where(band_valid(nc, A_PREV_CHUNKS)[None, None, :, None, :], logits, NEG_INF)
    p = jax.nn.softmax(logits, axis=-1).astype(v.dtype)
    return jnp.einsum('bhcqk,bckhd->bcqhd', p, vb).reshape(b, t, A_WIDTH)


def band_attention_a_sample(q, k_all, v_all, n_cache, table):
    b, s = q.shape[:2]
    logits = jnp.einsum('bqhd,bkhd->bhqk', q, k_all).astype(jnp.float32) * ATTN_SCALE
    logits = logits + clipped_rel_bias(cache_rel(n_cache, s), table)[None]
    p = jax.nn.softmax(logits, axis=-1).astype(v_all.dtype)
    return jnp.einsum('bhqk,bkhd->bqhd', p, v_all).reshape(b, s, A_WIDTH)


def window_attention_b_prompt(q, k, v, table, sinks):
    b, t = q.shape[:2]
    nc = t // CHUNK
    qc = q.reshape(b, nc, CHUNK, B_KV_HEADS, B_GROUP, HEAD_DIM)
    kb, vb = chunk_band(k, B_PREV_CHUNKS), chunk_band(v, B_PREV_CHUNKS)
    logits = jnp.einsum('bcqngd,bcknd->bngcqk', qc, kb).astype(jnp.float32) * ATTN_SCALE
    bias = t5_rel_bias(band_rel(B_PREV_CHUNKS), table).reshape(B_KV_HEADS, B_GROUP, CHUNK, -1)
    logits = logits + bias[None, :, :, None]
    logits = jnp.where(band_valid(nc, B_PREV_CHUNKS)[None, None, None, :, None, :], logits, NEG_INF)
    sink = sinks.astype(jnp.float32).reshape(B_KV_HEADS, B_GROUP)[None, :, :, None, None, None]
    p = sink_softmax(logits, sink).astype(v.dtype)
    return jnp.einsum('bngcqk,bcknd->bcqngd', p, vb).reshape(b, t, B_WIDTH)


def window_attention_b_sample(q, k_all, v_all, n_cache, table, sinks):
    b, s = q.shape[:2]
    qg = q.reshape(b, s, B_KV_HEADS, B_GROUP, HEAD_DIM)
    logits = jnp.einsum('bqngd,bknd->bngqk', qg, k_all).astype(jnp.float32) * ATTN_SCALE
    bias = t5_rel_bias(cache_rel(n_cache, s), table).reshape(B_KV_HEADS, B_GROUP, s, -1)
    logits = logits + bias[None]
    sink = sinks.astype(jnp.float32).reshape(B_KV_HEADS, B_GROUP)[None, :, :, None, None]
    p = sink_softmax(logits, sink).astype(v_all.dtype)
    return jnp.einsum('bngqk,bknd->bqngd', p, v_all).reshape(b, s, B_WIDTH)


def conv_module_tail(u_ext, conv_w, conv_b, ln_g, ln_b):
    y = lax.conv_general_dilated(u_ext, conv_w[:, None, :].astype(u_ext.dtype), window_strides=(1,),
                                 padding='VALID', dimension_numbers=('NWC', 'WIO', 'NWC'),
                                 feature_group_count=C_CH)
    return jax.nn.silu(layer_norm(y + conv_b, ln_g, ln_b))


def mixer_prompt(h, w_in, w_out, rel_a, t5_table, sinks, conv_w, conv_b, cln_g, cln_b):
    b, t = h.shape[:2]
    qa, ka, va, qb, kb, vb, u = project_heads(h, w_in)
    oa = band_attention_a_prompt(qa, ka, va, rel_a)
    ob = window_attention_b_prompt(qb, kb, vb, t5_table, sinks)
    u_ext = jnp.concatenate([jnp.zeros((b, CONV_W - 1, C_CH), u.dtype), u], axis=1)
    oc = conv_module_tail(u_ext, conv_w, conv_b, cln_g, cln_b)
    out = jnp.concatenate([oa, ob, oc], axis=-1) @ w_out
    na, nb = min(A_REACH, t), min(B_WINDOW, t)
    state = (ka[:, t - na:], va[:, t - na:], kb[:, t - nb:], vb[:, t - nb:], u_ext[:, -(CONV_W - 1):])
    return out, state


def mixer_sample(h, ca_k, ca_v, cb_k, cb_v, conv_state,
                 w_in, w_out, rel_a, t5_table, sinks, conv_w, conv_b, cln_g, cln_b):
    qa, ka, va, qb, kb, vb, u = project_heads(h, w_in)
    na, nb = ca_k.shape[1], cb_k.shape[1]
    ka_all = jnp.concatenate([ca_k, ka], axis=1)
    va_all = jnp.concatenate([ca_v, va], axis=1)
    kb_all = jnp.concatenate([cb_k, kb], axis=1)
    vb_all = jnp.concatenate([cb_v, vb], axis=1)
    oa = band_attention_a_sample(qa, ka_all, va_all, na, rel_a)
    ob = window_attention_b_sample(qb, kb_all, vb_all, nb, t5_table, sinks)
    u_ext = jnp.concatenate([conv_state, u], axis=1)
    oc = conv_module_tail(u_ext, conv_w, conv_b, cln_g, cln_b)
    out = jnp.concatenate([oa, ob, oc], axis=-1) @ w_out
    state = (ka_all[:, -na:], va_all[:, -na:], kb_all[:, -nb:], vb_all[:, -nb:], u_ext[:, -(CONV_W - 1):])
    return out, state


def moe_ffn(h, w_router, b_router, w_e_in, b_e_in, w_e_out, b_e_out):
    b, t, d = h.shape
    n_tok = b * t
    n_assign = n_tok * TOP_K
    tok = h.reshape(n_tok, d)
    logits = (tok @ w_router).astype(jnp.float32) + b_router.astype(jnp.float32)
    top_val, top_idx = lax.top_k(logits, TOP_K)
    gate = jax.nn.softmax(top_val, axis=-1)
    flat_e = top_idx.reshape(n_assign)
    order = jnp.argsort(flat_e)
    sorted_e = flat_e[order]
    counts = jnp.bincount(flat_e, length=N_EXPERTS)
    padded = (counts + EXPERT_BLOCK - 1) // EXPERT_BLOCK * EXPERT_BLOCK
    pad_end = jnp.cumsum(padded)
    pad_start = pad_end - padded
    grp_start = jnp.cumsum(counts) - counts
    slot_sorted = pad_start[sorted_e] + jnp.arange(n_assign) - grp_start[sorted_e]
    slot = jnp.zeros((n_assign,), jnp.int32).at[order].set(slot_sorted.astype(jnp.int32))
    n_blocks = -(-n_assign // EXPERT_BLOCK) + N_EXPERTS
    rows = jnp.zeros((n_blocks * EXPERT_BLOCK, d), tok.dtype).at[slot].set(jnp.repeat(tok, TOP_K, axis=0))
    block_e = jnp.minimum(jnp.searchsorted(pad_end, jnp.arange(n_blocks) * EXPERT_BLOCK, side='right'),
                          N_EXPERTS - 1)

    def expert_block(args):
        xb, e = args
        hid = xb @ w_e_in[e] + b_e_in[e]
        x_glu, x_lin = jnp.split(hid, 2, axis=-1)
        x_glu = jnp.minimum(x_glu, SWIGLU_LIMIT)
        x_lin = jnp.clip(x_lin, -SWIGLU_LIMIT, SWIGLU_LIMIT)
        act = x_glu * jax.nn.sigmoid(SWIGLU_ALPHA * x_glu) * (x_lin + 1.0)
        return act @ w_e_out[e] + b_e_out[e]

    out_rows = lax.map(expert_block, (rows.reshape(n_blocks, EXPERT_BLOCK, d), block_e))
    picked = out_rows.reshape(n_blocks * EXPERT_BLOCK, d)[slot].reshape(n_tok, TOP_K, d)
    y = jnp.einsum('nk,nkd->nd', gate.astype(picked.dtype), picked)
    return y.reshape(b, t, d)


def trunk_layer(x, c, mixer, w_ada, b_ada, ln_g, ln_b, w_router, b_router, w_e_in, b_e_in, w_e_out, b_e_out):
    mod = jax.nn.silu(c) @ w_ada + b_ada
    sh1, sc1, g1, sh2, sc2, g2 = jnp.split(mod[:, None, :], 6, axis=-1)
    mix, state = mixer(x * (1.0 + sc1) + sh1)
    x = layer_norm(DEEPNORM_ALPHA * x + g1 * mix, ln_g[0], ln_b[0])
    ffn = moe_ffn(x * (1.0 + sc2) + sh2, w_router, b_router, w_e_in, b_e_in, w_e_out, b_e_out)
    x = layer_norm(DEEPNORM_ALPHA * x + g2 * ffn, ln_g[1], ln_b[1])
    return x, state


def setup_inputs(seed: int = 0) -> dict:
    key = jax.random.key(seed)
    ks = jax.random.split(key, 32)
    f32 = jnp.float32

    def normal(k, shape, scale):
        return jax.random.normal(k, shape, f32) * scale

    a_cache = min(A_REACH, PAST_LEN)
    b_cache = min(B_WINDOW, PAST_LEN)
    return {
        'x_prompt': normal(ks[0], (BATCH, SEQ, D_MODEL), 1.0),
        'x_sample': normal(ks[1], (DEC_BATCH, DEC_SEQ, D_MODEL), 1.0),
        'cache_a_k': normal(ks[2], (DEPTH, DEC_BATCH, a_cache, A_HEADS, HEAD_DIM), 1.0),
        'cache_a_v': normal(ks[3], (DEPTH, DEC_BATCH, a_cache, A_HEADS, HEAD_DIM), 1.0),
        'cache_b_k': normal(ks[4], (DEPTH, DEC_BATCH, b_cache, B_KV_HEADS, HEAD_DIM), 1.0),
        'cache_b_v': normal(ks[5], (DEPTH, DEC_BATCH, b_cache, B_KV_HEADS, HEAD_DIM), 1.0),
        'state_conv': normal(ks[6], (DEPTH, DEC_BATCH, CONV_W - 1, C_CH), 0.5),
        'c_prompt': normal(ks[7], (BATCH, D_MODEL), 1.0),
        'c_sample': normal(ks[8], (DEC_BATCH, D_MODEL), 1.0),
        'w_in': normal(ks[9], (DEPTH, D_MODEL, IN_PROJ_WIDTH), D_MODEL ** -0.5),
        'w_out': normal(ks[10], (DEPTH, MIX_WIDTH, D_MODEL), MIX_WIDTH ** -0.5 * DEEPNORM_BETA),
        'rel_bias_a': normal(ks[11], (DEPTH, 2 * A_REL_CLIP + 1, A_HEADS), 0.1),
        't5_bias': normal(ks[12], (T5_BUCKETS, B_HEADS), 0.1),
        'sinks': normal(ks[13], (DEPTH, B_HEADS), 0.5),
        'conv_w': normal(ks[14], (DEPTH, CONV_W, C_CH), CONV_W ** -0.5),
        'conv_b': normal(ks[15], (DEPTH, C_CH), 0.02),
        'conv_ln_g': 1.0 + normal(ks[16], (DEPTH, C_CH), 0.05),
        'conv_ln_b': normal(ks[17], (DEPTH, C_CH), 0.02),
        'w_ada': normal(ks[18], (DEPTH, D_MODEL, 6 * D_MODEL), 0.5 * D_MODEL ** -0.5),
        'b_ada': normal(ks[19], (DEPTH, 6 * D_MODEL), 0.02),
        'ln_g': 1.0 + normal(ks[20], (DEPTH, 2, D_MODEL), 0.05),
        'ln_b': normal(ks[21], (DEPTH, 2, D_MODEL), 0.02),
        'w_router': normal(ks[22], (DEPTH, D_MODEL, N_EXPERTS), D_MODEL ** -0.5),
        'b_router': normal(ks[23], (DEPTH, N_EXPERTS), 0.01),
        'w_e_in': normal(ks[24], (DEPTH, N_EXPERTS, D_MODEL, 2 * D_FF), D_MODEL ** -0.5),
        'b_e_in': normal(ks[25], (DEPTH, N_EXPERTS, 2 * D_FF), 0.02),
        'w_e_out': normal(ks[26], (DEPTH, N_EXPERTS, D_FF, D_MODEL), D_FF ** -0.5 * DEEPNORM_BETA),
        'b_e_out': normal(ks[27], (DEPTH, N_EXPERTS, D_MODEL), 0.02),
    }


def reference(x_prompt, x_sample, cache_a_k, cache_a_v, cache_b_k, cache_b_v, state_conv,
              c_prompt, c_sample, w_in, w_out, rel_bias_a, t5_bias, sinks, conv_w, conv_b,
              conv_ln_g, conv_ln_b, w_ada, b_ada, ln_g, ln_b, w_router, b_router,
              w_e_in, b_e_in, w_e_out, b_e_out):
    xp, xs = x_prompt, x_sample
    states_p, states_s = [], []
    for l in range(DEPTH):
        mix_w = (w_in[l], w_out[l], rel_bias_a[l], t5_bias, sinks[l],
                 conv_w[l], conv_b[l], conv_ln_g[l], conv_ln_b[l])
        ffn_w = (w_ada[l], b_ada[l], ln_g[l], ln_b[l], w_router[l], b_router[l],
                 w_e_in[l], b_e_in[l], w_e_out[l], b_e_out[l])
        xp, sp = trunk_layer(xp, c_prompt, lambda h: mixer_prompt(h, *mix_w), *ffn_w)
        xs, ss = trunk_layer(
            xs, c_sample,
            lambda h: mixer_sample(h, cache_a_k[l], cache_a_v[l], cache_b_k[l], cache_b_v[l],
                                   state_conv[l], *mix_w),
            *ffn_w)
        states_p.append(sp)
        states_s.append(ss)
    a_k_p, a_v_p, b_k_p, b_v_p, conv_p = (jnp.stack(z) for z in zip(*states_p))
    a_k_s, a_v_s, b_k_s, b_v_s, conv_s = (jnp.stack(z) for z in zip(*states_s))
    return (xp, xs, a_k_p, a_v_p, b_k_p, b_v_p, conv_p, a_k_s, a_v_s, b_k_s, b_v_s, conv_s)
```

```python
import functools
import math

import jax
import jax.numpy as jnp
import numpy as np
from jax import lax
from jax.experimental import pallas as pl
from jax.experimental.pallas import tpu as pltpu

F32 = jnp.float32
BF16 = jnp.bfloat16

D_MODEL = 1024
DEPTH = 2
CHUNK = 64
HEAD_DIM = 64
ATTN_SCALE = HEAD_DIM ** -0.5
A_HEADS = 4
A_WIDTH = A_HEADS * HEAD_DIM
A_PREV_CHUNKS = 8
A_REACH = A_PREV_CHUNKS * CHUNK
A_REL_CLIP = 128
B_HEADS = 8
B_KV_HEADS = 2
B_GROUP = B_HEADS // B_KV_HEADS
B_WIDTH = B_HEADS * HEAD_DIM
B_KV_WIDTH = B_KV_HEADS * HEAD_DIM
B_WINDOW = 128
B_PREV_CHUNKS = B_WINDOW // CHUNK
T5_BUCKETS = 32
T5_MAX_DISTANCE = 128
C_CH = D_MODEL // 4
CONV_W = 31
CONV_HALO = 32
N_EXPERTS = 32
TOP_K = 4
D_FF = D_MODEL
SWIGLU_LIMIT = 7.0
SWIGLU_ALPHA = 1.702
DEEPNORM_ALPHA = (2 * DEPTH) ** 0.25
LN_EPS = 1e-5
NEG_INF = -1e30

LANES = 128
ROW_TILE = 512
MOE_TILE = 256
IDX_STRIDE = 1024
VMEM_LIMIT = 56 * 1024 * 1024


def _cparams(n_axes=1, vmem=None):
    return pltpu.CompilerParams(dimension_semantics=("arbitrary",) * n_axes,
                                vmem_limit_bytes=vmem)


def _dot(a, b):
    return jnp.dot(a, b, preferred_element_type=F32)


def _layer_norm(z, g, b):
    mu = jnp.mean(z, axis=-1, keepdims=True)
    d = z - mu
    var = jnp.mean(d * d, axis=-1, keepdims=True)
    return d * lax.rsqrt(var + LN_EPS) * g + b


def _split_bf16(a):
    hi = a.astype(BF16)
    lo = (a - hi.astype(F32)).astype(BF16)
    return hi, lo


def _adaln_kernel(c_ref, w_ref, b_ref, o_ref):
    c = c_ref[...]
    a_hi, a_lo = _split_bf16(c * jax.nn.sigmoid(c))
    w_hi, w_lo = _split_bf16(w_ref[...])
    o_ref[...] = _dot(a_hi, w_hi) + _dot(a_lo, w_hi) + _dot(a_hi, w_lo) + b_ref[...]


def _adaln(c_all, w, b):
    nb, d = c_all.shape
    n = w.shape[1]
    tn = 1536
    return pl.pallas_call(
        _adaln_kernel,
        out_shape=jax.ShapeDtypeStruct((nb, n), F32),
        grid=(n // tn,),
        in_specs=[pl.BlockSpec((nb, d), lambda j: (0, 0)),
                  pl.BlockSpec((d, tn), lambda j: (0, j)),
                  pl.BlockSpec((1, tn), lambda j: (0, j))],
        out_specs=pl.BlockSpec((nb, tn), lambda j: (0, j)),
        compiler_params=_cparams(1, VMEM_LIMIT),
        name="adaln",
    )(c_all, w, b.reshape(1, n))


_QA = (0, A_WIDTH)
_KA = (A_WIDTH, 2 * A_WIDTH)
_VA = (2 * A_WIDTH, 3 * A_WIDTH)
_QB = (3 * A_WIDTH, 3 * A_WIDTH + B_WIDTH)
_KB = (_QB[1], _QB[1] + B_KV_WIDTH)
_VB = (_KB[1], _KB[1] + B_KV_WIDTH)
_GA = (_VB[1], _VB[1] + C_CH)
_GG = (_GA[1], _GA[1] + C_CH)


def _inproj_kernel(x_ref, sc_ref, sh_ref, w_ref,
                   qa_ref, ka_ref, va_ref, qb_ref, kb_ref, vb_ref, u_ref):
    h = (x_ref[...] * (1.0 + sc_ref[...]) + sh_ref[...]).astype(BF16)

    def proj(cols):
        return _dot(h, w_ref[:, cols[0]:cols[1]])

    qa_ref[...] = (proj(_QA) * ATTN_SCALE).astype(BF16)
    ka_ref[...] = proj(_KA)
    va_ref[...] = proj(_VA)
    qb_ref[...] = (proj(_QB) * ATTN_SCALE).astype(BF16)
    kb_ref[...] = proj(_KB)
    vb_ref[...] = proj(_VB)
    u_ref[...] = proj(_GA) * jax.nn.sigmoid(proj(_GG))


def _inproj(x, sc, sh, w_bf, *, tm, rows_per_batch, batch_off):
    m, d = x.shape
    bpb = rows_per_batch // tm
    mod_spec = pl.BlockSpec((None, 1, d), lambda i: (i // bpb + batch_off, 0, 0))
    widths = (A_WIDTH, A_WIDTH, A_WIDTH, B_WIDTH, B_KV_WIDTH, B_KV_WIDTH, C_CH)
    dtypes = (BF16, F32, F32, BF16, F32, F32, F32)
    return pl.pallas_call(
        _inproj_kernel,
        out_shape=[jax.ShapeDtypeStruct((m, w), dt) for w, dt in zip(widths, dtypes)],
        grid=(m // tm,),
        in_specs=[pl.BlockSpec((tm, d), lambda i: (i, 0)), mod_spec, mod_spec,
                  pl.BlockSpec(w_bf.shape, lambda i: (0, 0))],
        out_specs=[pl.BlockSpec((tm, w), lambda i: (i, 0)) for w in widths],
        compiler_params=_cparams(1, VMEM_LIMIT),
        name="inproj",
    )(x, sc, sh, w_bf)


def _attn_kernel(*refs, nc, cq, bw, pad, n_heads, group, use_sink):
    if use_sink:
        q_ref, k_ref, v_ref, bias_ref, sink_ref, o_ref, kp_ref, vp_ref = refs
    else:
        q_ref, k_ref, v_ref, bias_ref, o_ref, kp_ref, vp_ref = refs
        sink_ref = None
    tk = k_ref.shape[0]
    if pad:
        zeros = jnp.zeros((pad, kp_ref.shape[1]), BF16)
        kp_ref[0:pad, :] = zeros
        vp_ref[0:pad, :] = zeros
    kp_ref[pad:pad + tk, :] = k_ref[...].astype(BF16)
    vp_ref[pad:pad + tk, :] = v_ref[...].astype(BF16)

    def chunk(c):
        q0 = c * cq
        if not isinstance(c, int):
            q0 = pl.multiple_of(q0, cq)
        qt = q_ref[pl.ds(q0, cq), :]
        kt = kp_ref[pl.ds(q0, bw), :]
        vt = vp_ref[pl.ds(q0, bw), :]
        if pad:
            key_pos = lax.broadcasted_iota(jnp.int32, (cq, bw), 1)
            valid = key_pos >= pad - q0
        outs = []
        for h in range(n_heads):
            n = h // group
            qh = qt[:, h * HEAD_DIM:(h + 1) * HEAD_DIM]
            kh = kt[:, n * HEAD_DIM:(n + 1) * HEAD_DIM]
            vh = vt[:, n * HEAD_DIM:(n + 1) * HEAD_DIM]
            s = lax.dot_general(qh, kh, (((1,), (1,)), ((), ())),
                                preferred_element_type=F32)
            s = s + bias_ref[h]
            if pad:
                s = jnp.where(valid, s, NEG_INF)
            m = jnp.max(s, axis=-1, keepdims=True)
            if use_sink:
                m = jnp.maximum(m, sink_ref[h])
            e = jnp.exp(s - m)
            den = jnp.sum(e, axis=-1, keepdims=True)
            if use_sink:
                den = den + jnp.exp(sink_ref[h] - m)
            outs.append(_dot(e.astype(BF16), vh) / den)
        o_ref[pl.ds(q0, cq), :] = jnp.concatenate(outs, axis=-1).astype(o_ref.dtype)

    if nc == 1:
        chunk(0)
    else:
        def body(c, carry):
            chunk(c)
            return carry
        lax.fori_loop(0, nc, body, 0)


def _attention(q, k, v, bias, sinks, *, cq, n_prev, n_heads, group):
    b, t, qw = q.shape
    tk, kw = k.shape[1], k.shape[2]
    if n_prev is None:
        nc, bw, pad = 1, tk, 0
    else:
        nc, bw, pad = t // cq, (n_prev + 1) * cq, n_prev * cq
    use_sink = sinks is not None
    kern = functools.partial(_attn_kernel, nc=nc, cq=cq, bw=bw, pad=pad,
                             n_heads=n_heads, group=group, use_sink=use_sink)
    in_specs = [pl.BlockSpec((None, t, qw), lambda i: (i, 0, 0)),
                pl.BlockSpec((None, tk, kw), lambda i: (i, 0, 0)),
                pl.BlockSpec((None, tk, kw), lambda i: (i, 0, 0)),
                pl.BlockSpec(bias.shape, lambda i: (0, 0, 0))]
    args = [q, k, v, bias]
    if use_sink:
        in_specs.append(pl.BlockSpec(memory_space=pltpu.SMEM))
        args.append(sinks)
    return pl.pallas_call(
        kern,
        out_shape=jax.ShapeDtypeStruct((b, t, qw), BF16),
        grid=(b,),
        in_specs=in_specs,
        out_specs=pl.BlockSpec((None, t, qw), lambda i: (i, 0, 0)),
        scratch_shapes=[pltpu.VMEM((pad + tk, kw), BF16), pltpu.VMEM((pad + tk, kw), BF16)],
        compiler_params=_cparams(1, VMEM_LIMIT),
        name="attn_sink" if use_sink else "attn",
    )(*args)


def _conv_kernel(prev_ref, u_ref, w_ref, cb_ref, g_ref, b_ref, o_ref, up_ref, *, t, tt):
    up_ref[0:CONV_HALO, :] = prev_ref[...]
    up_ref[CONV_HALO:CONV_HALO + t, :] = u_ref[...]
    lead = CONV_HALO - (CONV_W - 1)

    def tile(t0):
        win = up_ref[pl.ds(t0, tt + CONV_HALO), :]
        acc = jnp.zeros((tt, C_CH), F32)
        for j in range(CONV_W):
            acc = acc + win[j + lead:j + lead + tt, :] * w_ref[j:j + 1, :]
        y = _layer_norm(acc + cb_ref[...], g_ref[...], b_ref[...])
        o_ref[pl.ds(t0, tt), :] = (y * jax.nn.sigmoid(y)).astype(o_ref.dtype)

    if t == tt:
        tile(0)
    else:
        def body(i, carry):
            tile(pl.multiple_of(i * tt, tt))
            return carry
        lax.fori_loop(0, t // tt, body, 0)


def _conv_tail(prev, u, conv_w, conv_b, ln_g, ln_b):
    b, t, c = u.shape
    tt = min(t, 128)
    vec = lambda a: a.reshape(1, c)
    vspec = pl.BlockSpec((1, c), lambda i: (0, 0))
    return pl.pallas_call(
        functools.partial(_conv_kernel, t=t, tt=tt),
        out_shape=jax.ShapeDtypeStruct((b, t, c), BF16),
        grid=(b,),
        in_specs=[pl.BlockSpec((None, CONV_HALO, c), lambda i: (i, 0, 0)),
                  pl.BlockSpec((None, t, c), lambda i: (i, 0, 0)),
                  pl.BlockSpec((CONV_W, c), lambda i: (0, 0)),
                  vspec, vspec, vspec],
        out_specs=pl.BlockSpec((None, t, c), lambda i: (i, 0, 0)),
        scratch_shapes=[pltpu.VMEM((CONV_HALO + t, c), F32)],
        compiler_params=_cparams(1, VMEM_LIMIT),
        name="conv_tail",
    )(prev, u, conv_w, vec(conv_b), vec(ln_g), vec(ln_b))


def _outproj_kernel(oa_ref, ob_ref, oc_ref, x_ref, g1_ref, sc2_ref, sh2_ref, wo_ref,
                    lng_ref, lnb_ref, wr_hi_ref, wr_lo_ref, br_ref,
                    x1_ref, h2_ref, idx_ref, gate_ref):
    mix = (_dot(oa_ref[...], wo_ref[0:A_WIDTH, :])
           + _dot(ob_ref[...], wo_ref[A_WIDTH:A_WIDTH + B_WIDTH, :])
           + _dot(oc_ref[...], wo_ref[A_WIDTH + B_WIDTH:, :]))
    x1 = _layer_norm(DEEPNORM_ALPHA * x_ref[...] + g1_ref[...] * mix, lng_ref[...], lnb_ref[...])
    x1_ref[...] = x1
    h2 = x1 * (1.0 + sc2_ref[...]) + sh2_ref[...]
    h2_ref[...] = h2
    h_hi, h_lo = _split_bf16(h2)
    logits = (_dot(h_hi, wr_hi_ref[...]) + _dot(h_lo, wr_hi_ref[...])
              + _dot(h_hi, wr_lo_ref[...]) + br_ref[...])
    lane = lax.broadcasted_iota(jnp.int32, logits.shape, 1)
    cur = jnp.where(lane < N_EXPERTS, logits, NEG_INF)
    vals, idxs = [], []
    for _ in range(TOP_K):
        m = jnp.max(cur, axis=-1, keepdims=True)
        i = jnp.min(jnp.where(cur == m, lane, LANES), axis=-1, keepdims=True)
        vals.append(m)
        idxs.append(i)
        cur = jnp.where(lane == i, NEG_INF, cur)
    es = [jnp.exp(v - vals[0]) for v in vals]
    den = es[0] + es[1] + es[2] + es[3]
    idx_out = jnp.zeros(logits.shape, jnp.int32)
    gate_out = jnp.zeros(logits.shape, F32)
    for k in range(TOP_K):
        idx_out = jnp.where(lane == k, idxs[k], idx_out)
        gate_out = jnp.where(lane == k, es[k] / den, gate_out)
    idx_ref[...] = idx_out
    gate_ref[...] = gate_out


def _outproj(oa, ob, oc, x, g1, sc2, sh2, wo_bf, ln_g, ln_b, wr_hi, wr_lo, br,
             *, tm, rows_per_batch, batch_off):
    m, d = x.shape
    bpb = rows_per_batch // tm
    row = lambda w: pl.BlockSpec((tm, w), lambda i: (i, 0))
    mod_spec = pl.BlockSpec((None, 1, d), lambda i: (i // bpb + batch_off, 0, 0))
    const = lambda a: pl.BlockSpec(a.shape, lambda i: (0, 0))
    return pl.pallas_call(
        _outproj_kernel,
        out_shape=[jax.ShapeDtypeStruct((m, d), F32), jax.ShapeDtypeStruct((m, d), F32),
                   jax.ShapeDtypeStruct((m, LANES), jnp.int32),
                   jax.ShapeDtypeStruct((m, LANES), F32)],
        grid=(m // tm,),
        in_specs=[row(A_WIDTH), row(B_WIDTH), row(C_CH), row(d), mod_spec, mod_spec, mod_spec,
                  const(wo_bf), const(ln_g), const(ln_b), const(wr_hi), const(wr_lo), const(br)],
        out_specs=[row(d), row(d), row(LANES), row(LANES)],
        compiler_params=_cparams(1, VMEM_LIMIT),
        name="outproj_route",
    )(oa, ob, oc, x, g1, sc2, sh2, wo_bf, ln_g, ln_b, wr_hi, wr_lo, br)


def _moe_kernel(be_ref, nu_ref, idx_hbm, h2_hbm, w1_ref, b1_ref, w2_ref, b2_ref,
                picked_hbm, idx_smem, xbuf, obuf, w1b, w2b, sem_idx, sem_g, sem_s, *, tm):
    i = pl.program_id(0)

    @pl.when(i < nu_ref[0])
    def _():
        cp = pltpu.make_async_copy(
            idx_hbm.at[pl.ds(pl.multiple_of(i * IDX_STRIDE, IDX_STRIDE), IDX_STRIDE)],
            idx_smem, sem_idx)
        cp.start()
        cp.wait()

        def gather(r, carry):
            tok = idx_smem[r]
            pltpu.make_async_copy(h2_hbm.at[pl.ds(tok, 1), :], xbuf.at[pl.ds(r, 1), :],
                                  sem_g).start()
            return carry
        lax.fori_loop(0, tm, gather, 0)

        prev = jnp.maximum(i - 1, 0)

        @pl.when(jnp.logical_or(i == 0, be_ref[i] != be_ref[prev]))
        def _():
            w1b[...] = w1_ref[...].astype(BF16)
            w2b[...] = w2_ref[...].astype(BF16)

        def gather_wait(r, carry):
            pltpu.make_async_copy(h2_hbm.at[pl.ds(0, 1), :], xbuf.at[pl.ds(r, 1), :],
                                  sem_g).wait()
            return carry
        lax.fori_loop(0, tm, gather_wait, 0)

        x = xbuf[...].astype(BF16)
        x_glu = jnp.minimum(_dot(x, w1b[:, 0:D_FF]) + b1_ref[:, 0:D_FF], SWIGLU_LIMIT)
        x_lin = jnp.clip(_dot(x, w1b[:, D_FF:]) + b1_ref[:, D_FF:], -SWIGLU_LIMIT, SWIGLU_LIMIT)
        act = x_glu * jax.nn.sigmoid(SWIGLU_ALPHA * x_glu) * (x_lin + 1.0)
        obuf[...] = _dot(act.astype(BF16), w2b[...]) + b2_ref[...]

        def scatter(r, carry):
            dst = idx_smem[tm + r]
            pltpu.make_async_copy(obuf.at[pl.ds(r, 1), :], picked_hbm.at[pl.ds(dst, 1), :],
                                  sem_s).start()
            return carry
        lax.fori_loop(0, tm, scatter, 0)

        def scatter_wait(r, carry):
            pltpu.make_async_copy(obuf.at[pl.ds(r, 1), :], picked_hbm.at[pl.ds(0, 1), :],
                                  sem_s).wait()
            return carry
        lax.fori_loop(0, tm, scatter_wait, 0)

    @pl.when(i >= nu_ref[0])
    def _():
        @pl.when(i == nu_ref[0])
        def _():
            obuf[...] = jnp.zeros(obuf.shape, F32)
        cp = pltpu.make_async_copy(
            obuf, picked_hbm.at[pl.ds(pl.multiple_of(i * tm, tm), tm), :], sem_s)
        cp.start()
        cp.wait()


def _moe(block_e, n_used, idx_flat, h2, w1, b1, w2, b2, *, n_rows):
    tm = MOE_TILE
    n_blocks = block_e.shape[0]
    d = h2.shape[1]
    grid_spec = pltpu.PrefetchScalarGridSpec(
        num_scalar_prefetch=2,
        grid=(n_blocks,),
        in_specs=[pl.BlockSpec(memory_space=pl.ANY),
                  pl.BlockSpec(memory_space=pl.ANY),
                  pl.BlockSpec((None, d, 2 * D_FF), lambda i, be, nu: (be[i], 0, 0)),
                  pl.BlockSpec((None, 1, 2 * D_FF), lambda i, be, nu: (be[i], 0, 0)),
                  pl.BlockSpec((None, D_FF, d), lambda i, be, nu: (be[i], 0, 0)),
                  pl.BlockSpec((None, 1, d), lambda i, be, nu: (be[i], 0, 0))],
        out_specs=pl.BlockSpec(memory_space=pl.ANY),
        scratch_shapes=[pltpu.SMEM((IDX_STRIDE,), jnp.int32),
                        pltpu.VMEM((tm, d), F32), pltpu.VMEM((tm, d), F32),
                        pltpu.VMEM((d, 2 * D_FF), BF16), pltpu.VMEM((D_FF, d), BF16),
                        pltpu.SemaphoreType.DMA, pltpu.SemaphoreType.DMA,
                        pltpu.SemaphoreType.DMA])
    return pl.pallas_call(
        functools.partial(_moe_kernel, tm=tm),
        out_shape=jax.ShapeDtypeStruct((n_rows, d), F32),
        grid_spec=grid_spec,
        compiler_params=_cparams(1, VMEM_LIMIT),
        name="moe_experts",
    )(block_e, n_used, idx_flat, h2, w1, b1.reshape(N_EXPERTS, 1, -1), w2,
      b2.reshape(N_EXPERTS, 1, -1))


def _routing_tables(top_idx, n_tok):
    tm = MOE_TILE
    n_assign = n_tok * TOP_K
    n_blocks = -(-n_assign // tm) + N_EXPERTS
    n_rows = n_blocks * tm
    flat_e = top_idx.reshape(n_assign)
    order = jnp.argsort(flat_e, stable=True).astype(jnp.int32)
    sorted_e = flat_e[order]
    counts = jnp.bincount(flat_e, length=N_EXPERTS).astype(jnp.int32)
    padded = (counts + tm - 1) // tm * tm
    pad_end = jnp.cumsum(padded)
    pad_start = pad_end - padded
    grp_start = jnp.cumsum(counts) - counts
    slot = pad_start[sorted_e] + jnp.arange(n_assign, dtype=jnp.int32) - grp_start[sorted_e]
    tok = order // TOP_K
    dst = (order % TOP_K) * n_tok + tok
    row_tok = jnp.zeros((n_rows,), jnp.int32).at[slot].set(tok)
    is_pad = jnp.ones((n_rows,), jnp.int32).at[slot].set(0)
    trash = n_assign + jnp.cumsum(is_pad) - 1
    row_dst = trash.astype(jnp.int32).at[slot].set(dst)
    n_used = (pad_end[-1] // tm).astype(jnp.int32)
    blk = jnp.minimum(jnp.arange(n_blocks, dtype=jnp.int32), n_used - 1)
    block_e = jnp.minimum(jnp.searchsorted(pad_end, blk * tm, side='right'),
                          N_EXPERTS - 1).astype(jnp.int32)
    idx = jnp.zeros((n_blocks, IDX_STRIDE), jnp.int32)
    idx = idx.at[:, 0:tm].set(row_tok.reshape(n_blocks, tm))
    idx = idx.at[:, tm:2 * tm].set(row_dst.reshape(n_blocks, tm))
    return block_e, n_used.reshape(1), idx.reshape(-1), n_rows


def _combine_kernel(p0_ref, p1_ref, p2_ref, p3_ref, gate_ref, x1_ref, g2_ref,
                    lng_ref, lnb_ref, o_ref):
    gate = gate_ref[...]
    y = (gate[:, 0:1] * p0_ref[...] + gate[:, 1:2] * p1_ref[...]
         + gate[:, 2:3] * p2_ref[...] + gate[:, 3:4] * p3_ref[...])
    o_ref[...] = _layer_norm(DEEPNORM_ALPHA * x1_ref[...] + g2_ref[...] * y,
                             lng_ref[...], lnb_ref[...])


def _combine(picked, gate, x1, g2, ln_g, ln_b, *, tm, rows_per_batch, batch_off, row_off, n_tok):
    m, d = x1.shape
    bpb = rows_per_batch // tm
    blk_off = row_off // tm
    k_stride = n_tok // tm
    pspec = lambda k: pl.BlockSpec((tm, d), lambda i: (i + blk_off + k * k_stride, 0))
    row = lambda w: pl.BlockSpec((tm, w), lambda i: (i, 0))
    const = lambda a: pl.BlockSpec(a.shape, lambda i: (0, 0))
    return pl.pallas_call(
        _combine_kernel,
        out_shape=jax.ShapeDtypeStruct((m, d), F32),
        grid=(m // tm,),
        in_specs=[pspec(0), pspec(1), pspec(2), pspec(3), row(LANES), row(d),
                  pl.BlockSpec((None, 1, d), lambda i: (i // bpb + batch_off, 0, 0)),
                  const(ln_g), const(ln_b)],
        out_specs=row(d),
        compiler_params=_cparams(1, VMEM_LIMIT),
        name="combine_ln",
    )(picked, picked, picked, picked, gate, x1, g2, ln_g, ln_b)


def _band_rel(n_prev, cq):
    return np.arange((n_prev + 1) * cq)[None, :] - n_prev * cq - np.arange(cq)[:, None]


def _cache_rel(n_cache, s):
    return np.concatenate([np.arange(n_cache) - n_cache, np.arange(s)])[None, :] - np.arange(s)[:, None]


def _clipped_rel_bias(rel, table):
    idx = np.clip(rel, -A_REL_CLIP, A_REL_CLIP) + A_REL_CLIP
    return jnp.moveaxis(table[idx].astype(F32), -1, 0)


def _t5_bucket(rel):
    nb = T5_BUCKETS // 2
    max_exact = nb // 2
    n = np.abs(rel)
    nf = np.maximum(n, 1).astype(np.float32)
    large = max_exact + (np.log(nf / max_exact) / math.log(T5_MAX_DISTANCE / max_exact)
                         * (nb - max_exact)).astype(np.int32)
    large = np.minimum(large, nb - 1)
    return np.where(rel > 0, nb, 0) + np.where(n < max_exact, n, large)


def _t5_rel_bias(rel, table):
    return jnp.moveaxis(table[_t5_bucket(rel)].astype(F32), -1, 0)


def kernel(x_prompt, x_sample, cache_a_k, cache_a_v, cache_b_k, cache_b_v, state_conv,
           c_prompt, c_sample, w_in, w_out, rel_bias_a, t5_bias, sinks, conv_w, conv_b,
           conv_ln_g, conv_ln_b, w_ada, b_ada, ln_g, ln_b, w_router, b_router,
           w_e_in, b_e_in, w_e_out, b_e_out):
    bp, tp, d = x_prompt.shape
    bs, ts, _ = x_sample.shape
    mp, ms = bp * tp, bs * ts
    n_tok = mp + ms
    na, nb = cache_a_k.shape[2], cache_b_k.shape[2]
    assert tp % ROW_TILE == 0 and mp % ts == 0 and ts >= CONV_W - 1 and tp >= A_REACH

    xp = x_prompt.reshape(mp, d)
    xs = x_sample.reshape(ms, d)
    c_all = jnp.concatenate([c_prompt, c_sample], axis=0)

    t5_p = _t5_rel_bias(_band_rel(B_PREV_CHUNKS, CHUNK), t5_bias)
    t5_s = _t5_rel_bias(_cache_rel(nb, ts), t5_bias)
    conv_zero = jnp.zeros((bp, CONV_HALO, C_CH), F32)

    states_p, states_s = [], []
    for l in range(DEPTH):
        mod = _adaln(c_all, w_ada[l], b_ada[l])
        sh1, sc1, g1, sh2, sc2, g2 = (mod[:, j * d:(j + 1) * d].reshape(bp + bs, 1, d)
                                      for j in range(6))
        w_in_bf = w_in[l].astype(BF16)
        w_out_bf = w_out[l].astype(BF16)
        wr = jnp.pad(w_router[l], ((0, 0), (0, LANES - N_EXPERTS)))
        wr_hi = wr.astype(BF16)
        wr_lo = (wr - wr_hi.astype(F32)).astype(BF16)
        br = jnp.pad(b_router[l], (0, LANES - N_EXPERTS)).reshape(1, LANES)
        lng1, lnb1 = ln_g[l, 0].reshape(1, d), ln_b[l, 0].reshape(1, d)
        lng2, lnb2 = ln_g[l, 1].reshape(1, d), ln_b[l, 1].reshape(1, d)
        bias_a_p = _clipped_rel_bias(_band_rel(A_PREV_CHUNKS, CHUNK), rel_bias_a[l])
        bias_a_s = _clipped_rel_bias(_cache_rel(na, ts), rel_bias_a[l])
        conv_args = (conv_w[l], conv_b[l], conv_ln_g[l], conv_ln_b[l])

        qa, ka, va, qb, kb, vb, u = _inproj(xp, sc1, sh1, w_in_bf, tm=ROW_TILE,
                                            rows_per_batch=tp, batch_off=0)
        r3 = lambda a: a.reshape(bp, tp, a.shape[-1])
        oa = _attention(r3(qa), r3(ka), r3(va), bias_a_p, None,
                        cq=CHUNK, n_prev=A_PREV_CHUNKS, n_heads=A_HEADS, group=1)
        ob = _attention(r3(qb), r3(kb), r3(vb), t5_p, sinks[l],
                        cq=CHUNK, n_prev=B_PREV_CHUNKS, n_heads=B_HEADS, group=B_GROUP)
        oc = _conv_tail(conv_zero, r3(u), *conv_args)
        x1p, h2p, idxp, gatep = _outproj(
            oa.reshape(mp, -1), ob.reshape(mp, -1), oc.reshape(mp, -1), xp, g1, sc2, sh2,
            w_out_bf, lng1, lnb1, wr_hi, wr_lo, br, tm=ROW_TILE, rows_per_batch=tp, batch_off=0)
        states_p.append((
            r3(ka)[:, tp - A_REACH:].reshape(bp, A_REACH, A_HEADS, HEAD_DIM),
            r3(va)[:, tp - A_REACH:].reshape(bp, A_REACH, A_HEADS, HEAD_DIM),
            r3(kb)[:, tp - B_WINDOW:].reshape(bp, B_WINDOW, B_KV_HEADS, HEAD_DIM),
            r3(vb)[:, tp - B_WINDOW:].reshape(bp, B_WINDOW, B_KV_HEADS, HEAD_DIM),
            r3(u)[:, tp - (CONV_W - 1):]))

        qa, ka, va, qb, kb, vb, u = _inproj(xs, sc1, sh1, w_in_bf, tm=ts,
                                            rows_per_batch=ts, batch_off=bp)
        s3 = lambda a: a.reshape(bs, ts, a.shape[-1])
        ka_all = jnp.concatenate([cache_a_k[l].reshape(bs, na, A_WIDTH), s3(ka)], axis=1)
        va_all = jnp.concatenate([cache_a_v[l].reshape(bs, na, A_WIDTH), s3(va)], axis=1)
        kb_all = jnp.concatenate([cache_b_k[l].reshape(bs, nb, B_KV_WIDTH), s3(kb)], axis=1)
        vb_all = jnp.concatenate([cache_b_v[l].reshape(bs, nb, B_KV_WIDTH), s3(vb)], axis=1)
        oa = _attention(s3(qa), ka_all, va_all, bias_a_s, None,
                        cq=ts, n_prev=None, n_heads=A_HEADS, group=1)
        ob = _attention(s3(qb), kb_all, vb_all, t5_s, sinks[l],
                        cq=ts, n_prev=None, n_heads=B_HEADS, group=B_GROUP)
        prev = jnp.pad(state_conv[l], ((0, 0), (CONV_HALO - (CONV_W - 1), 0), (0, 0)))
        oc = _conv_tail(prev, s3(u), *conv_args)
        x1s, h2s, idxs, gates = _outproj(
            oa.reshape(ms, -1), ob.reshape(ms, -1), oc.reshape(ms, -1), xs, g1, sc2, sh2,
            w_out_bf, lng1, lnb1, wr_hi, wr_lo, br, tm=ts, rows_per_batch=ts, batch_off=bp)
        u_ext = jnp.concatenate([state_conv[l], s3(u)], axis=1)
        states_s.append((
            ka_all[:, -na:].reshape(bs, na, A_HEADS, HEAD_DIM),
            va_all[:, -na:].reshape(bs, na, A_HEADS, HEAD_DIM),
            kb_all[:, -nb:].reshape(bs, nb, B_KV_HEADS, HEAD_DIM),
            vb_all[:, -nb:].reshape(bs, nb, B_KV_HEADS, HEAD_DIM),
            u_ext[:, -(CONV_W - 1):]))

        h2 = jnp.concatenate([h2p, h2s], axis=0)
        top_idx = jnp.concatenate([idxp[:, :TOP_K], idxs[:, :TOP_K]], axis=0)
        block_e, n_used, idx_flat, n_rows = _routing_tables(top_idx, n_tok)
        picked = _moe(block_e, n_used, idx_flat, h2, w_e_in[l], b_e_in[l],
                      w_e_out[l], b_e_out[l], n_rows=n_rows)
        xp = _combine(picked, gatep, x1p, g2, lng2, lnb2, tm=ROW_TILE, rows_per_batch=tp,
                      batch_off=0, row_off=0, n_tok=n_tok)
        xs = _combine(picked, gates, x1s, g2, lng2, lnb2, tm=ts, rows_per_batch=ts,
                      batch_off=bp, row_off=mp, n_tok=n_tok)

    a_k_p, a_v_p, b_k_p, b_v_p, conv_p = (jnp.stack(z) for z in zip(*states_p))
    a_k_s, a_v_s, b_k_s, b_v_s, conv_s = (jnp.stack(z) for z in zip(*states_s))
    return (xp.reshape(bp, tp, d), xs.reshape(bs, ts, d), a_k_p, a_v_p, b_k_p, b_v_p, conv_p,
            a_k_s, a_v_s, b_k_s, b_v_s, conv_s)
```

```python
import functools
import math

import jax
import jax.numpy as jnp
import numpy as np
from jax import lax
from jax.experimental import pallas as pl
from jax.experimental.pallas import tpu as pltpu

F32 = jnp.float32
BF16 = jnp.bfloat16

D_MODEL = 1024
DEPTH = 2
CHUNK = 64
HEAD_DIM = 64
ATTN_SCALE = HEAD_DIM ** -0.5
A_HEADS = 4
A_WIDTH = A_HEADS * HEAD_DIM
A_PREV_CHUNKS = 8
A_REACH = A_PREV_CHUNKS * CHUNK
A_REL_CLIP = 128
B_HEADS = 8
B_KV_HEADS = 2
B_GROUP = B_HEADS // B_KV_HEADS
B_WIDTH = B_HEADS * HEAD_DIM
B_KV_WIDTH = B_KV_HEADS * HEAD_DIM
B_WINDOW = 128
B_PREV_CHUNKS = B_WINDOW // CHUNK
T5_BUCKETS = 32
T5_MAX_DISTANCE = 128
C_CH = D_MODEL // 4
CONV_W = 31
CONV_HALO = 32
N_EXPERTS = 32
TOP_K = 4
D_FF = D_MODEL
SWIGLU_LIMIT = 7.0
SWIGLU_ALPHA = 1.702
DEEPNORM_ALPHA = (2 * DEPTH) ** 0.25
LN_EPS = 1e-5
NEG_INF = -1e30

LANES = 128
ROW_TILE = 512
MOE_TILE = 256
IDX_STRIDE = 1024
MOE_IDX_SLOTS = 4
VMEM_LIMIT = 56 * 1024 * 1024


def _cparams(n_axes=1, vmem=None):
    return pltpu.CompilerParams(dimension_semantics=("arbitrary",) * n_axes,
                                vmem_limit_bytes=vmem)


def _dot(a, b):
    return jnp.dot(a, b, preferred_element_type=F32)


def _layer_norm(z, g, b):
    mu = jnp.mean(z, axis=-1, keepdims=True)
    d = z - mu
    var = jnp.mean(d * d, axis=-1, keepdims=True)
    return d * lax.rsqrt(var + LN_EPS) * g + b


def _split_bf16(a):
    hi = a.astype(BF16)
    lo = (a - hi.astype(F32)).astype(BF16)
    return hi, lo


def _adaln_kernel(c_ref, w_ref, b_ref, o_ref):
    c = c_ref[...]
    a_hi, a_lo = _split_bf16(c * jax.nn.sigmoid(c))
    w_hi, w_lo = _split_bf16(w_ref[...])
    o_ref[...] = _dot(a_hi, w_hi) + _dot(a_lo, w_hi) + _dot(a_hi, w_lo) + b_ref[...]


def _adaln(c_all, w, b):
    nb, d = c_all.shape
    n = w.shape[1]
    tn = 1536
    return pl.pallas_call(
        _adaln_kernel,
        out_shape=jax.ShapeDtypeStruct((nb, n), F32),
        grid=(n // tn,),
        in_specs=[pl.BlockSpec((nb, d), lambda j: (0, 0)),
                  pl.BlockSpec((d, tn), lambda j: (0, j)),
                  pl.BlockSpec((1, tn), lambda j: (0, j))],
        out_specs=pl.BlockSpec((nb, tn), lambda j: (0, j)),
        compiler_params=_cparams(1, VMEM_LIMIT),
        name="adaln",
    )(c_all, w, b.reshape(1, n))


_QA = (0, A_WIDTH)
_KA = (A_WIDTH, 2 * A_WIDTH)
_VA = (2 * A_WIDTH, 3 * A_WIDTH)
_QB = (3 * A_WIDTH, 3 * A_WIDTH + B_WIDTH)
_KB = (_QB[1], _QB[1] + B_KV_WIDTH)
_VB = (_KB[1], _KB[1] + B_KV_WIDTH)
_GA = (_VB[1], _VB[1] + C_CH)
_GG = (_GA[1], _GA[1] + C_CH)


def _inproj_kernel(x_ref, sc_ref, sh_ref, w_ref,
                   qa_ref, ka_ref, va_ref, qb_ref, kb_ref, vb_ref, u_ref):
    h = (x_ref[...] * (1.0 + sc_ref[...]) + sh_ref[...]).astype(BF16)

    def proj(cols):
        return _dot(h, w_ref[:, cols[0]:cols[1]])

    qa_ref[...] = (proj(_QA) * ATTN_SCALE).astype(BF16)
    ka_ref[...] = proj(_KA)
    va_ref[...] = proj(_VA)
    qb_ref[...] = (proj(_QB) * ATTN_SCALE).astype(BF16)
    kb_ref[...] = proj(_KB)
    vb_ref[...] = proj(_VB)
    u_ref[...] = proj(_GA) * jax.nn.sigmoid(proj(_GG))


def _inproj(x, sc, sh, w_bf, *, tm, rows_per_batch, batch_off):
    m, d = x.shape
    bpb = rows_per_batch // tm
    mod_spec = pl.BlockSpec((None, 1, d), lambda i: (i // bpb + batch_off, 0, 0))
    widths = (A_WIDTH, A_WIDTH, A_WIDTH, B_WIDTH, B_KV_WIDTH, B_KV_WIDTH, C_CH)
    dtypes = (BF16, F32, F32, BF16, F32, F32, F32)
    return pl.pallas_call(
        _inproj_kernel,
        out_shape=[jax.ShapeDtypeStruct((m, w), dt) for w, dt in zip(widths, dtypes)],
        grid=(m // tm,),
        in_specs=[pl.BlockSpec((tm, d), lambda i: (i, 0)), mod_spec, mod_spec,
                  pl.BlockSpec(w_bf.shape, lambda i: (0, 0))],
        out_specs=[pl.BlockSpec((tm, w), lambda i: (i, 0)) for w in widths],
        compiler_params=_cparams(1, VMEM_LIMIT),
        name="inproj",
    )(x, sc, sh, w_bf)


def _attn_kernel(*refs, nc, cq, bw, pad, n_heads, group, use_sink):
    if use_sink:
        q_ref, k_ref, v_ref, bias_ref, sink_ref, o_ref, kp_ref, vp_ref = refs
    else:
        q_ref, k_ref, v_ref, bias_ref, o_ref, kp_ref, vp_ref = refs
        sink_ref = None
    tk = k_ref.shape[0]
    if pad:
        zeros = jnp.zeros((pad, kp_ref.shape[1]), BF16)
        kp_ref[0:pad, :] = zeros
        vp_ref[0:pad, :] = zeros
    kp_ref[pad:pad + tk, :] = k_ref[...].astype(BF16)
    vp_ref[pad:pad + tk, :] = v_ref[...].astype(BF16)

    def chunk(c):
        q0 = c * cq
        if not isinstance(c, int):
            q0 = pl.multiple_of(q0, cq)
        qt = q_ref[pl.ds(q0, cq), :]
        kt = kp_ref[pl.ds(q0, bw), :]
        vt = vp_ref[pl.ds(q0, bw), :]
        if pad:
            key_pos = lax.broadcasted_iota(jnp.int32, (cq, bw), 1)
            valid = key_pos >= pad - q0
        outs = []
        for h in range(n_heads):
            n = h // group
            qh = qt[:, h * HEAD_DIM:(h + 1) * HEAD_DIM]
            kh = kt[:, n * HEAD_DIM:(n + 1) * HEAD_DIM]
            vh = vt[:, n * HEAD_DIM:(n + 1) * HEAD_DIM]
            s = lax.dot_general(qh, kh, (((1,), (1,)), ((), ())),
                                preferred_element_type=F32)
            s = s + bias_ref[h]
            if pad:
                s = jnp.where(valid, s, NEG_INF)
            m = jnp.max(s, axis=-1, keepdims=True)
            if use_sink:
                m = jnp.maximum(m, sink_ref[h])
            e = jnp.exp(s - m)
            den = jnp.sum(e, axis=-1, keepdims=True)
            if use_sink:
                den = den + jnp.exp(sink_ref[h] - m)
            outs.append(_dot(e.astype(BF16), vh) / den)
        o_ref[pl.ds(q0, cq), :] = jnp.concatenate(outs, axis=-1).astype(o_ref.dtype)

    if nc == 1:
        chunk(0)
    else:
        def body(c, carry):
            chunk(c)
            return carry
        lax.fori_loop(0, nc, body, 0)


def _attention(q, k, v, bias, sinks, *, cq, n_prev, n_heads, group):
    b, t, qw = q.shape
    tk, kw = k.shape[1], k.shape[2]
    if n_prev is None:
        nc, bw, pad = 1, tk, 0
    else:
        nc, bw, pad = t // cq, (n_prev + 1) * cq, n_prev * cq
    use_sink = sinks is not None
    kern = functools.partial(_attn_kernel, nc=nc, cq=cq, bw=bw, pad=pad,
                             n_heads=n_heads, group=group, use_sink=use_sink)
    in_specs = [pl.BlockSpec((None, t, qw), lambda i: (i, 0, 0)),
                pl.BlockSpec((None, tk, kw), lambda i: (i, 0, 0)),
                pl.BlockSpec((None, tk, kw), lambda i: (i, 0, 0)),
                pl.BlockSpec(bias.shape, lambda i: (0, 0, 0))]
    args = [q, k, v, bias]
    if use_sink:
        in_specs.append(pl.BlockSpec(memory_space=pltpu.SMEM))
        args.append(sinks)
    return pl.pallas_call(
        kern,
        out_shape=jax.ShapeDtypeStruct((b, t, qw), BF16),
        grid=(b,),
        in_specs=in_specs,
        out_specs=pl.BlockSpec((None, t, qw), lambda i: (i, 0, 0)),
        scratch_shapes=[pltpu.VMEM((pad + tk, kw), BF16), pltpu.VMEM((pad + tk, kw), BF16)],
        compiler_params=_cparams(1, VMEM_LIMIT),
        name="attn_sink" if use_sink else "attn",
    )(*args)


def _conv_kernel(prev_ref, u_ref, w_ref, cb_ref, g_ref, b_ref, o_ref, up_ref, *, t, tt):
    up_ref[0:CONV_HALO, :] = prev_ref[...]
    up_ref[CONV_HALO:CONV_HALO + t, :] = u_ref[...]
    lead = CONV_HALO - (CONV_W - 1)

    def tile(t0):
        win = up_ref[pl.ds(t0, tt + CONV_HALO), :]
        acc = jnp.zeros((tt, C_CH), F32)
        for j in range(CONV_W):
            acc = acc + win[j + lead:j + lead + tt, :] * w_ref[j:j + 1, :]
        y = _layer_norm(acc + cb_ref[...], g_ref[...], b_ref[...])
        o_ref[pl.ds(t0, tt), :] = (y * jax.nn.sigmoid(y)).astype(o_ref.dtype)

    if t == tt:
        tile(0)
    else:
        def body(i, carry):
            tile(pl.multiple_of(i * tt, tt))
            return carry
        lax.fori_loop(0, t // tt, body, 0)


def _conv_tail(prev, u, conv_w, conv_b, ln_g, ln_b):
    b, t, c = u.shape
    tt = min(t, 128)
    vec = lambda a: a.reshape(1, c)
    vspec = pl.BlockSpec((1, c), lambda i: (0, 0))
    return pl.pallas_call(
        functools.partial(_conv_kernel, t=t, tt=tt),
        out_shape=jax.ShapeDtypeStruct((b, t, c), BF16),
        grid=(b,),
        in_specs=[pl.BlockSpec((None, CONV_HALO, c), lambda i: (i, 0, 0)),
                  pl.BlockSpec((None, t, c), lambda i: (i, 0, 0)),
                  pl.BlockSpec((CONV_W, c), lambda i: (0, 0)),
                  vspec, vspec, vspec],
        out_specs=pl.BlockSpec((None, t, c), lambda i: (i, 0, 0)),
        scratch_shapes=[pltpu.VMEM((CONV_HALO + t, c), F32)],
        compiler_params=_cparams(1, VMEM_LIMIT),
        name="conv_tail",
    )(prev, u, conv_w, vec(conv_b), vec(ln_g), vec(ln_b))


def _outproj_tail_kernel(*refs, n_blk):
    tail_ref, h2_ref = refs[13], refs[15]
    i = pl.program_id(0)

    @pl.when(i < n_blk)
    def _():
        _outproj_kernel(*refs[:13], *refs[14:])

    @pl.when(i == n_blk)
    def _():
        h2_ref[...] = tail_ref[...]


def _outproj_kernel(oa_ref, ob_ref, oc_ref, x_ref, g1_ref, sc2_ref, sh2_ref, wo_ref,
                    lng_ref, lnb_ref, wr_hi_ref, wr_lo_ref, br_ref,
                    x1_ref, h2_ref, idx_ref, gate_ref):
    mix = (_dot(oa_ref[...], wo_ref[0:A_WIDTH, :])
           + _dot(ob_ref[...], wo_ref[A_WIDTH:A_WIDTH + B_WIDTH, :])
           + _dot(oc_ref[...], wo_ref[A_WIDTH + B_WIDTH:, :]))
    x1 = _layer_norm(DEEPNORM_ALPHA * x_ref[...] + g1_ref[...] * mix, lng_ref[...], lnb_ref[...])
    x1_ref[...] = x1
    h2 = x1 * (1.0 + sc2_ref[...]) + sh2_ref[...]
    h2_ref[...] = h2
    h_hi, h_lo = _split_bf16(h2)
    logits = (_dot(h_hi, wr_hi_ref[...]) + _dot(h_lo, wr_hi_ref[...])
              + _dot(h_hi, wr_lo_ref[...]) + br_ref[...])
    lane = lax.broadcasted_iota(jnp.int32, logits.shape, 1)
    cur = jnp.where(lane < N_EXPERTS, logits, NEG_INF)
    vals, idxs = [], []
    for _ in range(TOP_K):
        m = jnp.max(cur, axis=-1, keepdims=True)
        i = jnp.min(jnp.where(cur == m, lane, LANES), axis=-1, keepdims=True)
        vals.append(m)
        idxs.append(i)
        cur = jnp.where(lane == i, NEG_INF, cur)
    es = [jnp.exp(v - vals[0]) for v in vals]
    den = es[0] + es[1] + es[2] + es[3]
    idx_out = jnp.zeros(logits.shape, jnp.int32)
    gate_out = jnp.zeros(logits.shape, F32)
    for k in range(TOP_K):
        idx_out = jnp.where(lane == k, idxs[k], idx_out)
        gate_out = jnp.where(lane == k, es[k] / den, gate_out)
    idx_ref[...] = idx_out
    gate_ref[...] = gate_out


def _outproj(oa, ob, oc, x, g1, sc2, sh2, wo_bf, ln_g, ln_b, wr_hi, wr_lo, br,
             *, tm, rows_per_batch, batch_off, h2_tail=None):
    m, d = x.shape
    bpb = rows_per_batch // tm
    n_blk = m // tm
    last = n_blk - 1
    row = lambda w: pl.BlockSpec((tm, w), lambda i: (jnp.minimum(i, last), 0))
    mod_spec = pl.BlockSpec((None, 1, d),
                            lambda i: (jnp.minimum(i, last) // bpb + batch_off, 0, 0))
    const = lambda a: pl.BlockSpec(a.shape, lambda i: (0, 0))
    in_specs = [row(A_WIDTH), row(B_WIDTH), row(C_CH), row(d), mod_spec, mod_spec, mod_spec,
                const(wo_bf), const(ln_g), const(ln_b), const(wr_hi), const(wr_lo), const(br)]
    args = [oa, ob, oc, x, g1, sc2, sh2, wo_bf, ln_g, ln_b, wr_hi, wr_lo, br]
    if h2_tail is None:
        kern, steps, h2_rows = _outproj_kernel, n_blk, m
    else:
        assert h2_tail.shape == (tm, d)
        kern = functools.partial(_outproj_tail_kernel, n_blk=n_blk)
        steps, h2_rows = n_blk + 1, m + tm
        in_specs.append(const(h2_tail))
        args.append(h2_tail)
    return pl.pallas_call(
        kern,
        out_shape=[jax.ShapeDtypeStruct((m, d), F32), jax.ShapeDtypeStruct((h2_rows, d), F32),
                   jax.ShapeDtypeStruct((m, LANES), jnp.int32),
                   jax.ShapeDtypeStruct((m, LANES), F32)],
        grid=(steps,),
        in_specs=in_specs,
        out_specs=[row(d), pl.BlockSpec((tm, d), lambda i: (i, 0)), row(LANES), row(LANES)],
        compiler_params=_cparams(1, VMEM_LIMIT),
        name="outproj_route",
    )(*args)


def _moe_kernel(be_ref, nu_ref, idx_hbm, h2_hbm, w1_ref, b1_ref, w2_ref, b2_ref,
                picked_hbm, idx_smem, xbuf, obuf, w1b, w2b, sem_idx, sem_g, sem_s, *, tm):
    i = pl.program_id(0)
    nu = nu_ref[0]

    def table_copy(blk, slot):
        return pltpu.make_async_copy(
            idx_hbm.at[pl.ds(pl.multiple_of(blk * IDX_STRIDE, IDX_STRIDE), IDX_STRIDE)],
            idx_smem.at[pl.ds(pl.multiple_of(slot * IDX_STRIDE, IDX_STRIDE), IDX_STRIDE)],
            sem_idx.at[slot])

    def row_loop(body, static_rows):
        if static_rows:
            for r in range(tm):
                body(r)
        else:
            lax.fori_loop(0, tm, lambda r, c: (body(r), c)[1], 0)

    def start_gathers(tslot, bslot, static_rows=True):
        base = tslot * IDX_STRIDE

        def body(r):
            tok = idx_smem[base + r]
            pltpu.make_async_copy(h2_hbm.at[pl.ds(tok, 1), :],
                                  xbuf.at[bslot, pl.ds(r, 1), :], sem_g.at[bslot]).start()
        row_loop(body, static_rows)

    def wait_gathers(bslot):
        pltpu.make_async_copy(h2_hbm.at[pl.ds(0, tm), :], xbuf.at[bslot], sem_g.at[bslot]).wait()

    def start_scatters(tslot, bslot):
        base = tslot * IDX_STRIDE + tm

        def body(r):
            dst = idx_smem[base + r]
            pltpu.make_async_copy(obuf.at[bslot, pl.ds(r, 1), :],
                                  picked_hbm.at[pl.ds(dst, 1), :], sem_s.at[bslot]).start()
        row_loop(body, True)

    def wait_scatters(bslot):
        pltpu.make_async_copy(obuf.at[bslot], picked_hbm.at[pl.ds(0, tm), :],
                              sem_s.at[bslot]).wait()

    @pl.when(i < nu)
    def _():
        bslot = i % 2
        tslot = i % MOE_IDX_SLOTS

        @pl.when(i == 0)
        def _():
            first = table_copy(0, 0)
            first.start()
            first.wait()
            start_gathers(0, 0, static_rows=False)

            @pl.when(nu > 1)
            def _():
                table_copy(1, 1).start()

        @pl.when(i + 2 < nu)
        def _():
            table_copy(i + 2, (i + 2) % MOE_IDX_SLOTS).start()

        @pl.when(i + 1 < nu)
        def _():
            nslot = (i + 1) % MOE_IDX_SLOTS
            table_copy(i + 1, nslot).wait()
            start_gathers(nslot, 1 - bslot)

        @pl.when(jnp.logical_or(i == 0, be_ref[i] != be_ref[jnp.maximum(i - 1, 0)]))
        def _():
            w1b[...] = w1_ref[...].astype(BF16)
            w2b[...] = w2_ref[...].astype(BF16)

        wait_gathers(bslot)

        @pl.when(i >= 2)
        def _():
            wait_scatters(bslot)

        x = xbuf[bslot].astype(BF16)
        x_glu = jnp.minimum(_dot(x, w1b[:, 0:D_FF]) + b1_ref[:, 0:D_FF], SWIGLU_LIMIT)
        x_lin = jnp.clip(_dot(x, w1b[:, D_FF:]) + b1_ref[:, D_FF:], -SWIGLU_LIMIT, SWIGLU_LIMIT)
        act = x_glu * jax.nn.sigmoid(SWIGLU_ALPHA * x_glu) * (x_lin + 1.0)
        obuf[bslot] = _dot(act.astype(BF16), w2b[...]) + b2_ref[...]
        start_scatters(tslot, bslot)

        @pl.when(i == nu - 1)
        def _():
            @pl.when(i >= 1)
            def _():
                wait_scatters(1 - bslot)
            wait_scatters(bslot)

    @pl.when(i >= nu)
    def _():
        @pl.when(i == nu)
        def _():
            obuf[0] = jnp.zeros(obuf.shape[1:], F32)
        cp = pltpu.make_async_copy(
            obuf.at[0], picked_hbm.at[pl.ds(pl.multiple_of(i * tm, tm), tm), :], sem_s.at[0])
        cp.start()
        cp.wait()


def _moe(block_e, n_used, idx_flat, h2, w1, b1, w2, b2, *, n_rows):
    tm = MOE_TILE
    n_blocks = block_e.shape[0]
    d = h2.shape[1]
    grid_spec = pltpu.PrefetchScalarGridSpec(
        num_scalar_prefetch=2,
        grid=(n_blocks,),
        in_specs=[pl.BlockSpec(memory_space=pl.ANY),
                  pl.BlockSpec(memory_space=pl.ANY),
                  pl.BlockSpec((None, d, 2 * D_FF), lambda i, be, nu: (be[i], 0, 0)),
                  pl.BlockSpec((None, 1, 2 * D_FF), lambda i, be, nu: (be[i], 0, 0)),
                  pl.BlockSpec((None, D_FF, d), lambda i, be, nu: (be[i], 0, 0)),
                  pl.BlockSpec((None, 1, d), lambda i, be, nu: (be[i], 0, 0))],
        out_specs=pl.BlockSpec(memory_space=pl.ANY),
        scratch_shapes=[pltpu.SMEM((MOE_IDX_SLOTS * IDX_STRIDE,), jnp.int32),
                        pltpu.VMEM((2, tm, d), F32), pltpu.VMEM((2, tm, d), F32),
                        pltpu.VMEM((d, 2 * D_FF), BF16), pltpu.VMEM((D_FF, d), BF16),
                        pltpu.SemaphoreType.DMA((MOE_IDX_SLOTS,)),
                        pltpu.SemaphoreType.DMA((2,)), pltpu.SemaphoreType.DMA((2,))])
    return pl.pallas_call(
        functools.partial(_moe_kernel, tm=tm),
        out_shape=jax.ShapeDtypeStruct((n_rows, d), F32),
        grid_spec=grid_spec,
        compiler_params=_cparams(1, VMEM_LIMIT),
        name="moe_experts",
    )(block_e, n_used, idx_flat, h2, w1, b1.reshape(N_EXPERTS, 1, -1), w2,
      b2.reshape(N_EXPERTS, 1, -1))


def _routing_tables(top_idx, n_tok):
    tm = MOE_TILE
    n_assign = n_tok * TOP_K
    n_blocks = -(-n_assign // tm) + N_EXPERTS
    n_rows = n_blocks * tm
    i32 = jnp.int32
    flat_e = top_idx.reshape(n_assign)
    key_bits = (n_assign - 1).bit_length()
    assert N_EXPERTS << key_bits < 2 ** 31
    keys = jnp.sort(flat_e * (1 << key_bits) + jnp.arange(n_assign, dtype=i32))
    order = keys & ((1 << key_bits) - 1)
    experts = jnp.arange(N_EXPERTS, dtype=i32)
    counts = jnp.sum((flat_e[:, None] == experts[None, :]).astype(i32), axis=0)
    padded = (counts + tm - 1) // tm * tm
    pad_end = jnp.cumsum(padded)
    pad_start = pad_end - padded
    grp_start = jnp.cumsum(counts) - counts
    n_used = pad_end[-1] // tm
    blk = jnp.arange(n_blocks, dtype=i32)
    used = blk < n_used
    expert_at = lambda start: jnp.minimum(
        jnp.sum((pad_end[None, :] <= start[:, None]).astype(i32), axis=1), N_EXPERTS - 1)
    e_blk = expert_at(blk * tm)
    e_last = expert_at(((n_used - 1) * tm).reshape(1))[0]
    block_e = jnp.where(used, e_blk, e_last)
    row = blk[:, None] * tm + jnp.arange(tm, dtype=i32)[None, :]
    off = row - pad_start[e_blk][:, None]
    cnt = counts[e_blk][:, None]
    grp = grp_start[e_blk][:, None]
    valid = used[:, None] & (off < cnt)
    a = order[jnp.clip(grp + off, 0, n_assign - 1)]
    tok = a // TOP_K
    row_tok = jnp.where(valid, tok, 0)
    real_before = jnp.where(used[:, None], grp + cnt, n_assign)
    row_dst = jnp.where(valid, (a % TOP_K) * n_tok + tok, n_assign + row - real_before)
    idx = jnp.concatenate(
        [row_tok, row_dst, jnp.zeros((n_blocks, IDX_STRIDE - 2 * tm), i32)], axis=1)
    return block_e, n_used.reshape(1).astype(i32), idx.reshape(-1), n_rows


def _combine_kernel(p0_ref, p1_ref, p2_ref, p3_ref, gate_ref, x1_ref, g2_ref,
                    lng_ref, lnb_ref, o_ref):
    gate = gate_ref[...]
    y = (gate[:, 0:1] * p0_ref[...] + gate[:, 1:2] * p1_ref[...]
         + gate[:, 2:3] * p2_ref[...] + gate[:, 3:4] * p3_ref[...])
    o_ref[...] = _layer_norm(DEEPNORM_ALPHA * x1_ref[...] + g2_ref[...] * y,
                             lng_ref[...], lnb_ref[...])


def _combine(picked, gate, x1, g2, ln_g, ln_b, *, tm, rows_per_batch, batch_off, row_off, n_tok):
    m, d = x1.shape
    bpb = rows_per_batch // tm
    blk_off = row_off // tm
    k_stride = n_tok // tm
    pspec = lambda k: pl.BlockSpec((tm, d), lambda i: (i + blk_off + k * k_stride, 0))
    row = lambda w: pl.BlockSpec((tm, w), lambda i: (i, 0))
    const = lambda a: pl.BlockSpec(a.shape, lambda i: (0, 0))
    return pl.pallas_call(
        _combine_kernel,
        out_shape=jax.ShapeDtypeStruct((m, d), F32),
        grid=(m // tm,),
        in_specs=[pspec(0), pspec(1), pspec(2), pspec(3), row(LANES), row(d),
                  pl.BlockSpec((None, 1, d), lambda i: (i // bpb + batch_off, 0, 0)),
                  const(ln_g), const(ln_b)],
        out_specs=row(d),
        compiler_params=_cparams(1, VMEM_LIMIT),
        name="combine_ln",
    )(picked, picked, picked, picked, gate, x1, g2, ln_g, ln_b)


def _band_rel(n_prev, cq):
    return np.arange((n_prev + 1) * cq)[None, :] - n_prev * cq - np.arange(cq)[:, None]


def _cache_rel(n_cache, s):
    return np.concatenate([np.arange(n_cache) - n_cache, np.arange(s)])[None, :] - np.arange(s)[:, None]


def _clipped_rel_bias(rel, table):
    idx = np.clip(rel, -A_REL_CLIP, A_REL_CLIP) + A_REL_CLIP
    return jnp.moveaxis(table[idx].astype(F32), -1, 0)


def _t5_bucket(rel):
    nb = T5_BUCKETS // 2
    max_exact = nb // 2
    n = np.abs(rel)
    nf = np.maximum(n, 1).astype(np.float32)
    large = max_exact + (np.log(nf / max_exact) / math.log(T5_MAX_DISTANCE / max_exact)
                         * (nb - max_exact)).astype(np.int32)
    large = np.minimum(large, nb - 1)
    return np.where(rel > 0, nb, 0) + np.where(n < max_exact, n, large)


def _t5_rel_bias(rel, table):
    return jnp.moveaxis(table[_t5_bucket(rel)].astype(F32), -1, 0)


def kernel(x_prompt, x_sample, cache_a_k, cache_a_v, cache_b_k, cache_b_v, state_conv,
           c_prompt, c_sample, w_in, w_out, rel_bias_a, t5_bias, sinks, conv_w, conv_b,
           conv_ln_g, conv_ln_b, w_ada, b_ada, ln_g, ln_b, w_router, b_router,
           w_e_in, b_e_in, w_e_out, b_e_out):
    bp, tp, d = x_prompt.shape
    bs, ts, _ = x_sample.shape
    mp, ms = bp * tp, bs * ts
    n_tok = mp + ms
    na, nb = cache_a_k.shape[2], cache_b_k.shape[2]
    assert tp % ROW_TILE == 0 and mp % ts == 0 and ts >= CONV_W - 1 and tp >= A_REACH
    assert ms == ROW_TILE

    xp = x_prompt.reshape(mp, d)
    xs = x_sample.reshape(ms, d)
    c_all = jnp.concatenate([c_prompt, c_sample], axis=0)

    t5_p = _t5_rel_bias(_band_rel(B_PREV_CHUNKS, CHUNK), t5_bias)
    t5_s = _t5_rel_bias(_cache_rel(nb, ts), t5_bias)
    conv_zero = jnp.zeros((bp, CONV_HALO, C_CH), F32)

    states_p, states_s = [], []
    for l in range(DEPTH):
        mod = _adaln(c_all, w_ada[l], b_ada[l])
        sh1, sc1, g1, sh2, sc2, g2 = (mod[:, j * d:(j + 1) * d].reshape(bp + bs, 1, d)
                                      for j in range(6))
        w_in_bf = w_in[l].astype(BF16)
        w_out_bf = w_out[l].astype(BF16)
        wr = jnp.pad(w_router[l], ((0, 0), (0, LANES - N_EXPERTS)))
        wr_hi = wr.astype(BF16)
        wr_lo = (wr - wr_hi.astype(F32)).astype(BF16)
        br = jnp.pad(b_router[l], (0, LANES - N_EXPERTS)).reshape(1, LANES)
        lng1, lnb1 = ln_g[l, 0].reshape(1, d), ln_b[l, 0].reshape(1, d)
        lng2, lnb2 = ln_g[l, 1].reshape(1, d), ln_b[l, 1].reshape(1, d)
        bias_a_p = _clipped_rel_bias(_band_rel(A_PREV_CHUNKS, CHUNK), rel_bias_a[l])
        bias_a_s = _clipped_rel_bias(_cache_rel(na, ts), rel_bias_a[l])
        conv_args = (conv_w[l], conv_b[l], conv_ln_g[l], conv_ln_b[l])

        qa, ka, va, qb, kb, vb, u = _inproj(xs, sc1, sh1, w_in_bf, tm=ts,
                                            rows_per_batch=ts, batch_off=bp)
        s3 = lambda a: a.reshape(bs, ts, a.shape[-1])
        ka_all = jnp.concatenate([cache_a_k[l].reshape(bs, na, A_WIDTH), s3(ka)], axis=1)
        va_all = jnp.concatenate([cache_a_v[l].reshape(bs, na, A_WIDTH), s3(va)], axis=1)
        kb_all = jnp.concatenate([cache_b_k[l].reshape(bs, nb, B_KV_WIDTH), s3(kb)], axis=1)
        vb_all = jnp.concatenate([cache_b_v[l].reshape(bs, nb, B_KV_WIDTH), s3(vb)], axis=1)
        oa = _attention(s3(qa), ka_all, va_all, bias_a_s, None,
                        cq=ts, n_prev=None, n_heads=A_HEADS, group=1)
        ob = _attention(s3(qb), kb_all, vb_all, t5_s, sinks[l],
                        cq=ts, n_prev=None, n_heads=B_HEADS, group=B_GROUP)
        prev = jnp.pad(state_conv[l], ((0, 0), (CONV_HALO - (CONV_W - 1), 0), (0, 0)))
        oc = _conv_tail(prev, s3(u), *conv_args)
        x1s, h2s, idxs, gates = _outproj(
            oa.reshape(ms, -1), ob.reshape(ms, -1), oc.reshape(ms, -1), xs, g1, sc2, sh2,
            w_out_bf, lng1, lnb1, wr_hi, wr_lo, br, tm=ts, rows_per_batch=ts, batch_off=bp)
        u_ext = jnp.concatenate([state_conv[l], s3(u)], axis=1)
        states_s.append((
            ka_all[:, -na:].reshape(bs, na, A_HEADS, HEAD_DIM),
            va_all[:, -na:].reshape(bs, na, A_HEADS, HEAD_DIM),
            kb_all[:, -nb:].reshape(bs, nb, B_KV_HEADS, HEAD_DIM),
            vb_all[:, -nb:].reshape(bs, nb, B_KV_HEADS, HEAD_DIM),
            u_ext[:, -(CONV_W - 1):]))

        qa, ka, va, qb, kb, vb, u = _inproj(xp, sc1, sh1, w_in_bf, tm=ROW_TILE,
                                            rows_per_batch=tp, batch_off=0)
        r3 = lambda a: a.reshape(bp, tp, a.shape[-1])
        oa = _attention(r3(qa), r3(ka), r3(va), bias_a_p, None,
                        cq=CHUNK, n_prev=A_PREV_CHUNKS, n_heads=A_HEADS, group=1)
        ob = _attention(r3(qb), r3(kb), r3(vb), t5_p, sinks[l],
                        cq=CHUNK, n_prev=B_PREV_CHUNKS, n_heads=B_HEADS, group=B_GROUP)
        oc = _conv_tail(conv_zero, r3(u), *conv_args)
        x1p, h2, idxp, gatep = _outproj(
            oa.reshape(mp, -1), ob.reshape(mp, -1), oc.reshape(mp, -1), xp, g1, sc2, sh2,
            w_out_bf, lng1, lnb1, wr_hi, wr_lo, br, tm=ROW_TILE, rows_per_batch=tp, batch_off=0,
            h2_tail=h2s)
        states_p.append((
            r3(ka)[:, tp - A_REACH:].reshape(bp, A_REACH, A_HEADS, HEAD_DIM),
            r3(va)[:, tp - A_REACH:].reshape(bp, A_REACH, A_HEADS, HEAD_DIM),
            r3(kb)[:, tp - B_WINDOW:].reshape(bp, B_WINDOW, B_KV_HEADS, HEAD_DIM),
            r3(vb)[:, tp - B_WINDOW:].reshape(bp, B_WINDOW, B_KV_HEADS, HEAD_DIM),
            r3(u)[:, tp - (CONV_W - 1):]))

        top_idx = jnp.concatenate([idxp[:, :TOP_K], idxs[:, :TOP_K]], axis=0)
        block_e, n_used, idx_flat, n_rows = _routing_tables(top_idx, n_tok)
        picked = _moe(block_e, n_used, idx_flat, h2, w_e_in[l], b_e_in[l],
                      w_e_out[l], b_e_out[l], n_rows=n_rows)
        xp = _combine(picked, gatep, x1p, g2, lng2, lnb2, tm=ROW_TILE, rows_per_batch=tp,
                      batch_off=0, row_off=0, n_tok=n_tok)
        xs = _combine(picked, gates, x1s, g2, lng2, lnb2, tm=ts, rows_per_batch=ts,
                      batch_off=bp, row_off=mp, n_tok=n_tok)

    a_k_p, a_v_p, b_k_p, b_v_p, conv_p = (jnp.stack(z) for z in zip(*states_p))
    a_k_s, a_v_s, b_k_s, b_v_s, conv_s = (jnp.stack(z) for z in zip(*states_s))
    return (xp.reshape(bp, tp, d), xs.reshape(bs, ts, d), a_k_p, a_v_p, b_k_p, b_v_p, conv_p,
            a_k_s, a_v_s, b_k_s, b_v_s, conv_s)
```

```python
import functools
import math

import jax
import jax.numpy as jnp
import numpy as np
from jax import lax
from jax.experimental import pallas as pl
from jax.experimental.pallas import tpu as pltpu

F32 = jnp.float32
BF16 = jnp.bfloat16

D_MODEL = 1024
DEPTH = 2
CHUNK = 64
HEAD_DIM = 64
ATTN_SCALE = HEAD_DIM ** -0.5
A_HEADS = 4
A_WIDTH = A_HEADS * HEAD_DIM
A_PREV_CHUNKS = 8
A_REACH = A_PREV_CHUNKS * CHUNK
A_REL_CLIP = 128
B_HEADS = 8
B_KV_HEADS = 2
B_GROUP = B_HEADS // B_KV_HEADS
B_WIDTH = B_HEADS * HEAD_DIM
B_KV_WIDTH = B_KV_HEADS * HEAD_DIM
B_WINDOW = 128
B_PREV_CHUNKS = B_WINDOW // CHUNK
T5_BUCKETS = 32
T5_MAX_DISTANCE = 128
C_CH = D_MODEL // 4
CONV_W = 31
CONV_HALO = 32
N_EXPERTS = 32
TOP_K = 4
D_FF = D_MODEL
SWIGLU_LIMIT = 7.0
SWIGLU_ALPHA = 1.702
DEEPNORM_ALPHA = (2 * DEPTH) ** 0.25
LN_EPS = 1e-5
NEG_INF = -1e30

LANES = 128
ROW_TILE = 512
MOE_TILE = 256
IDX_STRIDE = 1024
MOE_IDX_SLOTS = 4
A_BAND = dict(g_chunks=4, n_prev=A_PREV_CHUNKS, n_kv=A_HEADS, group=1)
B_BAND = dict(g_chunks=2, n_prev=B_PREV_CHUNKS, n_kv=B_KV_HEADS, group=B_GROUP)
VMEM_LIMIT = 56 * 1024 * 1024


def _cparams(n_axes=1, vmem=None):
    return pltpu.CompilerParams(dimension_semantics=("arbitrary",) * n_axes,
                                vmem_limit_bytes=vmem)


def _dot(a, b):
    return jnp.dot(a, b, preferred_element_type=F32)


def _layer_norm(z, g, b):
    mu = jnp.mean(z, axis=-1, keepdims=True)
    d = z - mu
    var = jnp.mean(d * d, axis=-1, keepdims=True)
    return d * lax.rsqrt(var + LN_EPS) * g + b


def _split_bf16(a):
    hi = a.astype(BF16)
    lo = (a - hi.astype(F32)).astype(BF16)
    return hi, lo


def _adaln_kernel(c_ref, w_ref, b_ref, o_ref):
    c = c_ref[...]
    a_hi, a_lo = _split_bf16(c * jax.nn.sigmoid(c))
    w_hi, w_lo = _split_bf16(w_ref[...])
    o_ref[...] = _dot(a_hi, w_hi) + _dot(a_lo, w_hi) + _dot(a_hi, w_lo) + b_ref[...]


def _adaln(c_all, w_all, b_all, layer):
    nb, d = c_all.shape
    n = w_all.shape[2]
    tn = 1536
    return pl.pallas_call(
        _adaln_kernel,
        out_shape=jax.ShapeDtypeStruct((nb, n), F32),
        grid=(n // tn,),
        in_specs=[pl.BlockSpec((nb, d), lambda j: (0, 0)),
                  pl.BlockSpec((None, d, tn), lambda j: (layer, 0, j)),
                  pl.BlockSpec((None, 1, tn), lambda j: (layer, 0, j))],
        out_specs=pl.BlockSpec((nb, tn), lambda j: (0, j)),
        compiler_params=_cparams(1, VMEM_LIMIT),
        name="adaln",
    )(c_all, w_all, b_all.reshape(b_all.shape[0], 1, n))


_QA = (0, A_WIDTH)
_KA = (A_WIDTH, 2 * A_WIDTH)
_VA = (2 * A_WIDTH, 3 * A_WIDTH)
_QB = (3 * A_WIDTH, 3 * A_WIDTH + B_WIDTH)
_KB = (_QB[1], _QB[1] + B_KV_WIDTH)
_VB = (_KB[1], _KB[1] + B_KV_WIDTH)
_GA = (_VB[1], _VB[1] + C_CH)
_GG = (_GA[1], _GA[1] + C_CH)


def _inproj_kernel(x_ref, sc_ref, sh_ref, w_ref,
                   qa_ref, ka_ref, va_ref, qb_ref, kb_ref, vb_ref, u_ref):
    h = (x_ref[...] * (1.0 + sc_ref[...]) + sh_ref[...]).astype(BF16)

    def proj(cols):
        return _dot(h, w_ref[:, cols[0]:cols[1]])

    qa_ref[...] = (proj(_QA) * ATTN_SCALE).astype(BF16)
    ka_ref[...] = proj(_KA)
    va_ref[...] = proj(_VA)
    qb_ref[...] = (proj(_QB) * ATTN_SCALE).astype(BF16)
    kb_ref[...] = proj(_KB)
    vb_ref[...] = proj(_VB)
    u_ref[...] = proj(_GA) * jax.nn.sigmoid(proj(_GG))


def _inproj(x, sc, sh, w_bf, *, tm, rows_per_batch, batch_off):
    m, d = x.shape
    bpb = rows_per_batch // tm
    mod_spec = pl.BlockSpec((None, 1, d), lambda i: (i // bpb + batch_off, 0, 0))
    widths = (A_WIDTH, A_WIDTH, A_WIDTH, B_WIDTH, B_KV_WIDTH, B_KV_WIDTH, C_CH)
    dtypes = (BF16, F32, F32, BF16, F32, F32, F32)
    return pl.pallas_call(
        _inproj_kernel,
        out_shape=[jax.ShapeDtypeStruct((m, w), dt) for w, dt in zip(widths, dtypes)],
        grid=(m // tm,),
        in_specs=[pl.BlockSpec((tm, d), lambda i: (i, 0)), mod_spec, mod_spec,
                  pl.BlockSpec(w_bf.shape, lambda i: (0, 0))],
        out_specs=[pl.BlockSpec((tm, w), lambda i: (i, 0)) for w in widths],
        compiler_params=_cparams(1, VMEM_LIMIT),
        name="inproj",
    )(x, sc, sh, w_bf)


def _attn_kernel(*refs, nc, cq, bw, pad, n_heads, group, use_sink):
    if use_sink:
        q_ref, k_ref, v_ref, bias_ref, sink_ref, o_ref, kp_ref, vp_ref = refs
    else:
        q_ref, k_ref, v_ref, bias_ref, o_ref, kp_ref, vp_ref = refs
        sink_ref = None
    tk = k_ref.shape[0]
    if pad:
        zeros = jnp.zeros((pad, kp_ref.shape[1]), BF16)
        kp_ref[0:pad, :] = zeros
        vp_ref[0:pad, :] = zeros
    kp_ref[pad:pad + tk, :] = k_ref[...].astype(BF16)
    vp_ref[pad:pad + tk, :] = v_ref[...].astype(BF16)

    def chunk(c):
        q0 = c * cq
        if not isinstance(c, int):
            q0 = pl.multiple_of(q0, cq)
        qt = q_ref[pl.ds(q0, cq), :]
        kt = kp_ref[pl.ds(q0, bw), :]
        vt = vp_ref[pl.ds(q0, bw), :]
        if pad:
            key_pos = lax.broadcasted_iota(jnp.int32, (cq, bw), 1)
            valid = key_pos >= pad - q0
        outs = []
        for h in range(n_heads):
            n = h // group
            qh = qt[:, h * HEAD_DIM:(h + 1) * HEAD_DIM]
            kh = kt[:, n * HEAD_DIM:(n + 1) * HEAD_DIM]
            vh = vt[:, n * HEAD_DIM:(n + 1) * HEAD_DIM]
            s = lax.dot_general(qh, kh, (((1,), (1,)), ((), ())),
                                preferred_element_type=F32)
            s = s + bias_ref[h]
            if pad:
                s = jnp.where(valid, s, NEG_INF)
            m = jnp.max(s, axis=-1, keepdims=True)
            if use_sink:
                m = jnp.maximum(m, sink_ref[h])
            e = jnp.exp(s - m)
            den = jnp.sum(e, axis=-1, keepdims=True)
            if use_sink:
                den = den + jnp.exp(sink_ref[h] - m)
            outs.append(_dot(e.astype(BF16), vh) / den)
        o_ref[pl.ds(q0, cq), :] = jnp.concatenate(outs, axis=-1).astype(o_ref.dtype)

    if nc == 1:
        chunk(0)
    else:
        def body(c, carry):
            chunk(c)
            return carry
        lax.fori_loop(0, nc, body, 0)


def _attention(q, k, v, bias, sinks, *, cq, n_prev, n_heads, group):
    b, t, qw = q.shape
    tk, kw = k.shape[1], k.shape[2]
    if n_prev is None:
        nc, bw, pad = 1, tk, 0
    else:
        nc, bw, pad = t // cq, (n_prev + 1) * cq, n_prev * cq
    use_sink = sinks is not None
    kern = functools.partial(_attn_kernel, nc=nc, cq=cq, bw=bw, pad=pad,
                             n_heads=n_heads, group=group, use_sink=use_sink)
    in_specs = [pl.BlockSpec((None, t, qw), lambda i: (i, 0, 0)),
                pl.BlockSpec((None, tk, kw), lambda i: (i, 0, 0)),
                pl.BlockSpec((None, tk, kw), lambda i: (i, 0, 0)),
                pl.BlockSpec(bias.shape, lambda i: (0, 0, 0))]
    args = [q, k, v, bias]
    if use_sink:
        in_specs.append(pl.BlockSpec(memory_space=pltpu.SMEM))
        args.append(sinks)
    return pl.pallas_call(
        kern,
        out_shape=jax.ShapeDtypeStruct((b, t, qw), BF16),
        grid=(b,),
        in_specs=in_specs,
        out_specs=pl.BlockSpec((None, t, qw), lambda i: (i, 0, 0)),
        scratch_shapes=[pltpu.VMEM((pad + tk, kw), BF16), pltpu.VMEM((pad + tk, kw), BF16)],
        compiler_params=_cparams(1, VMEM_LIMIT),
        name="attn_sink" if use_sink else "attn",
    )(*args)


def _band_attn_kernel(*refs, t, g_chunks, n_prev, n_kv, group, use_sink):
    if use_sink:
        q_ref, k_ref, v_ref, bias_ref, sink_ref, o_ref, qs_ref, kp_ref, vp_ref = refs
    else:
        q_ref, k_ref, v_ref, bias_ref, o_ref, qs_ref, kp_ref, vp_ref = refs
    pad = n_prev * CHUNK
    gq = g_chunks * CHUNK
    u = pad + gq
    m = group * gq
    hd = HEAD_DIM
    for n in range(n_kv):
        zeros = jnp.zeros((pad, hd), BF16)
        kp_ref[n, 0:pad, :] = zeros
        vp_ref[n, 0:pad, :] = zeros
        kp_ref[n, pad:pad + t, :] = k_ref[:, n * hd:(n + 1) * hd].astype(BF16)
        vp_ref[n, pad:pad + t, :] = v_ref[:, n * hd:(n + 1) * hd].astype(BF16)
    for h in range(n_kv * group):
        qs_ref[h] = q_ref[:, h * hd:(h + 1) * hd]
    key_pos = lax.broadcasted_iota(jnp.int32, (m, u), 1)
    row = lax.broadcasted_iota(jnp.int32, (m, 1), 0)

    def body(g, carry):
        q0 = pl.multiple_of(g * gq, gq)
        valid = key_pos >= pad - q0
        outs = []
        for n in range(n_kv):
            qstk = jnp.concatenate(
                [qs_ref[n * group + j, pl.ds(q0, gq), :] for j in range(group)], axis=0)
            kt = kp_ref[n, pl.ds(q0, u), :]
            vt = vp_ref[n, pl.ds(q0, u), :]
            s = lax.dot_general(qstk, kt, (((1,), (1,)), ((), ())),
                                preferred_element_type=F32)
            s = jnp.where(valid, s + bias_ref[n], NEG_INF)
            mx = jnp.max(s, axis=-1, keepdims=True)
            if use_sink:
                sink = jnp.full((m, 1), sink_ref[n * group], F32)
                for j in range(1, group):
                    sink = jnp.where(row >= j * gq, sink_ref[n * group + j], sink)
                mx = jnp.maximum(mx, sink)
            e = jnp.exp(s - mx)
            den = jnp.sum(e, axis=-1, keepdims=True)
            if use_sink:
                den = den + jnp.exp(sink - mx)
            o = _dot(e.astype(BF16), vt) / den
            outs.extend(o[j * gq:(j + 1) * gq, :] for j in range(group))
        o_ref[pl.ds(q0, gq), :] = jnp.concatenate(outs, axis=-1).astype(o_ref.dtype)
        return carry
    lax.fori_loop(0, t // gq, body, 0)


def _band_bias(head_bias, *, g_chunks, n_prev, n_kv, group):
    pad, gq = n_prev * CHUNK, g_chunks * CHUNK
    u = pad + gq
    r = np.arange(gq)[:, None]
    kk = np.arange(u)[None, :]
    lo = (r // CHUNK) * CHUNK
    in_band = (kk >= lo) & (kk < lo + pad + CHUNK)
    tile = jnp.where(in_band[None], head_bias(kk - pad - r), NEG_INF)
    return tile.reshape(n_kv, group * gq, u)


def _band_attention(q, k, v, bias, sinks, *, g_chunks, n_prev, n_kv, group):
    b, t, qw = q.shape
    kw = k.shape[2]
    pad = n_prev * CHUNK
    use_sink = sinks is not None
    kern = functools.partial(_band_attn_kernel, t=t, g_chunks=g_chunks, n_prev=n_prev,
                             n_kv=n_kv, group=group, use_sink=use_sink)
    in_specs = [pl.BlockSpec((None, t, qw), lambda i: (i, 0, 0)),
                pl.BlockSpec((None, t, kw), lambda i: (i, 0, 0)),
                pl.BlockSpec((None, t, kw), lambda i: (i, 0, 0)),
                pl.BlockSpec(bias.shape, lambda i: (0, 0, 0))]
    args = [q, k, v, bias]
    if use_sink:
        in_specs.append(pl.BlockSpec(memory_space=pltpu.SMEM))
        args.append(sinks)
    return pl.pallas_call(
        kern,
        out_shape=jax.ShapeDtypeStruct((b, t, qw), BF16),
        grid=(b,),
        in_specs=in_specs,
        out_specs=pl.BlockSpec((None, t, qw), lambda i: (i, 0, 0)),
        scratch_shapes=[pltpu.VMEM((n_kv * group, t, HEAD_DIM), BF16),
                        pltpu.VMEM((n_kv, pad + t, HEAD_DIM), BF16),
                        pltpu.VMEM((n_kv, pad + t, HEAD_DIM), BF16)],
        compiler_params=_cparams(1, VMEM_LIMIT),
        name="band_attn_sink" if use_sink else "band_attn",
    )(*args)


def _conv_kernel(prev_ref, u_ref, w_ref, cb_ref, g_ref, b_ref, o_ref, up_ref, *, t, tt):
    up_ref[0:CONV_HALO, :] = prev_ref[...]
    up_ref[CONV_HALO:CONV_HALO + t, :] = u_ref[...]
    lead = CONV_HALO - (CONV_W - 1)

    def tile(t0):
        win = up_ref[pl.ds(t0, tt + CONV_HALO), :]
        acc = jnp.zeros((tt, C_CH), F32)
        for j in range(CONV_W):
            acc = acc + win[j + lead:j + lead + tt, :] * w_ref[j:j + 1, :]
        y = _layer_norm(acc + cb_ref[...], g_ref[...], b_ref[...])
        o_ref[pl.ds(t0, tt), :] = (y * jax.nn.sigmoid(y)).astype(o_ref.dtype)

    if t == tt:
        tile(0)
    else:
        def body(i, carry):
            tile(pl.multiple_of(i * tt, tt))
            return carry
        lax.fori_loop(0, t // tt, body, 0)


def _conv_tail(prev, u, conv_w, conv_b, ln_g, ln_b):
    b, t, c = u.shape
    tt = min(t, 128)
    vec = lambda a: a.reshape(1, c)
    vspec = pl.BlockSpec((1, c), lambda i: (0, 0))
    return pl.pallas_call(
        functools.partial(_conv_kernel, t=t, tt=tt),
        out_shape=jax.ShapeDtypeStruct((b, t, c), BF16),
        grid=(b,),
        in_specs=[pl.BlockSpec((None, CONV_HALO, c), lambda i: (i, 0, 0)),
                  pl.BlockSpec((None, t, c), lambda i: (i, 0, 0)),
                  pl.BlockSpec((CONV_W, c), lambda i: (0, 0)),
                  vspec, vspec, vspec],
        out_specs=pl.BlockSpec((None, t, c), lambda i: (i, 0, 0)),
        scratch_shapes=[pltpu.VMEM((CONV_HALO + t, c), F32)],
        compiler_params=_cparams(1, VMEM_LIMIT),
        name="conv_tail",
    )(prev, u, conv_w, vec(conv_b), vec(ln_g), vec(ln_b))


def _outproj_tail_kernel(*refs, n_blk):
    tail_ref, h2_ref = refs[13], refs[15]
    i = pl.program_id(0)

    @pl.when(i < n_blk)
    def _():
        _outproj_kernel(*refs[:13], *refs[14:])

    @pl.when(i == n_blk)
    def _():
        h2_ref[...] = tail_ref[...]


def _outproj_kernel(oa_ref, ob_ref, oc_ref, x_ref, g1_ref, sc2_ref, sh2_ref, wo_ref,
                    lng_ref, lnb_ref, wr_hi_ref, wr_lo_ref, br_ref,
                    x1_ref, h2_ref, idx_ref, gate_ref):
    mix = (_dot(oa_ref[...], wo_ref[0:A_WIDTH, :])
           + _dot(ob_ref[...], wo_ref[A_WIDTH:A_WIDTH + B_WIDTH, :])
           + _dot(oc_ref[...], wo_ref[A_WIDTH + B_WIDTH:, :]))
    x1 = _layer_norm(DEEPNORM_ALPHA * x_ref[...] + g1_ref[...] * mix, lng_ref[...], lnb_ref[...])
    x1_ref[...] = x1
    h2 = x1 * (1.0 + sc2_ref[...]) + sh2_ref[...]
    h2_ref[...] = h2
    h_hi, h_lo = _split_bf16(h2)
    logits = (_dot(h_hi, wr_hi_ref[...]) + _dot(h_lo, wr_hi_ref[...])
              + _dot(h_hi, wr_lo_ref[...]) + br_ref[...])
    lane = lax.broadcasted_iota(jnp.int32, logits.shape, 1)
    cur = jnp.where(lane < N_EXPERTS, logits, NEG_INF)
    vals, idxs = [], []
    for _ in range(TOP_K):
        m = jnp.max(cur, axis=-1, keepdims=True)
        i = jnp.min(jnp.where(cur == m, lane, LANES), axis=-1, keepdims=True)
        vals.append(m)
        idxs.append(i)
        cur = jnp.where(lane == i, NEG_INF, cur)
    es = [jnp.exp(v - vals[0]) for v in vals]
    den = es[0] + es[1] + es[2] + es[3]
    idx_out = jnp.zeros(logits.shape, jnp.int32)
    gate_out = jnp.zeros(logits.shape, F32)
    for k in range(TOP_K):
        idx_out = jnp.where(lane == k, idxs[k], idx_out)
        gate_out = jnp.where(lane == k, es[k] / den, gate_out)
    idx_ref[...] = idx_out
    gate_ref[...] = gate_out


def _outproj(oa, ob, oc, x, g1, sc2, sh2, wo_bf, ln_g, ln_b, wr_hi, wr_lo, br,
             *, tm, rows_per_batch, batch_off, h2_tail=None):
    m, d = x.shape
    bpb = rows_per_batch // tm
    n_blk = m // tm
    last = n_blk - 1
    row = lambda w: pl.BlockSpec((tm, w), lambda i: (jnp.minimum(i, last), 0))
    mod_spec = pl.BlockSpec((None, 1, d),
                            lambda i: (jnp.minimum(i, last) // bpb + batch_off, 0, 0))
    const = lambda a: pl.BlockSpec(a.shape, lambda i: (0, 0))
    in_specs = [row(A_WIDTH), row(B_WIDTH), row(C_CH), row(d), mod_spec, mod_spec, mod_spec,
                const(wo_bf), const(ln_g), const(ln_b), const(wr_hi), const(wr_lo), const(br)]
    args = [oa, ob, oc, x, g1, sc2, sh2, wo_bf, ln_g, ln_b, wr_hi, wr_lo, br]
    if h2_tail is None:
        kern, steps, h2_rows = _outproj_kernel, n_blk, m
    else:
        assert h2_tail.shape == (tm, d)
        kern = functools.partial(_outproj_tail_kernel, n_blk=n_blk)
        steps, h2_rows = n_blk + 1, m + tm
        in_specs.append(const(h2_tail))
        args.append(h2_tail)
    return pl.pallas_call(
        kern,
        out_shape=[jax.ShapeDtypeStruct((m, d), F32), jax.ShapeDtypeStruct((h2_rows, d), F32),
                   jax.ShapeDtypeStruct((m, LANES), jnp.int32),
                   jax.ShapeDtypeStruct((m, LANES), F32)],
        grid=(steps,),
        in_specs=in_specs,
        out_specs=[row(d), pl.BlockSpec((tm, d), lambda i: (i, 0)), row(LANES), row(LANES)],
        compiler_params=_cparams(1, VMEM_LIMIT),
        name="outproj_route",
    )(*args)


def _moe_kernel(be_ref, nu_ref, idx_hbm, h2_hbm, w1_ref, b1_ref, w2_ref, b2_ref,
                picked_hbm, idx_smem, xbuf, obuf, w1b, w2b, sem_idx, sem_g, sem_s, *, tm):
    i = pl.program_id(0)
    nu = nu_ref[0]

    def table_copy(blk, slot):
        return pltpu.make_async_copy(
            idx_hbm.at[pl.ds(pl.multiple_of(blk * IDX_STRIDE, IDX_STRIDE), IDX_STRIDE)],
            idx_smem.at[pl.ds(pl.multiple_of(slot * IDX_STRIDE, IDX_STRIDE), IDX_STRIDE)],
            sem_idx.at[slot])

    def row_loop(body, static_rows):
        if static_rows:
            for r in range(tm):
                body(r)
        else:
            lax.fori_loop(0, tm, lambda r, c: (body(r), c)[1], 0)

    def start_gathers(tslot, bslot, static_rows=True):
        base = tslot * IDX_STRIDE

        def body(r):
            tok = idx_smem[base + r]
            pltpu.make_async_copy(h2_hbm.at[pl.ds(tok, 1), :],
                                  xbuf.at[bslot, pl.ds(r, 1), :], sem_g.at[bslot]).start()
        row_loop(body, static_rows)

    def wait_gathers(bslot):
        pltpu.make_async_copy(h2_hbm.at[pl.ds(0, tm), :], xbuf.at[bslot], sem_g.at[bslot]).wait()

    def start_scatters(tslot, bslot):
        base = tslot * IDX_STRIDE + tm

        def body(r):
            dst = idx_smem[base + r]
            pltpu.make_async_copy(obuf.at[bslot, pl.ds(r, 1), :],
                                  picked_hbm.at[pl.ds(dst, 1), :], sem_s.at[bslot]).start()
        row_loop(body, True)

    def wait_scatters(bslot):
        pltpu.make_async_copy(obuf.at[bslot], picked_hbm.at[pl.ds(0, tm), :],
                              sem_s.at[bslot]).wait()

    @pl.when(i < nu)
    def _():
        bslot = i % 2
        tslot = i % MOE_IDX_SLOTS

        @pl.when(i == 0)
        def _():
            first = table_copy(0, 0)
            first.start()
            first.wait()
            start_gathers(0, 0, static_rows=False)

            @pl.when(nu > 1)
            def _():
                table_copy(1, 1).start()

        @pl.when(i + 2 < nu)
        def _():
            table_copy(i + 2, (i + 2) % MOE_IDX_SLOTS).start()

        @pl.when(i + 1 < nu)
        def _():
            nslot = (i + 1) % MOE_IDX_SLOTS
            table_copy(i + 1, nslot).wait()
            start_gathers(nslot, 1 - bslot)

        @pl.when(jnp.logical_or(i == 0, be_ref[i] != be_ref[jnp.maximum(i - 1, 0)]))
        def _():
            w1b[...] = w1_ref[...].astype(BF16)
            w2b[...] = w2_ref[...].astype(BF16)

        wait_gathers(bslot)

        @pl.when(i >= 2)
        def _():
            wait_scatters(bslot)

        x = xbuf[bslot].astype(BF16)
        x_glu = jnp.minimum(_dot(x, w1b[:, 0:D_FF]) + b1_ref[:, 0:D_FF], SWIGLU_LIMIT)
        x_lin = jnp.clip(_dot(x, w1b[:, D_FF:]) + b1_ref[:, D_FF:], -SWIGLU_LIMIT, SWIGLU_LIMIT)
        act = x_glu * jax.nn.sigmoid(SWIGLU_ALPHA * x_glu) * (x_lin + 1.0)
        obuf[bslot] = _dot(act.astype(BF16), w2b[...]) + b2_ref[...]
        start_scatters(tslot, bslot)

        @pl.when(i == nu - 1)
        def _():
            @pl.when(i >= 1)
            def _():
                wait_scatters(1 - bslot)
            wait_scatters(bslot)

    @pl.when(i >= nu)
    def _():
        @pl.when(i == nu)
        def _():
            obuf[0] = jnp.zeros(obuf.shape[1:], F32)
        cp = pltpu.make_async_copy(
            obuf.at[0], picked_hbm.at[pl.ds(pl.multiple_of(i * tm, tm), tm), :], sem_s.at[0])
        cp.start()
        cp.wait()


def _moe(block_e, n_used, idx_flat, h2, w1, b1, w2, b2, *, n_rows, layer):
    tm = MOE_TILE
    n_blocks = block_e.shape[0]
    d = h2.shape[1]
    expert = lambda i, be, nu: (layer, be[i], 0, 0)
    grid_spec = pltpu.PrefetchScalarGridSpec(
        num_scalar_prefetch=2,
        grid=(n_blocks,),
        in_specs=[pl.BlockSpec(memory_space=pl.ANY),
                  pl.BlockSpec(memory_space=pl.ANY),
                  pl.BlockSpec((None, None, d, 2 * D_FF), expert),
                  pl.BlockSpec((None, None, 1, 2 * D_FF), expert),
                  pl.BlockSpec((None, None, D_FF, d), expert),
                  pl.BlockSpec((None, None, 1, d), expert)],
        out_specs=pl.BlockSpec(memory_space=pl.ANY),
        scratch_shapes=[pltpu.SMEM((MOE_IDX_SLOTS * IDX_STRIDE,), jnp.int32),
                        pltpu.VMEM((2, tm, d), F32), pltpu.VMEM((2, tm, d), F32),
                        pltpu.VMEM((d, 2 * D_FF), BF16), pltpu.VMEM((D_FF, d), BF16),
                        pltpu.SemaphoreType.DMA((MOE_IDX_SLOTS,)),
                        pltpu.SemaphoreType.DMA((2,)), pltpu.SemaphoreType.DMA((2,))])
    return pl.pallas_call(
        functools.partial(_moe_kernel, tm=tm),
        out_shape=jax.ShapeDtypeStruct((n_rows, d), F32),
        grid_spec=grid_spec,
        compiler_params=_cparams(1, VMEM_LIMIT),
        name="moe_experts",
    )(block_e, n_used, idx_flat, h2, w1, b1.reshape(b1.shape[0], N_EXPERTS, 1, -1), w2,
      b2.reshape(b2.shape[0], N_EXPERTS, 1, -1))


def _routing_tables(top_idx, n_tok):
    tm = MOE_TILE
    n_assign = n_tok * TOP_K
    n_blocks = -(-n_assign // tm) + N_EXPERTS
    n_rows = n_blocks * tm
    i32 = jnp.int32
    flat_e = top_idx.reshape(n_assign)
    key_bits = (n_assign - 1).bit_length()
    assert N_EXPERTS << key_bits < 2 ** 31
    keys = jnp.sort(flat_e * (1 << key_bits) + jnp.arange(n_assign, dtype=i32))
    order = keys & ((1 << key_bits) - 1)
    experts = jnp.arange(N_EXPERTS, dtype=i32)
    counts = jnp.sum((flat_e[:, None] == experts[None, :]).astype(i32), axis=0)
    padded = (counts + tm - 1) // tm * tm
    pad_end = jnp.cumsum(padded)
    pad_start = pad_end - padded
    grp_start = jnp.cumsum(counts) - counts
    n_used = pad_end[-1] // tm
    blk = jnp.arange(n_blocks, dtype=i32)
    used = blk < n_used
    expert_at = lambda start: jnp.minimum(
        jnp.sum((pad_end[None, :] <= start[:, None]).astype(i32), axis=1), N_EXPERTS - 1)
    e_blk = expert_at(blk * tm)
    e_last = expert_at(((n_used - 1) * tm).reshape(1))[0]
    block_e = jnp.where(used, e_blk, e_last)
    row = blk[:, None] * tm + jnp.arange(tm, dtype=i32)[None, :]
    off = row - pad_start[e_blk][:, None]
    cnt = counts[e_blk][:, None]
    grp = grp_start[e_blk][:, None]
    valid = used[:, None] & (off < cnt)
    a = order[jnp.clip(grp + off, 0, n_assign - 1)]
    tok = a // TOP_K
    row_tok = jnp.where(valid, tok, 0)
    real_before = jnp.where(used[:, None], grp + cnt, n_assign)
    row_dst = jnp.where(valid, (a % TOP_K) * n_tok + tok, n_assign + row - real_before)
    idx = jnp.concatenate(
        [row_tok, row_dst, jnp.zeros((n_blocks, IDX_STRIDE - 2 * tm), i32)], axis=1)
    return block_e, n_used.reshape(1).astype(i32), idx.reshape(-1), n_rows


def _combine_kernel(p0_ref, p1_ref, p2_ref, p3_ref, gate_ref, x1_ref, g2_ref,
                    lng_ref, lnb_ref, o_ref):
    gate = gate_ref[...]
    y = (gate[:, 0:1] * p0_ref[...] + gate[:, 1:2] * p1_ref[...]
         + gate[:, 2:3] * p2_ref[...] + gate[:, 3:4] * p3_ref[...])
    o_ref[...] = _layer_norm(DEEPNORM_ALPHA * x1_ref[...] + g2_ref[...] * y,
                             lng_ref[...], lnb_ref[...])


def _combine(picked, gate, x1, g2, ln_g, ln_b, *, tm, rows_per_batch, batch_off, row_off, n_tok):
    m, d = x1.shape
    bpb = rows_per_batch // tm
    blk_off = row_off // tm
    k_stride = n_tok // tm
    pspec = lambda k: pl.BlockSpec((tm, d), lambda i: (i + blk_off + k * k_stride, 0))
    row = lambda w: pl.BlockSpec((tm, w), lambda i: (i, 0))
    const = lambda a: pl.BlockSpec(a.shape, lambda i: (0, 0))
    return pl.pallas_call(
        _combine_kernel,
        out_shape=jax.ShapeDtypeStruct((m, d), F32),
        grid=(m // tm,),
        in_specs=[pspec(0), pspec(1), pspec(2), pspec(3), row(LANES), row(d),
                  pl.BlockSpec((None, 1, d), lambda i: (i // bpb + batch_off, 0, 0)),
                  const(ln_g), const(ln_b)],
        out_specs=row(d),
        compiler_params=_cparams(1, VMEM_LIMIT),
        name="combine_ln",
    )(picked, picked, picked, picked, gate, x1, g2, ln_g, ln_b)


def _band_rel(n_prev, cq):
    return np.arange((n_prev + 1) * cq)[None, :] - n_prev * cq - np.arange(cq)[:, None]


def _cache_rel(n_cache, s):
    return np.concatenate([np.arange(n_cache) - n_cache, np.arange(s)])[None, :] - np.arange(s)[:, None]


def _clipped_rel_bias(rel, table):
    idx = np.clip(rel, -A_REL_CLIP, A_REL_CLIP) + A_REL_CLIP
    return jnp.moveaxis(table[idx].astype(F32), -1, 0)


def _t5_bucket(rel):
    nb = T5_BUCKETS // 2
    max_exact = nb // 2
    n = np.abs(rel)
    nf = np.maximum(n, 1).astype(np.float32)
    large = max_exact + (np.log(nf / max_exact) / math.log(T5_MAX_DISTANCE / max_exact)
                         * (nb - max_exact)).astype(np.int32)
    large = np.minimum(large, nb - 1)
    return np.where(rel > 0, nb, 0) + np.where(n < max_exact, n, large)


def _t5_rel_bias(rel, table):
    return jnp.moveaxis(table[_t5_bucket(rel)].astype(F32), -1, 0)


def kernel(x_prompt, x_sample, cache_a_k, cache_a_v, cache_b_k, cache_b_v, state_conv,
           c_prompt, c_sample, w_in, w_out, rel_bias_a, t5_bias, sinks, conv_w, conv_b,
           conv_ln_g, conv_ln_b, w_ada, b_ada, ln_g, ln_b, w_router, b_router,
           w_e_in, b_e_in, w_e_out, b_e_out):
    bp, tp, d = x_prompt.shape
    bs, ts, _ = x_sample.shape
    mp, ms = bp * tp, bs * ts
    n_tok = mp + ms
    na, nb = cache_a_k.shape[2], cache_b_k.shape[2]
    assert tp % ROW_TILE == 0 and mp % ts == 0 and ts >= CONV_W - 1 and tp >= A_REACH
    assert ms == ROW_TILE

    xp = x_prompt.reshape(mp, d)
    xs = x_sample.reshape(ms, d)
    c_all = jnp.concatenate([c_prompt, c_sample], axis=0)

    t5_p = _band_bias(lambda rel: _t5_rel_bias(rel, t5_bias), **B_BAND)
    t5_s = _t5_rel_bias(_cache_rel(nb, ts), t5_bias)
    conv_zero = jnp.zeros((bp, CONV_HALO, C_CH), F32)

    states_p, states_s = [], []
    for l in range(DEPTH):
        mod = _adaln(c_all, w_ada, b_ada, l)
        sh1, sc1, g1, sh2, sc2, g2 = (mod[:, j * d:(j + 1) * d].reshape(bp + bs, 1, d)
                                      for j in range(6))
        w_in_bf = w_in[l].astype(BF16)
        w_out_bf = w_out[l].astype(BF16)
        wr = jnp.pad(w_router[l], ((0, 0), (0, LANES - N_EXPERTS)))
        wr_hi = wr.astype(BF16)
        wr_lo = (wr - wr_hi.astype(F32)).astype(BF16)
        br = jnp.pad(b_router[l], (0, LANES - N_EXPERTS)).reshape(1, LANES)
        lng1, lnb1 = ln_g[l, 0].reshape(1, d), ln_b[l, 0].reshape(1, d)
        lng2, lnb2 = ln_g[l, 1].reshape(1, d), ln_b[l, 1].reshape(1, d)
        bias_a_p = _band_bias(lambda rel: _clipped_rel_bias(rel, rel_bias_a[l]), **A_BAND)
        bias_a_s = _clipped_rel_bias(_cache_rel(na, ts), rel_bias_a[l])
        conv_args = (conv_w[l], conv_b[l], conv_ln_g[l], conv_ln_b[l])

        qa, ka, va, qb, kb, vb, u = _inproj(xs, sc1, sh1, w_in_bf, tm=ts,
                                            rows_per_batch=ts, batch_off=bp)
        s3 = lambda a: a.reshape(bs, ts, a.shape[-1])
        ka_all = jnp.concatenate([cache_a_k[l].reshape(bs, na, A_WIDTH), s3(ka)], axis=1)
        va_all = jnp.concatenate([cache_a_v[l].reshape(bs, na, A_WIDTH), s3(va)], axis=1)
        kb_all = jnp.concatenate([cache_b_k[l].reshape(bs, nb, B_KV_WIDTH), s3(kb)], axis=1)
        vb_all = jnp.concatenate([cache_b_v[l].reshape(bs, nb, B_KV_WIDTH), s3(vb)], axis=1)
        oa = _attention(s3(qa), ka_all, va_all, bias_a_s, None,
                        cq=ts, n_prev=None, n_heads=A_HEADS, group=1)
        ob = _attention(s3(qb), kb_all, vb_all, t5_s, sinks[l],
                        cq=ts, n_prev=None, n_heads=B_HEADS, group=B_GROUP)
        prev = jnp.pad(state_conv[l], ((0, 0), (CONV_HALO - (CONV_W - 1), 0), (0, 0)))
        oc = _conv_tail(prev, s3(u), *conv_args)
        x1s, h2s, idxs, gates = _outproj(
            oa.reshape(ms, -1), ob.reshape(ms, -1), oc.reshape(ms, -1), xs, g1, sc2, sh2,
            w_out_bf, lng1, lnb1, wr_hi, wr_lo, br, tm=ts, rows_per_batch=ts, batch_off=bp)
        u_ext = jnp.concatenate([state_conv[l], s3(u)], axis=1)
        states_s.append((
            ka_all[:, -na:].reshape(bs, na, A_HEADS, HEAD_DIM),
            va_all[:, -na:].reshape(bs, na, A_HEADS, HEAD_DIM),
            kb_all[:, -nb:].reshape(bs, nb, B_KV_HEADS, HEAD_DIM),
            vb_all[:, -nb:].reshape(bs, nb, B_KV_HEADS, HEAD_DIM),
            u_ext[:, -(CONV_W - 1):]))

        qa, ka, va, qb, kb, vb, u = _inproj(xp, sc1, sh1, w_in_bf, tm=ROW_TILE,
                                            rows_per_batch=tp, batch_off=0)
        r3 = lambda a: a.reshape(bp, tp, a.shape[-1])
        oa = _band_attention(r3(qa), r3(ka), r3(va), bias_a_p, None, **A_BAND)
        ob = _band_attention(r3(qb), r3(kb), r3(vb), t5_p, sinks[l], **B_BAND)
        oc = _conv_tail(conv_zero, r3(u), *conv_args)
        x1p, h2, idxp, gatep = _outproj(
            oa.reshape(mp, -1), ob.reshape(mp, -1), oc.reshape(mp, -1), xp, g1, sc2, sh2,
            w_out_bf, lng1, lnb1, wr_hi, wr_lo, br, tm=ROW_TILE, rows_per_batch=tp, batch_off=0,
            h2_tail=h2s)
        states_p.append((
            r3(ka)[:, tp - A_REACH:].reshape(bp, A_REACH, A_HEADS, HEAD_DIM),
            r3(va)[:, tp - A_REACH:].reshape(bp, A_REACH, A_HEADS, HEAD_DIM),
            r3(kb)[:, tp - B_WINDOW:].reshape(bp, B_WINDOW, B_KV_HEADS, HEAD_DIM),
            r3(vb)[:, tp - B_WINDOW:].reshape(bp, B_WINDOW, B_KV_HEADS, HEAD_DIM),
            r3(u)[:, tp - (CONV_W - 1):]))

        top_idx = jnp.concatenate([idxp[:, :TOP_K], idxs[:, :TOP_K]], axis=0)
        block_e, n_used, idx_flat, n_rows = _routing_tables(top_idx, n_tok)
        picked = _moe(block_e, n_used, idx_flat, h2, w_e_in, b_e_in, w_e_out, b_e_out,
                      n_rows=n_rows, layer=l)
        xp = _combine(picked, gatep, x1p, g2, lng2, lnb2, tm=ROW_TILE, rows_per_batch=tp,
                      batch_off=0, row_off=0, n_tok=n_tok)
        xs = _combine(picked, gates, x1s, g2, lng2, lnb2, tm=ts, rows_per_batch=ts,
                      batch_off=bp, row_off=mp, n_tok=n_tok)

    a_k_p, a_v_p, b_k_p, b_v_p, conv_p = (jnp.stack(z) for z in zip(*states_p))
    a_k_s, a_v_s, b_k_s, b_v_s, conv_s = (jnp.stack(z) for z in zip(*states_s))
    return (xp.reshape(bp, tp, d), xs.reshape(bs, ts, d), a_k_p, a_v_p, b_k_p, b_v_p, conv_p,
            a_k_s, a_v_s, b_k_s, b_v_s, conv_s)
```

```python
import functools
import math

import jax
import jax.numpy as jnp
import numpy as np
from jax import lax
from jax.experimental import pallas as pl
from jax.experimental.pallas import tpu as pltpu

F32 = jnp.float32
BF16 = jnp.bfloat16

D_MODEL = 1024
DEPTH = 2
CHUNK = 64
HEAD_DIM = 64
ATTN_SCALE = HEAD_DIM ** -0.5
A_HEADS = 4
A_WIDTH = A_HEADS * HEAD_DIM
A_PREV_CHUNKS = 8
A_REACH = A_PREV_CHUNKS * CHUNK
A_REL_CLIP = 128
B_HEADS = 8
B_KV_HEADS = 2
B_GROUP = B_HEADS // B_KV_HEADS
B_WIDTH = B_HEADS * HEAD_DIM
B_KV_WIDTH = B_KV_HEADS * HEAD_DIM
B_WINDOW = 128
B_PREV_CHUNKS = B_WINDOW // CHUNK
T5_BUCKETS = 32
T5_MAX_DISTANCE = 128
C_CH = D_MODEL // 4
CONV_W = 31
CONV_HALO = 32
N_EXPERTS = 32
TOP_K = 4
D_FF = D_MODEL
SWIGLU_LIMIT = 7.0
SWIGLU_ALPHA = 1.702
DEEPNORM_ALPHA = (2 * DEPTH) ** 0.25
LN_EPS = 1e-5
NEG_INF = -1e30

LANES = 128
ROW_TILE = 512
MOE_TILE = 256
IDX_STRIDE = 1024
MOE_IDX_SLOTS = 4
A_BAND = dict(g_chunks=4, n_prev=A_PREV_CHUNKS, n_kv=A_HEADS, group=1)
B_BAND = dict(g_chunks=2, n_prev=B_PREV_CHUNKS, n_kv=B_KV_HEADS, group=B_GROUP)
VMEM_LIMIT = 56 * 1024 * 1024


def _cparams(n_axes=1, vmem=None):
    return pltpu.CompilerParams(dimension_semantics=("arbitrary",) * n_axes,
                                vmem_limit_bytes=vmem)


def _dot(a, b):
    return jnp.dot(a, b, preferred_element_type=F32)


def _layer_norm(z, g, b):
    mu = jnp.mean(z, axis=-1, keepdims=True)
    d = z - mu
    var = jnp.mean(d * d, axis=-1, keepdims=True)
    return d * lax.rsqrt(var + LN_EPS) * g + b


SUBLANES = 8
ROW_PIECES = D_MODEL // LANES


def _store_token_tiles(ref, val):
    n = val.shape[0]
    for c in range(ROW_PIECES):
        ref[pl.ds(c, n, stride=ROW_PIECES), :] = val[:, c * LANES:(c + 1) * LANES]


def _load_token_tiles(ref, n):
    return jnp.concatenate(
        [ref[pl.ds(c, n, stride=ROW_PIECES), :] for c in range(ROW_PIECES)], axis=1)


def _split_bf16(a):
    hi = a.astype(BF16)
    lo = (a - hi.astype(F32)).astype(BF16)
    return hi, lo


def _adaln_kernel(c_ref, w_ref, b_ref, o_ref):
    c = c_ref[...]
    a_hi, a_lo = _split_bf16(c * jax.nn.sigmoid(c))
    w_hi, w_lo = _split_bf16(w_ref[...])
    o_ref[...] = _dot(a_hi, w_hi) + _dot(a_lo, w_hi) + _dot(a_hi, w_lo) + b_ref[...]


def _adaln(c_all, w_all, b_all, layer):
    nb, d = c_all.shape
    n = w_all.shape[2]
    tn = 1536
    return pl.pallas_call(
        _adaln_kernel,
        out_shape=jax.ShapeDtypeStruct((nb, n), F32),
        grid=(n // tn,),
        in_specs=[pl.BlockSpec((nb, d), lambda j: (0, 0)),
                  pl.BlockSpec((None, d, tn), lambda j: (layer, 0, j)),
                  pl.BlockSpec((None, 1, tn), lambda j: (layer, 0, j))],
        out_specs=pl.BlockSpec((nb, tn), lambda j: (0, j)),
        compiler_params=_cparams(1, VMEM_LIMIT),
        name="adaln",
    )(c_all, w_all, b_all.reshape(b_all.shape[0], 1, n))


_QA = (0, A_WIDTH)
_KA = (A_WIDTH, 2 * A_WIDTH)
_VA = (2 * A_WIDTH, 3 * A_WIDTH)
_QB = (3 * A_WIDTH, 3 * A_WIDTH + B_WIDTH)
_KB = (_QB[1], _QB[1] + B_KV_WIDTH)
_VB = (_KB[1], _KB[1] + B_KV_WIDTH)
_GA = (_VB[1], _VB[1] + C_CH)
_GG = (_GA[1], _GA[1] + C_CH)


def _inproj_kernel(x_ref, sc_ref, sh_ref, w_ref,
                   qa_ref, ka_ref, va_ref, qb_ref, kb_ref, vb_ref, u_ref):
    h = (x_ref[...] * (1.0 + sc_ref[...]) + sh_ref[...]).astype(BF16)

    def proj(cols):
        return _dot(h, w_ref[:, cols[0]:cols[1]])

    qa_ref[...] = (proj(_QA) * ATTN_SCALE).astype(BF16)
    ka_ref[...] = proj(_KA)
    va_ref[...] = proj(_VA)
    qb_ref[...] = (proj(_QB) * ATTN_SCALE).astype(BF16)
    kb_ref[...] = proj(_KB)
    vb_ref[...] = proj(_VB)
    u_ref[...] = proj(_GA) * jax.nn.sigmoid(proj(_GG))


def _inproj(x, sc, sh, w_bf, *, tm, rows_per_batch, batch_off):
    m, d = x.shape
    bpb = rows_per_batch // tm
    mod_spec = pl.BlockSpec((None, 1, d), lambda i: (i // bpb + batch_off, 0, 0))
    widths = (A_WIDTH, A_WIDTH, A_WIDTH, B_WIDTH, B_KV_WIDTH, B_KV_WIDTH, C_CH)
    dtypes = (BF16, F32, F32, BF16, F32, F32, F32)
    return pl.pallas_call(
        _inproj_kernel,
        out_shape=[jax.ShapeDtypeStruct((m, w), dt) for w, dt in zip(widths, dtypes)],
        grid=(m // tm,),
        in_specs=[pl.BlockSpec((tm, d), lambda i: (i, 0)), mod_spec, mod_spec,
                  pl.BlockSpec(w_bf.shape, lambda i: (0, 0))],
        out_specs=[pl.BlockSpec((tm, w), lambda i: (i, 0)) for w in widths],
        compiler_params=_cparams(1, VMEM_LIMIT),
        name="inproj",
    )(x, sc, sh, w_bf)


def _attn_kernel(*refs, nc, cq, bw, pad, n_heads, group, use_sink):
    if use_sink:
        q_ref, k_ref, v_ref, bias_ref, sink_ref, o_ref, kp_ref, vp_ref = refs
    else:
        q_ref, k_ref, v_ref, bias_ref, o_ref, kp_ref, vp_ref = refs
        sink_ref = None
    tk = k_ref.shape[0]
    if pad:
        zeros = jnp.zeros((pad, kp_ref.shape[1]), BF16)
        kp_ref[0:pad, :] = zeros
        vp_ref[0:pad, :] = zeros
    kp_ref[pad:pad + tk, :] = k_ref[...].astype(BF16)
    vp_ref[pad:pad + tk, :] = v_ref[...].astype(BF16)

    def chunk(c):
        q0 = c * cq
        if not isinstance(c, int):
            q0 = pl.multiple_of(q0, cq)
        qt = q_ref[pl.ds(q0, cq), :]
        kt = kp_ref[pl.ds(q0, bw), :]
        vt = vp_ref[pl.ds(q0, bw), :]
        if pad:
            key_pos = lax.broadcasted_iota(jnp.int32, (cq, bw), 1)
            valid = key_pos >= pad - q0
        outs = []
        for h in range(n_heads):
            n = h // group
            qh = qt[:, h * HEAD_DIM:(h + 1) * HEAD_DIM]
            kh = kt[:, n * HEAD_DIM:(n + 1) * HEAD_DIM]
            vh = vt[:, n * HEAD_DIM:(n + 1) * HEAD_DIM]
            s = lax.dot_general(qh, kh, (((1,), (1,)), ((), ())),
                                preferred_element_type=F32)
            s = s + bias_ref[h]
            if pad:
                s = jnp.where(valid, s, NEG_INF)
            m = jnp.max(s, axis=-1, keepdims=True)
            if use_sink:
                m = jnp.maximum(m, sink_ref[h])
            e = jnp.exp(s - m)
            den = jnp.sum(e, axis=-1, keepdims=True)
            if use_sink:
                den = den + jnp.exp(sink_ref[h] - m)
            outs.append(_dot(e.astype(BF16), vh) / den)
        o_ref[pl.ds(q0, cq), :] = jnp.concatenate(outs, axis=-1).astype(o_ref.dtype)

    if nc == 1:
        chunk(0)
    else:
        def body(c, carry):
            chunk(c)
            return carry
        lax.fori_loop(0, nc, body, 0)


def _attention(q, k, v, bias, sinks, *, cq, n_prev, n_heads, group):
    b, t, qw = q.shape
    tk, kw = k.shape[1], k.shape[2]
    if n_prev is None:
        nc, bw, pad = 1, tk, 0
    else:
        nc, bw, pad = t // cq, (n_prev + 1) * cq, n_prev * cq
    use_sink = sinks is not None
    kern = functools.partial(_attn_kernel, nc=nc, cq=cq, bw=bw, pad=pad,
                             n_heads=n_heads, group=group, use_sink=use_sink)
    in_specs = [pl.BlockSpec((None, t, qw), lambda i: (i, 0, 0)),
                pl.BlockSpec((None, tk, kw), lambda i: (i, 0, 0)),
                pl.BlockSpec((None, tk, kw), lambda i: (i, 0, 0)),
                pl.BlockSpec(bias.shape, lambda i: (0, 0, 0))]
    args = [q, k, v, bias]
    if use_sink:
        in_specs.append(pl.BlockSpec(memory_space=pltpu.SMEM))
        args.append(sinks)
    return pl.pallas_call(
        kern,
        out_shape=jax.ShapeDtypeStruct((b, t, qw), BF16),
        grid=(b,),
        in_specs=in_specs,
        out_specs=pl.BlockSpec((None, t, qw), lambda i: (i, 0, 0)),
        scratch_shapes=[pltpu.VMEM((pad + tk, kw), BF16), pltpu.VMEM((pad + tk, kw), BF16)],
        compiler_params=_cparams(1, VMEM_LIMIT),
        name="attn_sink" if use_sink else "attn",
    )(*args)


def _band_attn_kernel(*refs, t, g_chunks, n_prev, n_kv, group, use_sink):
    if use_sink:
        q_ref, k_ref, v_ref, bias_ref, sink_ref, o_ref, qs_ref, kp_ref, vp_ref = refs
    else:
        q_ref, k_ref, v_ref, bias_ref, o_ref, qs_ref, kp_ref, vp_ref = refs
    pad = n_prev * CHUNK
    gq = g_chunks * CHUNK
    u = pad + gq
    m = group * gq
    hd = HEAD_DIM
    for n in range(n_kv):
        zeros = jnp.zeros((pad, hd), BF16)
        kp_ref[n, 0:pad, :] = zeros
        vp_ref[n, 0:pad, :] = zeros
        kp_ref[n, pad:pad + t, :] = k_ref[:, n * hd:(n + 1) * hd].astype(BF16)
        vp_ref[n, pad:pad + t, :] = v_ref[:, n * hd:(n + 1) * hd].astype(BF16)
    for h in range(n_kv * group):
        qs_ref[h] = q_ref[:, h * hd:(h + 1) * hd]
    key_pos = lax.broadcasted_iota(jnp.int32, (m, u), 1)
    row = lax.broadcasted_iota(jnp.int32, (m, 1), 0)

    def body(g, carry):
        q0 = pl.multiple_of(g * gq, gq)
        valid = key_pos >= pad - q0
        outs = []
        for n in range(n_kv):
            qstk = jnp.concatenate(
                [qs_ref[n * group + j, pl.ds(q0, gq), :] for j in range(group)], axis=0)
            kt = kp_ref[n, pl.ds(q0, u), :]
            vt = vp_ref[n, pl.ds(q0, u), :]
            s = lax.dot_general(qstk, kt, (((1,), (1,)), ((), ())),
                                preferred_element_type=F32)
            s = jnp.where(valid, s + bias_ref[n], NEG_INF)
            mx = jnp.max(s, axis=-1, keepdims=True)
            if use_sink:
                sink = jnp.full((m, 1), sink_ref[n * group], F32)
                for j in range(1, group):
                    sink = jnp.where(row >= j * gq, sink_ref[n * group + j], sink)
                mx = jnp.maximum(mx, sink)
            e = jnp.exp(s - mx)
            den = jnp.sum(e, axis=-1, keepdims=True)
            if use_sink:
                den = den + jnp.exp(sink - mx)
            o = _dot(e.astype(BF16), vt) / den
            outs.extend(o[j * gq:(j + 1) * gq, :] for j in range(group))
        o_ref[pl.ds(q0, gq), :] = jnp.concatenate(outs, axis=-1).astype(o_ref.dtype)
        return carry
    lax.fori_loop(0, t // gq, body, 0)


def _band_bias(head_bias, *, g_chunks, n_prev, n_kv, group):
    pad, gq = n_prev * CHUNK, g_chunks * CHUNK
    u = pad + gq
    r = np.arange(gq)[:, None]
    kk = np.arange(u)[None, :]
    lo = (r // CHUNK) * CHUNK
    in_band = (kk >= lo) & (kk < lo + pad + CHUNK)
    tile = jnp.where(in_band[None], _rel_bias_tile(head_bias, gq, u, pad), NEG_INF)
    return tile.reshape(n_kv, group * gq, u)


def _rel_bias_tile(head_bias, rows, cols, pad):
    n_off = rows + cols - 1
    vec = head_bias(np.arange(n_off) - pad - (rows - 1))
    h = vec.shape[0]
    padded = jnp.concatenate([vec, jnp.zeros((h, 1), vec.dtype)], axis=1)
    skew = jnp.tile(padded, (1, rows))[:, :rows * n_off].reshape(h, rows, n_off)
    return skew[:, :, rows - 1:rows - 1 + cols]


def _band_attention(q, k, v, bias, sinks, *, g_chunks, n_prev, n_kv, group):
    b, t, qw = q.shape
    kw = k.shape[2]
    pad = n_prev * CHUNK
    use_sink = sinks is not None
    kern = functools.partial(_band_attn_kernel, t=t, g_chunks=g_chunks, n_prev=n_prev,
                             n_kv=n_kv, group=group, use_sink=use_sink)
    in_specs = [pl.BlockSpec((None, t, qw), lambda i: (i, 0, 0)),
                pl.BlockSpec((None, t, kw), lambda i: (i, 0, 0)),
                pl.BlockSpec((None, t, kw), lambda i: (i, 0, 0)),
                pl.BlockSpec(bias.shape, lambda i: (0, 0, 0))]
    args = [q, k, v, bias]
    if use_sink:
        in_specs.append(pl.BlockSpec(memory_space=pltpu.SMEM))
        args.append(sinks)
    return pl.pallas_call(
        kern,
        out_shape=jax.ShapeDtypeStruct((b, t, qw), BF16),
        grid=(b,),
        in_specs=in_specs,
        out_specs=pl.BlockSpec((None, t, qw), lambda i: (i, 0, 0)),
        scratch_shapes=[pltpu.VMEM((n_kv * group, t, HEAD_DIM), BF16),
                        pltpu.VMEM((n_kv, pad + t, HEAD_DIM), BF16),
                        pltpu.VMEM((n_kv, pad + t, HEAD_DIM), BF16)],
        compiler_params=_cparams(1, VMEM_LIMIT),
        name="band_attn_sink" if use_sink else "band_attn",
    )(*args)


def _conv_kernel(prev_ref, u_ref, w_ref, cb_ref, g_ref, b_ref, o_ref, up_ref, *, t, tt):
    up_ref[0:CONV_HALO, :] = prev_ref[...]
    up_ref[CONV_HALO:CONV_HALO + t, :] = u_ref[...]
    lead = CONV_HALO - (CONV_W - 1)

    def tile(t0):
        win = up_ref[pl.ds(t0, tt + CONV_HALO), :]
        acc = jnp.zeros((tt, C_CH), F32)
        for j in range(CONV_W):
            acc = acc + win[j + lead:j + lead + tt, :] * w_ref[j:j + 1, :]
        y = _layer_norm(acc + cb_ref[...], g_ref[...], b_ref[...])
        o_ref[pl.ds(t0, tt), :] = (y * jax.nn.sigmoid(y)).astype(o_ref.dtype)

    if t == tt:
        tile(0)
    else:
        def body(i, carry):
            tile(pl.multiple_of(i * tt, tt))
            return carry
        lax.fori_loop(0, t // tt, body, 0)


def _conv_tail(prev, u, conv_w, conv_b, ln_g, ln_b):
    b, t, c = u.shape
    tt = min(t, 128)
    vec = lambda a: a.reshape(1, c)
    vspec = pl.BlockSpec((1, c), lambda i: (0, 0))
    return pl.pallas_call(
        functools.partial(_conv_kernel, t=t, tt=tt),
        out_shape=jax.ShapeDtypeStruct((b, t, c), BF16),
        grid=(b,),
        in_specs=[pl.BlockSpec((None, CONV_HALO, c), lambda i: (i, 0, 0)),
                  pl.BlockSpec((None, t, c), lambda i: (i, 0, 0)),
                  pl.BlockSpec((CONV_W, c), lambda i: (0, 0)),
                  vspec, vspec, vspec],
        out_specs=pl.BlockSpec((None, t, c), lambda i: (i, 0, 0)),
        scratch_shapes=[pltpu.VMEM((CONV_HALO + t, c), F32)],
        compiler_params=_cparams(1, VMEM_LIMIT),
        name="conv_tail",
    )(prev, u, conv_w, vec(conv_b), vec(ln_g), vec(ln_b))


def _outproj_tail_kernel(*refs, n_blk):
    tail_ref, h2_ref = refs[13], refs[15]
    i = pl.program_id(0)

    @pl.when(i < n_blk)
    def _():
        _outproj_kernel(*refs[:13], *refs[14:])

    @pl.when(i == n_blk)
    def _():
        h2_ref[...] = tail_ref[...]


def _outproj_kernel(oa_ref, ob_ref, oc_ref, x_ref, g1_ref, sc2_ref, sh2_ref, wo_ref,
                    lng_ref, lnb_ref, wr_hi_ref, wr_lo_ref, br_ref,
                    x1_ref, h2_ref, idx_ref, gate_ref):
    mix = (_dot(oa_ref[...], wo_ref[0:A_WIDTH, :])
           + _dot(ob_ref[...], wo_ref[A_WIDTH:A_WIDTH + B_WIDTH, :])
           + _dot(oc_ref[...], wo_ref[A_WIDTH + B_WIDTH:, :]))
    x1 = _layer_norm(DEEPNORM_ALPHA * x_ref[...] + g1_ref[...] * mix, lng_ref[...], lnb_ref[...])
    x1_ref[...] = x1
    h2 = x1 * (1.0 + sc2_ref[...]) + sh2_ref[...]
    _store_token_tiles(h2_ref, h2)
    h_hi, h_lo = _split_bf16(h2)
    logits = (_dot(h_hi, wr_hi_ref[...]) + _dot(h_lo, wr_hi_ref[...])
              + _dot(h_hi, wr_lo_ref[...]) + br_ref[...])
    lane = lax.broadcasted_iota(jnp.int32, logits.shape, 1)
    cur = jnp.where(lane < N_EXPERTS, logits, NEG_INF)
    vals, idxs = [], []
    for _ in range(TOP_K):
        m = jnp.max(cur, axis=-1, keepdims=True)
        i = jnp.min(jnp.where(cur == m, lane, LANES), axis=-1, keepdims=True)
        vals.append(m)
        idxs.append(i)
        cur = jnp.where(lane == i, NEG_INF, cur)
    es = [jnp.exp(v - vals[0]) for v in vals]
    den = es[0] + es[1] + es[2] + es[3]
    idx_out = jnp.zeros(logits.shape, jnp.int32)
    gate_out = jnp.zeros(logits.shape, F32)
    for k in range(TOP_K):
        idx_out = jnp.where(lane == k, idxs[k], idx_out)
        gate_out = jnp.where(lane == k, es[k] / den, gate_out)
    idx_ref[...] = idx_out
    gate_ref[...] = gate_out


def _outproj(oa, ob, oc, x, g1, sc2, sh2, wo_bf, ln_g, ln_b, wr_hi, wr_lo, br,
             *, tm, rows_per_batch, batch_off, h2_tail=None):
    m, d = x.shape
    bpb = rows_per_batch // tm
    n_blk = m // tm
    last = n_blk - 1
    row = lambda w: pl.BlockSpec((tm, w), lambda i: (jnp.minimum(i, last), 0))
    mod_spec = pl.BlockSpec((None, 1, d),
                            lambda i: (jnp.minimum(i, last) // bpb + batch_off, 0, 0))
    const = lambda a: pl.BlockSpec(a.shape, lambda i: (0, 0))
    in_specs = [row(A_WIDTH), row(B_WIDTH), row(C_CH), row(d), mod_spec, mod_spec, mod_spec,
                const(wo_bf), const(ln_g), const(ln_b), const(wr_hi), const(wr_lo), const(br)]
    args = [oa, ob, oc, x, g1, sc2, sh2, wo_bf, ln_g, ln_b, wr_hi, wr_lo, br]
    if h2_tail is None:
        kern, steps, h2_rows = _outproj_kernel, n_blk, m
    else:
        assert h2_tail.shape == (tm * ROW_PIECES, LANES)
        kern = functools.partial(_outproj_tail_kernel, n_blk=n_blk)
        steps, h2_rows = n_blk + 1, m + tm
        in_specs.append(const(h2_tail))
        args.append(h2_tail)
    return pl.pallas_call(
        kern,
        out_shape=[jax.ShapeDtypeStruct((m, d), F32),
                   jax.ShapeDtypeStruct((h2_rows * ROW_PIECES, LANES), F32),
                   jax.ShapeDtypeStruct((m, LANES), jnp.int32),
                   jax.ShapeDtypeStruct((m, LANES), F32)],
        grid=(steps,),
        in_specs=in_specs,
        out_specs=[row(d), pl.BlockSpec((tm * ROW_PIECES, LANES), lambda i: (i, 0)),
                   row(LANES), row(LANES)],
        compiler_params=_cparams(1, VMEM_LIMIT),
        name="outproj_route",
    )(*args)


def _moe_kernel(be_ref, nu_ref, idx_hbm, h2_hbm, w1_ref, b1_ref, w2_ref, b2_ref,
                picked_hbm, idx_smem, xbuf, obuf, w1b, w2b, sem_idx, sem_g, sem_s, *, tm):
    i = pl.program_id(0)
    nu = nu_ref[0]
    rp = ROW_PIECES

    def table_copy(blk, slot):
        return pltpu.make_async_copy(
            idx_hbm.at[pl.ds(pl.multiple_of(blk * IDX_STRIDE, IDX_STRIDE), IDX_STRIDE)],
            idx_smem.at[pl.ds(pl.multiple_of(slot * IDX_STRIDE, IDX_STRIDE), IDX_STRIDE)],
            sem_idx.at[slot])

    def row_loop(body, static_rows):
        if static_rows:
            for r in range(tm):
                body(r)
        else:
            lax.fori_loop(0, tm, lambda r, c: (body(r), c)[1], 0)

    def start_gathers(tslot, bslot, static_rows=True):
        base = tslot * IDX_STRIDE

        def body(r):
            src = pl.multiple_of(idx_smem[base + r], rp)
            row0 = r * rp if isinstance(r, int) else pl.multiple_of(r * rp, rp)
            pltpu.make_async_copy(h2_hbm.at[pl.ds(src, rp), :],
                                  xbuf.at[bslot, pl.ds(row0, rp), :], sem_g.at[bslot]).start()
        row_loop(body, static_rows)

    def wait_gathers(bslot):
        pltpu.make_async_copy(h2_hbm.at[pl.ds(0, tm * rp), :], xbuf.at[bslot],
                              sem_g.at[bslot]).wait()

    def start_scatters(tslot, bslot):
        base = tslot * IDX_STRIDE + tm

        def body(r):
            dst = pl.multiple_of(idx_smem[base + r], rp)
            pltpu.make_async_copy(obuf.at[bslot, pl.ds(r * rp, rp), :],
                                  picked_hbm.at[pl.ds(dst, rp), :], sem_s.at[bslot]).start()
        row_loop(body, True)

    def wait_scatters(bslot):
        pltpu.make_async_copy(obuf.at[bslot], picked_hbm.at[pl.ds(0, tm * rp), :],
                              sem_s.at[bslot]).wait()

    @pl.when(i < nu)
    def _():
        bslot = i % 2
        tslot = i % MOE_IDX_SLOTS

        @pl.when(i == 0)
        def _():
            first = table_copy(0, 0)
            first.start()
            first.wait()
            start_gathers(0, 0, static_rows=False)

            @pl.when(nu > 1)
            def _():
                table_copy(1, 1).start()

        @pl.when(i + 2 < nu)
        def _():
            table_copy(i + 2, (i + 2) % MOE_IDX_SLOTS).start()

        @pl.when(i + 1 < nu)
        def _():
            nslot = (i + 1) % MOE_IDX_SLOTS
            table_copy(i + 1, nslot).wait()
            start_gathers(nslot, 1 - bslot)

        @pl.when(jnp.logical_or(i == 0, be_ref[i] != be_ref[jnp.maximum(i - 1, 0)]))
        def _():
            w1b[...] = w1_ref[...].astype(BF16)
            w2b[...] = w2_ref[...].astype(BF16)

        wait_gathers(bslot)

        @pl.when(i >= 2)
        def _():
            wait_scatters(bslot)

        x = _load_token_tiles(xbuf.at[bslot], tm).astype(BF16)
        x_glu = jnp.minimum(_dot(x, w1b[:, 0:D_FF]) + b1_ref[:, 0:D_FF], SWIGLU_LIMIT)
        x_lin = jnp.clip(_dot(x, w1b[:, D_FF:]) + b1_ref[:, D_FF:], -SWIGLU_LIMIT, SWIGLU_LIMIT)
        act = x_glu * jax.nn.sigmoid(SWIGLU_ALPHA * x_glu) * (x_lin + 1.0)
        _store_token_tiles(obuf.at[bslot], _dot(act.astype(BF16), w2b[...]) + b2_ref[...])
        start_scatters(tslot, bslot)

        @pl.when(i == nu - 1)
        def _():
            @pl.when(i >= 1)
            def _():
                wait_scatters(1 - bslot)
            wait_scatters(bslot)

    @pl.when(i >= nu)
    def _():
        @pl.when(i == nu)
        def _():
            obuf[0] = jnp.zeros(obuf.shape[1:], F32)
        cp = pltpu.make_async_copy(
            obuf.at[0],
            picked_hbm.at[pl.ds(pl.multiple_of(i * (tm * rp), tm * rp), tm * rp), :], sem_s.at[0])
        cp.start()
        cp.wait()


def _moe(block_e, n_used, idx_flat, h2, w1, b1, w2, b2, *, n_rows, layer):
    tm = MOE_TILE
    n_blocks = block_e.shape[0]
    d = D_MODEL
    tile_rows = tm * ROW_PIECES
    expert = lambda i, be, nu: (layer, be[i], 0, 0)
    grid_spec = pltpu.PrefetchScalarGridSpec(
        num_scalar_prefetch=2,
        grid=(n_blocks,),
        in_specs=[pl.BlockSpec(memory_space=pl.ANY),
                  pl.BlockSpec(memory_space=pl.ANY),
                  pl.BlockSpec((None, None, d, 2 * D_FF), expert),
                  pl.BlockSpec((None, None, 1, 2 * D_FF), expert),
                  pl.BlockSpec((None, None, D_FF, d), expert),
                  pl.BlockSpec((None, None, 1, d), expert)],
        out_specs=pl.BlockSpec(memory_space=pl.ANY),
        scratch_shapes=[pltpu.SMEM((MOE_IDX_SLOTS * IDX_STRIDE,), jnp.int32),
                        pltpu.VMEM((2, tile_rows, LANES), F32),
                        pltpu.VMEM((2, tile_rows, LANES), F32),
                        pltpu.VMEM((d, 2 * D_FF), BF16), pltpu.VMEM((D_FF, d), BF16),
                        pltpu.SemaphoreType.DMA((MOE_IDX_SLOTS,)),
                        pltpu.SemaphoreType.DMA((2,)), pltpu.SemaphoreType.DMA((2,))])
    return pl.pallas_call(
        functools.partial(_moe_kernel, tm=tm),
        out_shape=jax.ShapeDtypeStruct((n_rows * ROW_PIECES, LANES), F32),
        grid_spec=grid_spec,
        compiler_params=_cparams(1, VMEM_LIMIT),
        name="moe_experts",
    )(block_e, n_used, idx_flat, h2, w1, b1.reshape(b1.shape[0], N_EXPERTS, 1, -1), w2,
      b2.reshape(b2.shape[0], N_EXPERTS, 1, -1))


def _routing_tables(top_idx, n_tok):
    tm = MOE_TILE
    n_assign = n_tok * TOP_K
    n_blocks = -(-n_assign // tm) + N_EXPERTS
    n_rows = n_blocks * tm
    i32 = jnp.int32
    flat_e = top_idx.reshape(n_assign)
    key_bits = (n_assign - 1).bit_length()
    assert N_EXPERTS << key_bits < 2 ** 31
    keys = jnp.sort(flat_e * (1 << key_bits) + jnp.arange(n_assign, dtype=i32))
    order = keys & ((1 << key_bits) - 1)
    experts = jnp.arange(N_EXPERTS, dtype=i32)
    counts = jnp.sum((flat_e[:, None] == experts[None, :]).astype(i32), axis=0)
    padded = (counts + tm - 1) // tm * tm
    pad_end = jnp.cumsum(padded)
    pad_start = pad_end - padded
    grp_start = jnp.cumsum(counts) - counts
    n_used = pad_end[-1] // tm
    blk = jnp.arange(n_blocks, dtype=i32)
    used = blk < n_used
    expert_at = lambda start: jnp.minimum(
        jnp.sum((pad_end[None, :] <= start[:, None]).astype(i32), axis=1), N_EXPERTS - 1)
    e_blk = expert_at(blk * tm)
    e_last = expert_at(((n_used - 1) * tm).reshape(1))[0]
    block_e = jnp.where(used, e_blk, e_last)
    row = blk[:, None] * tm + jnp.arange(tm, dtype=i32)[None, :]
    off = row - pad_start[e_blk][:, None]
    cnt = counts[e_blk][:, None]
    grp = grp_start[e_blk][:, None]
    valid = used[:, None] & (off < cnt)
    a = order[jnp.clip(grp + off, 0, n_assign - 1)]
    tok = a // TOP_K
    row_tok = jnp.where(valid, tok, 0)
    real_before = jnp.where(used[:, None], grp + cnt, n_assign)
    row_dst = jnp.where(valid, (a % TOP_K) * n_tok + tok, n_assign + row - real_before)
    idx = jnp.concatenate(
        [row_tok * ROW_PIECES, row_dst * ROW_PIECES,
         jnp.zeros((n_blocks, IDX_STRIDE - 2 * tm), i32)], axis=1)
    return block_e, n_used.reshape(1).astype(i32), idx.reshape(-1), n_rows


def _combine_kernel(p0_ref, p1_ref, p2_ref, p3_ref, gate_ref, x1_ref, g2_ref,
                    lng_ref, lnb_ref, o_ref):
    gate = gate_ref[...]
    n = gate.shape[0]
    y = (gate[:, 0:1] * _load_token_tiles(p0_ref, n) + gate[:, 1:2] * _load_token_tiles(p1_ref, n)
         + gate[:, 2:3] * _load_token_tiles(p2_ref, n) + gate[:, 3:4] * _load_token_tiles(p3_ref, n))
    o_ref[...] = _layer_norm(DEEPNORM_ALPHA * x1_ref[...] + g2_ref[...] * y,
                             lng_ref[...], lnb_ref[...])


def _combine(picked, gate, x1, g2, ln_g, ln_b, *, tm, rows_per_batch, batch_off, row_off, n_tok):
    m, d = x1.shape
    bpb = rows_per_batch // tm
    blk_off = row_off // tm
    k_stride = n_tok // tm
    pspec = lambda k: pl.BlockSpec((tm * ROW_PIECES, LANES),
                                   lambda i: (i + blk_off + k * k_stride, 0))
    row = lambda w: pl.BlockSpec((tm, w), lambda i: (i, 0))
    const = lambda a: pl.BlockSpec(a.shape, lambda i: (0, 0))
    return pl.pallas_call(
        _combine_kernel,
        out_shape=jax.ShapeDtypeStruct((m, d), F32),
        grid=(m // tm,),
        in_specs=[pspec(0), pspec(1), pspec(2), pspec(3), row(LANES), row(d),
                  pl.BlockSpec((None, 1, d), lambda i: (i // bpb + batch_off, 0, 0)),
                  const(ln_g), const(ln_b)],
        out_specs=row(d),
        compiler_params=_cparams(1, VMEM_LIMIT),
        name="combine_ln",
    )(picked, picked, picked, picked, gate, x1, g2, ln_g, ln_b)


def _clipped_rel_bias(rel, table):
    idx = np.clip(rel, -A_REL_CLIP, A_REL_CLIP) + A_REL_CLIP
    return jnp.moveaxis(table[idx].astype(F32), -1, 0)


def _t5_bucket(rel):
    nb = T5_BUCKETS // 2
    max_exact = nb // 2
    n = np.abs(rel)
    nf = np.maximum(n, 1).astype(np.float32)
    large = max_exact + (np.log(nf / max_exact) / math.log(T5_MAX_DISTANCE / max_exact)
                         * (nb - max_exact)).astype(np.int32)
    large = np.minimum(large, nb - 1)
    return np.where(rel > 0, nb, 0) + np.where(n < max_exact, n, large)


def _t5_rel_bias(rel, table):
    return jnp.moveaxis(table[_t5_bucket(rel)].astype(F32), -1, 0)


def kernel(x_prompt, x_sample, cache_a_k, cache_a_v, cache_b_k, cache_b_v, state_conv,
           c_prompt, c_sample, w_in, w_out, rel_bias_a, t5_bias, sinks, conv_w, conv_b,
           conv_ln_g, conv_ln_b, w_ada, b_ada, ln_g, ln_b, w_router, b_router,
           w_e_in, b_e_in, w_e_out, b_e_out):
    bp, tp, d = x_prompt.shape
    bs, ts, _ = x_sample.shape
    mp, ms = bp * tp, bs * ts
    n_tok = mp + ms
    na, nb = cache_a_k.shape[2], cache_b_k.shape[2]
    assert tp % ROW_TILE == 0 and mp % ts == 0 and ts >= CONV_W - 1 and tp >= A_REACH
    assert ms == ROW_TILE

    xp = x_prompt.reshape(mp, d)
    xs = x_sample.reshape(ms, d)
    c_all = jnp.concatenate([c_prompt, c_sample], axis=0)

    t5_p = _band_bias(lambda rel: _t5_rel_bias(rel, t5_bias), **B_BAND)
    t5_s = _rel_bias_tile(lambda rel: _t5_rel_bias(rel, t5_bias), ts, nb + ts, nb)
    conv_zero = jnp.zeros((bp, CONV_HALO, C_CH), F32)

    states_p, states_s = [], []
    for l in range(DEPTH):
        mod = _adaln(c_all, w_ada, b_ada, l)
        sh1, sc1, g1, sh2, sc2, g2 = (mod[:, j * d:(j + 1) * d].reshape(bp + bs, 1, d)
                                      for j in range(6))
        w_in_bf = w_in[l].astype(BF16)
        w_out_bf = w_out[l].astype(BF16)
        wr = jnp.pad(w_router[l], ((0, 0), (0, LANES - N_EXPERTS)))
        wr_hi = wr.astype(BF16)
        wr_lo = (wr - wr_hi.astype(F32)).astype(BF16)
        br = jnp.pad(b_router[l], (0, LANES - N_EXPERTS)).reshape(1, LANES)
        lng1, lnb1 = ln_g[l, 0].reshape(1, d), ln_b[l, 0].reshape(1, d)
        lng2, lnb2 = ln_g[l, 1].reshape(1, d), ln_b[l, 1].reshape(1, d)
        bias_a_p = _band_bias(lambda rel: _clipped_rel_bias(rel, rel_bias_a[l]), **A_BAND)
        bias_a_s = _rel_bias_tile(lambda rel: _clipped_rel_bias(rel, rel_bias_a[l]),
                                  ts, na + ts, na)
        conv_args = (conv_w[l], conv_b[l], conv_ln_g[l], conv_ln_b[l])

        qa, ka, va, qb, kb, vb, u = _inproj(xs, sc1, sh1, w_in_bf, tm=ts,
                                            rows_per_batch=ts, batch_off=bp)
        s3 = lambda a: a.reshape(bs, ts, a.shape[-1])
        ka_all = jnp.concatenate([cache_a_k[l].reshape(bs, na, A_WIDTH), s3(ka)], axis=1)
        va_all = jnp.concatenate([cache_a_v[l].reshape(bs, na, A_WIDTH), s3(va)], axis=1)
        kb_all = jnp.concatenate([cache_b_k[l].reshape(bs, nb, B_KV_WIDTH), s3(kb)], axis=1)
        vb_all = jnp.concatenate([cache_b_v[l].reshape(bs, nb, B_KV_WIDTH), s3(vb)], axis=1)
        oa = _attention(s3(qa), ka_all, va_all, bias_a_s, None,
                        cq=ts, n_prev=None, n_heads=A_HEADS, group=1)
        ob = _attention(s3(qb), kb_all, vb_all, t5_s, sinks[l],
                        cq=ts, n_prev=None, n_heads=B_HEADS, group=B_GROUP)
        prev = jnp.pad(state_conv[l], ((0, 0), (CONV_HALO - (CONV_W - 1), 0), (0, 0)))
        oc = _conv_tail(prev, s3(u), *conv_args)
        x1s, h2s, idxs, gates = _outproj(
            oa.reshape(ms, -1), ob.reshape(ms, -1), oc.reshape(ms, -1), xs, g1, sc2, sh2,
            w_out_bf, lng1, lnb1, wr_hi, wr_lo, br, tm=ts, rows_per_batch=ts, batch_off=bp)
        u_ext = jnp.concatenate([state_conv[l], s3(u)], axis=1)
        states_s.append((
            ka_all[:, -na:].reshape(bs, na, A_HEADS, HEAD_DIM),
            va_all[:, -na:].reshape(bs, na, A_HEADS, HEAD_DIM),
            kb_all[:, -nb:].reshape(bs, nb, B_KV_HEADS, HEAD_DIM),
            vb_all[:, -nb:].reshape(bs, nb, B_KV_HEADS, HEAD_DIM),
            u_ext[:, -(CONV_W - 1):]))

        qa, ka, va, qb, kb, vb, u = _inproj(xp, sc1, sh1, w_in_bf, tm=ROW_TILE,
                                            rows_per_batch=tp, batch_off=0)
        r3 = lambda a: a.reshape(bp, tp, a.shape[-1])
        oa = _band_attention(r3(qa), r3(ka), r3(va), bias_a_p, None, **A_BAND)
        ob = _band_attention(r3(qb), r3(kb), r3(vb), t5_p, sinks[l], **B_BAND)
        oc = _conv_tail(conv_zero, r3(u), *conv_args)
        x1p, h2, idxp, gatep = _outproj(
            oa.reshape(mp, -1), ob.reshape(mp, -1), oc.reshape(mp, -1), xp, g1, sc2, sh2,
            w_out_bf, lng1, lnb1, wr_hi, wr_lo, br, tm=ROW_TILE, rows_per_batch=tp, batch_off=0,
            h2_tail=h2s)
        states_p.append((
            r3(ka)[:, tp - A_REACH:].reshape(bp, A_REACH, A_HEADS, HEAD_DIM),
            r3(va)[:, tp - A_REACH:].reshape(bp, A_REACH, A_HEADS, HEAD_DIM),
            r3(kb)[:, tp - B_WINDOW:].reshape(bp, B_WINDOW, B_KV_HEADS, HEAD_DIM),
            r3(vb)[:, tp - B_WINDOW:].reshape(bp, B_WINDOW, B_KV_HEADS, HEAD_DIM),
            r3(u)[:, tp - (CONV_W - 1):]))

        top_idx = jnp.concatenate([idxp[:, :TOP_K], idxs[:, :TOP_K]], axis=0)
        block_e, n_used, idx_flat, n_rows = _routing_tables(top_idx, n_tok)
        picked = _moe(block_e, n_used, idx_flat, h2, w_e_in, b_e_in, w_e_out, b_e_out,
                      n_rows=n_rows, layer=l)
        xp = _combine(picked, gatep, x1p, g2, lng2, lnb2, tm=ROW_TILE, rows_per_batch=tp,
                      batch_off=0, row_off=0, n_tok=n_tok)
        xs = _combine(picked, gates, x1s, g2, lng2, lnb2, tm=ts, rows_per_batch=ts,
                      batch_off=bp, row_off=mp, n_tok=n_tok)

    a_k_p, a_v_p, b_k_p, b_v_p, conv_p = (jnp.stack(z) for z in zip(*states_p))
    a_k_s, a_v_s, b_k_s, b_v_s, conv_s = (jnp.stack(z) for z in zip(*states_s))
    return (xp.reshape(bp, tp, d), xs.reshape(bs, ts, d), a_k_p, a_v_p, b_k_p, b_v_p, conv_p,
            a_k_s, a_v_s, b_k_s, b_v_s, conv_s)
```

```python
import functools
import math

import jax
import jax.numpy as jnp
import numpy as np
from jax import lax
from jax.experimental import pallas as pl
from jax.experimental.pallas import tpu as pltpu

F32 = jnp.float32
BF16 = jnp.bfloat16

D_MODEL = 1024
DEPTH = 2
CHUNK = 64
HEAD_DIM = 64
ATTN_SCALE = HEAD_DIM ** -0.5
A_HEADS = 4
A_WIDTH = A_HEADS * HEAD_DIM
A_PREV_CHUNKS = 8
A_REACH = A_PREV_CHUNKS * CHUNK
A_REL_CLIP = 128
B_HEADS = 8
B_KV_HEADS = 2
B_GROUP = B_HEADS // B_KV_HEADS
B_WIDTH = B_HEADS * HEAD_DIM
B_KV_WIDTH = B_KV_HEADS * HEAD_DIM
B_WINDOW = 128
B_PREV_CHUNKS = B_WINDOW // CHUNK
T5_BUCKETS = 32
T5_MAX_DISTANCE = 128
C_CH = D_MODEL // 4
CONV_W = 31
CONV_HALO = 32
N_EXPERTS = 32
TOP_K = 4
D_FF = D_MODEL
SWIGLU_LIMIT = 7.0
SWIGLU_ALPHA = 1.702
DEEPNORM_ALPHA = (2 * DEPTH) ** 0.25
LN_EPS = 1e-5
NEG_INF = -1e30

LANES = 128
ROW_TILE = 512
MOE_TILE = 256
IDX_STRIDE = 1024
MOE_IDX_SLOTS = 4
A_BAND = dict(g_chunks=4, n_prev=A_PREV_CHUNKS, n_kv=A_HEADS, group=1)
B_BAND = dict(g_chunks=2, n_prev=B_PREV_CHUNKS, n_kv=B_KV_HEADS, group=B_GROUP)
VMEM_LIMIT = 56 * 1024 * 1024


def _cparams(n_axes=1, vmem=None):
    return pltpu.CompilerParams(dimension_semantics=("arbitrary",) * n_axes,
                                vmem_limit_bytes=vmem)


def _dot(a, b):
    return jnp.dot(a, b, preferred_element_type=F32)


def _layer_norm(z, g, b):
    mu = jnp.mean(z, axis=-1, keepdims=True)
    d = z - mu
    var = jnp.mean(d * d, axis=-1, keepdims=True)
    return d * lax.rsqrt(var + LN_EPS) * g + b


SUBLANES = 8
ROW_PIECES = D_MODEL // LANES


def _store_token_tiles(ref, val):
    n = val.shape[0]
    for c in range(ROW_PIECES):
        ref[pl.ds(c, n, stride=ROW_PIECES), :] = val[:, c * LANES:(c + 1) * LANES]


def _load_token_tiles(ref, n):
    return jnp.concatenate(
        [ref[pl.ds(c, n, stride=ROW_PIECES), :] for c in range(ROW_PIECES)], axis=1)


def _split_bf16(a):
    hi = a.astype(BF16)
    lo = (a - hi.astype(F32)).astype(BF16)
    return hi, lo


def _adaln_kernel(c_ref, w_ref, b_ref, o_ref):
    c = c_ref[...]
    a_hi, a_lo = _split_bf16(c * jax.nn.sigmoid(c))
    w_hi, w_lo = _split_bf16(w_ref[...])
    o_ref[...] = _dot(a_hi, w_hi) + _dot(a_lo, w_hi) + _dot(a_hi, w_lo) + b_ref[...]


def _adaln(c_all, w_all, b_all, layer):
    nb, d = c_all.shape
    n = w_all.shape[2]
    tn = 1536
    return pl.pallas_call(
        _adaln_kernel,
        out_shape=jax.ShapeDtypeStruct((nb, n), F32),
        grid=(n // tn,),
        in_specs=[pl.BlockSpec((nb, d), lambda j: (0, 0)),
                  pl.BlockSpec((None, d, tn), lambda j: (layer, 0, j)),
                  pl.BlockSpec((None, 1, tn), lambda j: (layer, 0, j))],
        out_specs=pl.BlockSpec((nb, tn), lambda j: (0, j)),
        compiler_params=_cparams(1, VMEM_LIMIT),
        name="adaln",
    )(c_all, w_all, b_all.reshape(b_all.shape[0], 1, n))


_QA = (0, A_WIDTH)
_KA = (A_WIDTH, 2 * A_WIDTH)
_VA = (2 * A_WIDTH, 3 * A_WIDTH)
_QB = (3 * A_WIDTH, 3 * A_WIDTH + B_WIDTH)
_KB = (_QB[1], _QB[1] + B_KV_WIDTH)
_VB = (_KB[1], _KB[1] + B_KV_WIDTH)
_GA = (_VB[1], _VB[1] + C_CH)
_GG = (_GA[1], _GA[1] + C_CH)


def _inproj_kernel(x_ref, sc_ref, sh_ref, w_ref,
                   qa_ref, ka_ref, va_ref, qb_ref, kb_ref, vb_ref, u_ref):
    h = (x_ref[...] * (1.0 + sc_ref[...]) + sh_ref[...]).astype(BF16)

    def proj(cols):
        return _dot(h, w_ref[:, cols[0]:cols[1]])

    qa_ref[...] = (proj(_QA) * ATTN_SCALE).astype(BF16)
    ka_ref[...] = proj(_KA)
    va_ref[...] = proj(_VA)
    qb_ref[...] = (proj(_QB) * ATTN_SCALE).astype(BF16)
    kb_ref[...] = proj(_KB)
    vb_ref[...] = proj(_VB)
    u_ref[...] = proj(_GA) * jax.nn.sigmoid(proj(_GG))


def _inproj(x, sc, sh, w_bf, *, tm, rows_per_batch, batch_off):
    m, d = x.shape
    bpb = rows_per_batch // tm
    mod_spec = pl.BlockSpec((None, 1, d), lambda i: (i // bpb + batch_off, 0, 0))
    widths = (A_WIDTH, A_WIDTH, A_WIDTH, B_WIDTH, B_KV_WIDTH, B_KV_WIDTH, C_CH)
    dtypes = (BF16, F32, F32, BF16, F32, F32, F32)
    return pl.pallas_call(
        _inproj_kernel,
        out_shape=[jax.ShapeDtypeStruct((m, w), dt) for w, dt in zip(widths, dtypes)],
        grid=(m // tm,),
        in_specs=[pl.BlockSpec((tm, d), lambda i: (i, 0)), mod_spec, mod_spec,
                  pl.BlockSpec(w_bf.shape, lambda i: (0, 0))],
        out_specs=[pl.BlockSpec((tm, w), lambda i: (i, 0)) for w in widths],
        compiler_params=_cparams(1, VMEM_LIMIT),
        name="inproj",
    )(x, sc, sh, w_bf)


def _attn_kernel(*refs, nc, cq, bw, pad, n_heads, group, use_sink):
    if use_sink:
        q_ref, k_ref, v_ref, bias_ref, sink_ref, o_ref, kp_ref, vp_ref = refs
    else:
        q_ref, k_ref, v_ref, bias_ref, o_ref, kp_ref, vp_ref = refs
        sink_ref = None
    tk = k_ref.shape[0]
    if pad:
        zeros = jnp.zeros((pad, kp_ref.shape[1]), BF16)
        kp_ref[0:pad, :] = zeros
        vp_ref[0:pad, :] = zeros
    kp_ref[pad:pad + tk, :] = k_ref[...].astype(BF16)
    vp_ref[pad:pad + tk, :] = v_ref[...].astype(BF16)

    def chunk(c):
        q0 = c * cq
        if not isinstance(c, int):
            q0 = pl.multiple_of(q0, cq)
        qt = q_ref[pl.ds(q0, cq), :]
        kt = kp_ref[pl.ds(q0, bw), :]
        vt = vp_ref[pl.ds(q0, bw), :]
        if pad:
            key_pos = lax.broadcasted_iota(jnp.int32, (cq, bw), 1)
            valid = key_pos >= pad - q0
        outs = []
        for h in range(n_heads):
            n = h // group
            qh = qt[:, h * HEAD_DIM:(h + 1) * HEAD_DIM]
            kh = kt[:, n * HEAD_DIM:(n + 1) * HEAD_DIM]
            vh = vt[:, n * HEAD_DIM:(n + 1) * HEAD_DIM]
            s = lax.dot_general(qh, kh, (((1,), (1,)), ((), ())),
                                preferred_element_type=F32)
            s = s + bias_ref[h]
            if pad:
                s = jnp.where(valid, s, NEG_INF)
            m = jnp.max(s, axis=-1, keepdims=True)
            if use_sink:
                m = jnp.maximum(m, sink_ref[h])
            e = jnp.exp(s - m)
            den = jnp.sum(e, axis=-1, keepdims=True)
            if use_sink:
                den = den + jnp.exp(sink_ref[h] - m)
            outs.append(_dot(e.astype(BF16), vh) / den)
        o_ref[pl.ds(q0, cq), :] = jnp.concatenate(outs, axis=-1).astype(o_ref.dtype)

    if nc == 1:
        chunk(0)
    else:
        def body(c, carry):
            chunk(c)
            return carry
        lax.fori_loop(0, nc, body, 0)


def _attention(q, k, v, bias, sinks, *, cq, n_prev, n_heads, group):
    b, t, qw = q.shape
    tk, kw = k.shape[1], k.shape[2]
    if n_prev is None:
        nc, bw, pad = 1, tk, 0
    else:
        nc, bw, pad = t // cq, (n_prev + 1) * cq, n_prev * cq
    use_sink = sinks is not None
    kern = functools.partial(_attn_kernel, nc=nc, cq=cq, bw=bw, pad=pad,
                             n_heads=n_heads, group=group, use_sink=use_sink)
    in_specs = [pl.BlockSpec((None, t, qw), lambda i: (i, 0, 0)),
                pl.BlockSpec((None, tk, kw), lambda i: (i, 0, 0)),
                pl.BlockSpec((None, tk, kw), lambda i: (i, 0, 0)),
                pl.BlockSpec(bias.shape, lambda i: (0, 0, 0))]
    args = [q, k, v, bias]
    if use_sink:
        in_specs.append(pl.BlockSpec(memory_space=pltpu.SMEM))
        args.append(sinks)
    return pl.pallas_call(
        kern,
        out_shape=jax.ShapeDtypeStruct((b, t, qw), BF16),
        grid=(b,),
        in_specs=in_specs,
        out_specs=pl.BlockSpec((None, t, qw), lambda i: (i, 0, 0)),
        scratch_shapes=[pltpu.VMEM((pad + tk, kw), BF16), pltpu.VMEM((pad + tk, kw), BF16)],
        compiler_params=_cparams(1, VMEM_LIMIT),
        name="attn_sink" if use_sink else "attn",
    )(*args)


def _band_attn_kernel(*refs, t, g_chunks, n_prev, n_kv, group, use_sink):
    if use_sink:
        q_ref, k_ref, v_ref, bias_ref, sink_ref, o_ref, qs_ref, kp_ref, vp_ref = refs
    else:
        q_ref, k_ref, v_ref, bias_ref, o_ref, qs_ref, kp_ref, vp_ref = refs
    pad = n_prev * CHUNK
    gq = g_chunks * CHUNK
    u = pad + gq
    m = group * gq
    hd = HEAD_DIM
    for n in range(n_kv):
        zeros = jnp.zeros((pad, hd), BF16)
        kp_ref[n, 0:pad, :] = zeros
        vp_ref[n, 0:pad, :] = zeros
        kp_ref[n, pad:pad + t, :] = k_ref[:, n * hd:(n + 1) * hd].astype(BF16)
        vp_ref[n, pad:pad + t, :] = v_ref[:, n * hd:(n + 1) * hd].astype(BF16)
    for h in range(n_kv * group):
        qs_ref[h] = q_ref[:, h * hd:(h + 1) * hd]
    key_pos = lax.broadcasted_iota(jnp.int32, (m, u), 1)
    row = lax.broadcasted_iota(jnp.int32, (m, 1), 0)

    def body(g, carry):
        q0 = pl.multiple_of(g * gq, gq)
        valid = key_pos >= pad - q0
        outs = []
        for n in range(n_kv):
            qstk = jnp.concatenate(
                [qs_ref[n * group + j, pl.ds(q0, gq), :] for j in range(group)], axis=0)
            kt = kp_ref[n, pl.ds(q0, u), :]
            vt = vp_ref[n, pl.ds(q0, u), :]
            s = lax.dot_general(qstk, kt, (((1,), (1,)), ((), ())),
                                preferred_element_type=F32)
            s = jnp.where(valid, s + bias_ref[n], NEG_INF)
            mx = jnp.max(s, axis=-1, keepdims=True)
            if use_sink:
                sink = jnp.full((m, 1), sink_ref[n * group], F32)
                for j in range(1, group):
                    sink = jnp.where(row >= j * gq, sink_ref[n * group + j], sink)
                mx = jnp.maximum(mx, sink)
            e = jnp.exp(s - mx)
            den = jnp.sum(e, axis=-1, keepdims=True)
            if use_sink:
                den = den + jnp.exp(sink - mx)
            o = _dot(e.astype(BF16), vt) / den
            outs.extend(o[j * gq:(j + 1) * gq, :] for j in range(group))
        o_ref[pl.ds(q0, gq), :] = jnp.concatenate(outs, axis=-1).astype(o_ref.dtype)
        return carry
    lax.fori_loop(0, t // gq, body, 0)


def _band_bias(head_bias, *, g_chunks, n_prev, n_kv, group):
    pad, gq = n_prev * CHUNK, g_chunks * CHUNK
    u = pad + gq
    r = np.arange(gq)[:, None]
    kk = np.arange(u)[None, :]
    lo = (r // CHUNK) * CHUNK
    in_band = (kk >= lo) & (kk < lo + pad + CHUNK)
    tile = jnp.where(in_band[None], _rel_bias_tile(head_bias, gq, u, pad), NEG_INF)
    return tile.reshape(n_kv, group * gq, u)


def _rel_bias_tile(head_bias, rows, cols, pad):
    n_off = rows + cols - 1
    vec = head_bias(np.arange(n_off) - pad - (rows - 1))
    h = vec.shape[0]
    padded = jnp.concatenate([vec, jnp.zeros((h, 1), vec.dtype)], axis=1)
    skew = jnp.tile(padded, (1, rows))[:, :rows * n_off].reshape(h, rows, n_off)
    return skew[:, :, rows - 1:rows - 1 + cols]


def _band_attention(q, k, v, bias, sinks, *, g_chunks, n_prev, n_kv, group):
    b, t, qw = q.shape
    kw = k.shape[2]
    pad = n_prev * CHUNK
    use_sink = sinks is not None
    kern = functools.partial(_band_attn_kernel, t=t, g_chunks=g_chunks, n_prev=n_prev,
                             n_kv=n_kv, group=group, use_sink=use_sink)
    in_specs = [pl.BlockSpec((None, t, qw), lambda i: (i, 0, 0)),
                pl.BlockSpec((None, t, kw), lambda i: (i, 0, 0)),
                pl.BlockSpec((None, t, kw), lambda i: (i, 0, 0)),
                pl.BlockSpec(bias.shape, lambda i: (0, 0, 0))]
    args = [q, k, v, bias]
    if use_sink:
        in_specs.append(pl.BlockSpec(memory_space=pltpu.SMEM))
        args.append(sinks)
    return pl.pallas_call(
        kern,
        out_shape=jax.ShapeDtypeStruct((b, t, qw), BF16),
        grid=(b,),
        in_specs=in_specs,
        out_specs=pl.BlockSpec((None, t, qw), lambda i: (i, 0, 0)),
        scratch_shapes=[pltpu.VMEM((n_kv * group, t, HEAD_DIM), BF16),
                        pltpu.VMEM((n_kv, pad + t, HEAD_DIM), BF16),
                        pltpu.VMEM((n_kv, pad + t, HEAD_DIM), BF16)],
        compiler_params=_cparams(1, VMEM_LIMIT),
        name="band_attn_sink" if use_sink else "band_attn",
    )(*args)


def _conv_kernel(prev_ref, u_ref, w_ref, cb_ref, g_ref, b_ref, o_ref, up_ref, *, t, tt):
    up_ref[0:CONV_HALO, :] = prev_ref[...]
    up_ref[CONV_HALO:CONV_HALO + t, :] = u_ref[...]
    lead = CONV_HALO - (CONV_W - 1)

    def tile(t0):
        win = up_ref[pl.ds(t0, tt + CONV_HALO), :]
        acc = jnp.zeros((tt, C_CH), F32)
        for j in range(CONV_W):
            acc = acc + win[j + lead:j + lead + tt, :] * w_ref[j:j + 1, :]
        y = _layer_norm(acc + cb_ref[...], g_ref[...], b_ref[...])
        o_ref[pl.ds(t0, tt), :] = (y * jax.nn.sigmoid(y)).astype(o_ref.dtype)

    if t == tt:
        tile(0)
    else:
        def body(i, carry):
            tile(pl.multiple_of(i * tt, tt))
            return carry
        lax.fori_loop(0, t // tt, body, 0)


def _conv_tail(prev, u, conv_w, conv_b, ln_g, ln_b):
    b, t, c = u.shape
    tt = min(t, 128)
    vec = lambda a: a.reshape(1, c)
    vspec = pl.BlockSpec((1, c), lambda i: (0, 0))
    return pl.pallas_call(
        functools.partial(_conv_kernel, t=t, tt=tt),
        out_shape=jax.ShapeDtypeStruct((b, t, c), BF16),
        grid=(b,),
        in_specs=[pl.BlockSpec((None, CONV_HALO, c), lambda i: (i, 0, 0)),
                  pl.BlockSpec((None, t, c), lambda i: (i, 0, 0)),
                  pl.BlockSpec((CONV_W, c), lambda i: (0, 0)),
                  vspec, vspec, vspec],
        out_specs=pl.BlockSpec((None, t, c), lambda i: (i, 0, 0)),
        scratch_shapes=[pltpu.VMEM((CONV_HALO + t, c), F32)],
        compiler_params=_cparams(1, VMEM_LIMIT),
        name="conv_tail",
    )(prev, u, conv_w, vec(conv_b), vec(ln_g), vec(ln_b))


def _outproj_tail_kernel(*refs, n_blk):
    tail_ref, h2_ref = refs[13], refs[15]
    i = pl.program_id(0)

    @pl.when(i < n_blk)
    def _():
        _outproj_kernel(*refs[:13], *refs[14:])

    @pl.when(i == n_blk)
    def _():
        h2_ref[...] = tail_ref[...]


def _outproj_kernel(oa_ref, ob_ref, oc_ref, x_ref, g1_ref, sc2_ref, sh2_ref, wo_ref,
                    lng_ref, lnb_ref, wr_hi_ref, wr_lo_ref, br_ref,
                    x1_ref, h2_ref, idx_ref, gate_ref):
    mix = (_dot(oa_ref[...], wo_ref[0:A_WIDTH, :])
           + _dot(ob_ref[...], wo_ref[A_WIDTH:A_WIDTH + B_WIDTH, :])
           + _dot(oc_ref[...], wo_ref[A_WIDTH + B_WIDTH:, :]))
    x1 = _layer_norm(DEEPNORM_ALPHA * x_ref[...] + g1_ref[...] * mix, lng_ref[...], lnb_ref[...])
    x1_ref[...] = x1
    h2 = x1 * (1.0 + sc2_ref[...]) + sh2_ref[...]
    _store_token_tiles(h2_ref, h2)
    h_hi, h_lo = _split_bf16(h2)
    logits = (_dot(h_hi, wr_hi_ref[...]) + _dot(h_lo, wr_hi_ref[...])
              + _dot(h_hi, wr_lo_ref[...]) + br_ref[...])
    lane = lax.broadcasted_iota(jnp.int32, logits.shape, 1)
    cur = jnp.where(lane < N_EXPERTS, logits, NEG_INF)
    vals, idxs = [], []
    for _ in range(TOP_K):
        m = jnp.max(cur, axis=-1, keepdims=True)
        i = jnp.min(jnp.where(cur == m, lane, LANES), axis=-1, keepdims=True)
        vals.append(m)
        idxs.append(i)
        cur = jnp.where(lane == i, NEG_INF, cur)
    es = [jnp.exp(v - vals[0]) for v in vals]
    den = es[0] + es[1] + es[2] + es[3]
    idx_out = jnp.zeros(logits.shape, jnp.int32)
    gate_out = jnp.zeros(logits.shape, F32)
    for k in range(TOP_K):
        idx_out = jnp.where(lane == k, idxs[k], idx_out)
        gate_out = jnp.where(lane == k, es[k] / den, gate_out)
    idx_ref[...] = idx_out
    gate_ref[...] = gate_out


def _outproj(oa, ob, oc, x, g1, sc2, sh2, wo_bf, ln_g, ln_b, wr_hi, wr_lo, br,
             *, tm, rows_per_batch, batch_off, h2_tail=None):
    m, d = x.shape
    bpb = rows_per_batch // tm
    n_blk = m // tm
    last = n_blk - 1
    row = lambda w: pl.BlockSpec((tm, w), lambda i: (jnp.minimum(i, last), 0))
    mod_spec = pl.BlockSpec((None, 1, d),
                            lambda i: (jnp.minimum(i, last) // bpb + batch_off, 0, 0))
    const = lambda a: pl.BlockSpec(a.shape, lambda i: (0, 0))
    in_specs = [row(A_WIDTH), row(B_WIDTH), row(C_CH), row(d), mod_spec, mod_spec, mod_spec,
                const(wo_bf), const(ln_g), const(ln_b), const(wr_hi), const(wr_lo), const(br)]
    args = [oa, ob, oc, x, g1, sc2, sh2, wo_bf, ln_g, ln_b, wr_hi, wr_lo, br]
    if h2_tail is None:
        kern, steps, h2_rows = _outproj_kernel, n_blk, m
    else:
        assert h2_tail.shape == (tm * ROW_PIECES, LANES)
        kern = functools.partial(_outproj_tail_kernel, n_blk=n_blk)
        steps, h2_rows = n_blk + 1, m + tm
        in_specs.append(const(h2_tail))
        args.append(h2_tail)
    return pl.pallas_call(
        kern,
        out_shape=[jax.ShapeDtypeStruct((m, d), F32),
                   jax.ShapeDtypeStruct((h2_rows * ROW_PIECES, LANES), F32),
                   jax.ShapeDtypeStruct((m, LANES), jnp.int32),
                   jax.ShapeDtypeStruct((m, LANES), F32)],
        grid=(steps,),
        in_specs=in_specs,
        out_specs=[row(d), pl.BlockSpec((tm * ROW_PIECES, LANES), lambda i: (i, 0)),
                   row(LANES), row(LANES)],
        compiler_params=_cparams(1, VMEM_LIMIT),
        name="outproj_route",
    )(*args)


def _moe_kernel(be_ref, nu_ref, idx_hbm, h2_hbm, w1_ref, b1_ref, w2_ref, b2_ref,
                picked_hbm, idx_smem, xbuf, obuf, w1b, w2b, sem_idx, sem_g, sem_s, *, tm):
    i = pl.program_id(0)
    nu = nu_ref[0]
    rp = ROW_PIECES

    def table_copy(blk, slot):
        return pltpu.make_async_copy(
            idx_hbm.at[pl.ds(pl.multiple_of(blk * IDX_STRIDE, IDX_STRIDE), IDX_STRIDE)],
            idx_smem.at[pl.ds(pl.multiple_of(slot * IDX_STRIDE, IDX_STRIDE), IDX_STRIDE)],
            sem_idx.at[slot])

    def row_loop(body, static_rows):
        if static_rows:
            for r in range(tm):
                body(r)
        else:
            lax.fori_loop(0, tm, lambda r, c: (body(r), c)[1], 0)

    def start_gathers(tslot, bslot, static_rows=True):
        base = tslot * IDX_STRIDE

        def body(r):
            src = pl.multiple_of(idx_smem[base + r], rp)
            row0 = r * rp if isinstance(r, int) else pl.multiple_of(r * rp, rp)
            pltpu.make_async_copy(
                h2_hbm.at[pl.ds(src, rp), :], xbuf.at[bslot, pl.ds(row0, rp), :],
                sem_g.at[bslot]).start(priority=r % 2 if isinstance(r, int) else 0)
        row_loop(body, static_rows)

    def wait_gathers(bslot):
        pltpu.make_async_copy(h2_hbm.at[pl.ds(0, tm * rp), :], xbuf.at[bslot],
                              sem_g.at[bslot]).wait()

    def start_scatters(tslot, bslot):
        base = tslot * IDX_STRIDE + tm

        def body(r):
            dst = pl.multiple_of(idx_smem[base + r], rp)
            pltpu.make_async_copy(obuf.at[bslot, pl.ds(r * rp, rp), :],
                                  picked_hbm.at[pl.ds(dst, rp), :],
                                  sem_s.at[bslot]).start(priority=r % 2)
        row_loop(body, True)

    def wait_scatters(bslot):
        pltpu.make_async_copy(obuf.at[bslot], picked_hbm.at[pl.ds(0, tm * rp), :],
                              sem_s.at[bslot]).wait()

    @pl.when(i < nu)
    def _():
        bslot = i % 2
        tslot = i % MOE_IDX_SLOTS

        @pl.when(i == 0)
        def _():
            first = table_copy(0, 0)
            first.start()
            first.wait()
            start_gathers(0, 0, static_rows=False)

            @pl.when(nu > 1)
            def _():
                table_copy(1, 1).start()

        @pl.when(i + 2 < nu)
        def _():
            table_copy(i + 2, (i + 2) % MOE_IDX_SLOTS).start()

        @pl.when(i + 1 < nu)
        def _():
            nslot = (i + 1) % MOE_IDX_SLOTS
            table_copy(i + 1, nslot).wait()
            start_gathers(nslot, 1 - bslot)

        @pl.when(jnp.logical_or(i == 0, be_ref[i] != be_ref[jnp.maximum(i - 1, 0)]))
        def _():
            w1b[...] = w1_ref[...].astype(BF16)
            w2b[...] = w2_ref[...].astype(BF16)

        wait_gathers(bslot)

        @pl.when(i >= 2)
        def _():
            wait_scatters(bslot)

        x = _load_token_tiles(xbuf.at[bslot], tm).astype(BF16)
        x_glu = jnp.minimum(_dot(x, w1b[:, 0:D_FF]) + b1_ref[:, 0:D_FF], SWIGLU_LIMIT)
        x_lin = jnp.clip(_dot(x, w1b[:, D_FF:]) + b1_ref[:, D_FF:], -SWIGLU_LIMIT, SWIGLU_LIMIT)
        act = x_glu * jax.nn.sigmoid(SWIGLU_ALPHA * x_glu) * (x_lin + 1.0)
        _store_token_tiles(obuf.at[bslot], _dot(act.astype(BF16), w2b[...]) + b2_ref[...])
        start_scatters(tslot, bslot)

        @pl.when(i == nu - 1)
        def _():
            @pl.when(i >= 1)
            def _():
                wait_scatters(1 - bslot)
            wait_scatters(bslot)

    @pl.when(i >= nu)
    def _():
        @pl.when(i == nu)
        def _():
            obuf[0] = jnp.zeros(obuf.shape[1:], F32)
        cp = pltpu.make_async_copy(
            obuf.at[0],
            picked_hbm.at[pl.ds(pl.multiple_of(i * (tm * rp), tm * rp), tm * rp), :], sem_s.at[0])
        cp.start()
        cp.wait()


def _moe(block_e, n_used, idx_flat, h2, w1, b1, w2, b2, *, n_rows, layer):
    tm = MOE_TILE
    n_blocks = block_e.shape[0]
    d = D_MODEL
    tile_rows = tm * ROW_PIECES
    expert = lambda i, be, nu: (layer, be[i], 0, 0)
    grid_spec = pltpu.PrefetchScalarGridSpec(
        num_scalar_prefetch=2,
        grid=(n_blocks,),
        in_specs=[pl.BlockSpec(memory_space=pl.ANY),
                  pl.BlockSpec(memory_space=pl.ANY),
                  pl.BlockSpec((None, None, d, 2 * D_FF), expert),
                  pl.BlockSpec((None, None, 1, 2 * D_FF), expert),
                  pl.BlockSpec((None, None, D_FF, d), expert),
                  pl.BlockSpec((None, None, 1, d), expert)],
        out_specs=pl.BlockSpec(memory_space=pl.ANY),
        scratch_shapes=[pltpu.SMEM((MOE_IDX_SLOTS * IDX_STRIDE,), jnp.int32),
                        pltpu.VMEM((2, tile_rows, LANES), F32),
                        pltpu.VMEM((2, tile_rows, LANES), F32),
                        pltpu.VMEM((d, 2 * D_FF), BF16), pltpu.VMEM((D_FF, d), BF16),
                        pltpu.SemaphoreType.DMA((MOE_IDX_SLOTS,)),
                        pltpu.SemaphoreType.DMA((2,)), pltpu.SemaphoreType.DMA((2,))])
    return pl.pallas_call(
        functools.partial(_moe_kernel, tm=tm),
        out_shape=jax.ShapeDtypeStruct((n_rows * ROW_PIECES, LANES), F32),
        grid_spec=grid_spec,
        compiler_params=_cparams(1, VMEM_LIMIT),
        name="moe_experts",
    )(block_e, n_used, idx_flat, h2, w1, b1.reshape(b1.shape[0], N_EXPERTS, 1, -1), w2,
      b2.reshape(b2.shape[0], N_EXPERTS, 1, -1))


def _routing_tables(top_idx, n_tok):
    tm = MOE_TILE
    n_assign = n_tok * TOP_K
    n_blocks = -(-n_assign // tm) + N_EXPERTS
    n_rows = n_blocks * tm
    i32 = jnp.int32
    flat_e = top_idx.reshape(n_assign)
    key_bits = (n_assign - 1).bit_length()
    assert N_EXPERTS << key_bits < 2 ** 31
    keys = jnp.sort(flat_e * (1 << key_bits) + jnp.arange(n_assign, dtype=i32))
    order = keys & ((1 << key_bits) - 1)
    experts = jnp.arange(N_EXPERTS, dtype=i32)
    counts = jnp.sum((flat_e[:, None] == experts[None, :]).astype(i32), axis=0)
    padded = (counts + tm - 1) // tm * tm
    pad_end = jnp.cumsum(padded)
    pad_start = pad_end - padded
    grp_start = jnp.cumsum(counts) - counts
    n_used = pad_end[-1] // tm
    blk = jnp.arange(n_blocks, dtype=i32)
    used = blk < n_used
    expert_at = lambda start: jnp.minimum(
        jnp.sum((pad_end[None, :] <= start[:, None]).astype(i32), axis=1), N_EXPERTS - 1)
    e_blk = expert_at(blk * tm)
    e_last = expert_at(((n_used - 1) * tm).reshape(1))[0]
    block_e = jnp.where(used, e_blk, e_last)
    row = blk[:, None] * tm + jnp.arange(tm, dtype=i32)[None, :]
    off = row - pad_start[e_blk][:, None]
    cnt = counts[e_blk][:, None]
    grp = grp_start[e_blk][:, None]
    valid = used[:, None] & (off < cnt)
    a = order[jnp.clip(grp + off, 0, n_assign - 1)]
    tok = a // TOP_K
    row_tok = jnp.where(valid, tok, 0)
    real_before = jnp.where(used[:, None], grp + cnt, n_assign)
    row_dst = jnp.where(valid, (a % TOP_K) * n_tok + tok, n_assign + row - real_before)
    idx = jnp.concatenate(
        [row_tok * ROW_PIECES, row_dst * ROW_PIECES,
         jnp.zeros((n_blocks, IDX_STRIDE - 2 * tm), i32)], axis=1)
    return block_e, n_used.reshape(1).astype(i32), idx.reshape(-1), n_rows


def _combine_kernel(p0_ref, p1_ref, p2_ref, p3_ref, gate_ref, x1_ref, g2_ref,
                    lng_ref, lnb_ref, o_ref):
    gate = gate_ref[...]
    n = gate.shape[0]
    y = (gate[:, 0:1] * _load_token_tiles(p0_ref, n) + gate[:, 1:2] * _load_token_tiles(p1_ref, n)
         + gate[:, 2:3] * _load_token_tiles(p2_ref, n) + gate[:, 3:4] * _load_token_tiles(p3_ref, n))
    o_ref[...] = _layer_norm(DEEPNORM_ALPHA * x1_ref[...] + g2_ref[...] * y,
                             lng_ref[...], lnb_ref[...])


def _combine(picked, gate, x1, g2, ln_g, ln_b, *, tm, rows_per_batch, batch_off, row_off, n_tok):
    m, d = x1.shape
    bpb = rows_per_batch // tm
    blk_off = row_off // tm
    k_stride = n_tok // tm
    pspec = lambda k: pl.BlockSpec((tm * ROW_PIECES, LANES),
                                   lambda i: (i + blk_off + k * k_stride, 0))
    row = lambda w: pl.BlockSpec((tm, w), lambda i: (i, 0))
    const = lambda a: pl.BlockSpec(a.shape, lambda i: (0, 0))
    return pl.pallas_call(
        _combine_kernel,
        out_shape=jax.ShapeDtypeStruct((m, d), F32),
        grid=(m // tm,),
        in_specs=[pspec(0), pspec(1), pspec(2), pspec(3), row(LANES), row(d),
                  pl.BlockSpec((None, 1, d), lambda i: (i // bpb + batch_off, 0, 0)),
                  const(ln_g), const(ln_b)],
        out_specs=row(d),
        compiler_params=_cparams(1, VMEM_LIMIT),
        name="combine_ln",
    )(picked, picked, picked, picked, gate, x1, g2, ln_g, ln_b)


def _clipped_rel_bias(rel, table):
    idx = np.clip(rel, -A_REL_CLIP, A_REL_CLIP) + A_REL_CLIP
    return jnp.moveaxis(table[idx].astype(F32), -1, 0)


def _t5_bucket(rel):
    nb = T5_BUCKETS // 2
    max_exact = nb // 2
    n = np.abs(rel)
    nf = np.maximum(n, 1).astype(np.float32)
    large = max_exact + (np.log(nf / max_exact) / math.log(T5_MAX_DISTANCE / max_exact)
                         * (nb - max_exact)).astype(np.int32)
    large = np.minimum(large, nb - 1)
    return np.where(rel > 0, nb, 0) + np.where(n < max_exact, n, large)


def _t5_rel_bias(rel, table):
    return jnp.moveaxis(table[_t5_bucket(rel)].astype(F32), -1, 0)


def kernel(x_prompt, x_sample, cache_a_k, cache_a_v, cache_b_k, cache_b_v, state_conv,
           c_prompt, c_sample, w_in, w_out, rel_bias_a, t5_bias, sinks, conv_w, conv_b,
           conv_ln_g, conv_ln_b, w_ada, b_ada, ln_g, ln_b, w_router, b_router,
           w_e_in, b_e_in, w_e_out, b_e_out):
    bp, tp, d = x_prompt.shape
    bs, ts, _ = x_sample.shape
    mp, ms = bp * tp, bs * ts
    n_tok = mp + ms
    na, nb = cache_a_k.shape[2], cache_b_k.shape[2]
    assert tp % ROW_TILE == 0 and mp % ts == 0 and ts >= CONV_W - 1 and tp >= A_REACH
    assert ms == ROW_TILE

    xp = x_prompt.reshape(mp, d)
    xs = x_sample.reshape(ms, d)
    c_all = jnp.concatenate([c_prompt, c_sample], axis=0)

    t5_p = _band_bias(lambda rel: _t5_rel_bias(rel, t5_bias), **B_BAND)
    t5_s = _rel_bias_tile(lambda rel: _t5_rel_bias(rel, t5_bias), ts, nb + ts, nb)
    conv_zero = jnp.zeros((bp, CONV_HALO, C_CH), F32)

    states_p, states_s = [], []
    for l in range(DEPTH):
        mod = _adaln(c_all, w_ada, b_ada, l)
        sh1, sc1, g1, sh2, sc2, g2 = (mod[:, j * d:(j + 1) * d].reshape(bp + bs, 1, d)
                                      for j in range(6))
        w_in_bf = w_in[l].astype(BF16)
        w_out_bf = w_out[l].astype(BF16)
        wr = jnp.pad(w_router[l], ((0, 0), (0, LANES - N_EXPERTS)))
        wr_hi = wr.astype(BF16)
        wr_lo = (wr - wr_hi.astype(F32)).astype(BF16)
        br = jnp.pad(b_router[l], (0, LANES - N_EXPERTS)).reshape(1, LANES)
        lng1, lnb1 = ln_g[l, 0].reshape(1, d), ln_b[l, 0].reshape(1, d)
        lng2, lnb2 = ln_g[l, 1].reshape(1, d), ln_b[l, 1].reshape(1, d)
        bias_a_p = _band_bias(lambda rel: _clipped_rel_bias(rel, rel_bias_a[l]), **A_BAND)
        bias_a_s = _rel_bias_tile(lambda rel: _clipped_rel_bias(rel, rel_bias_a[l]),
                                  ts, na + ts, na)
        conv_args = (conv_w[l], conv_b[l], conv_ln_g[l], conv_ln_b[l])

        qa, ka, va, qb, kb, vb, u = _inproj(xs, sc1, sh1, w_in_bf, tm=ts,
                                            rows_per_batch=ts, batch_off=bp)
        s3 = lambda a: a.reshape(bs, ts, a.shape[-1])
        ka_all = jnp.concatenate([cache_a_k[l].reshape(bs, na, A_WIDTH), s3(ka)], axis=1)
        va_all = jnp.concatenate([cache_a_v[l].reshape(bs, na, A_WIDTH), s3(va)], axis=1)
        kb_all = jnp.concatenate([cache_b_k[l].reshape(bs, nb, B_KV_WIDTH), s3(kb)], axis=1)
        vb_all = jnp.concatenate([cache_b_v[l].reshape(bs, nb, B_KV_WIDTH), s3(vb)], axis=1)
        oa = _attention(s3(qa), ka_all, va_all, bias_a_s, None,
                        cq=ts, n_prev=None, n_heads=A_HEADS, group=1)
        ob = _attention(s3(qb), kb_all, vb_all, t5_s, sinks[l],
                        cq=ts, n_prev=None, n_heads=B_HEADS, group=B_GROUP)
        prev = jnp.pad(state_conv[l], ((0, 0), (CONV_HALO - (CONV_W - 1), 0), (0, 0)))
        oc = _conv_tail(prev, s3(u), *conv_args)
        x1s, h2s, idxs, gates = _outproj(
            oa.reshape(ms, -1), ob.reshape(ms, -1), oc.reshape(ms, -1), xs, g1, sc2, sh2,
            w_out_bf, lng1, lnb1, wr_hi, wr_lo, br, tm=ts, rows_per_batch=ts, batch_off=bp)
        u_ext = jnp.concatenate([state_conv[l], s3(u)], axis=1)
        states_s.append((
            ka_all[:, -na:].reshape(bs, na, A_HEADS, HEAD_DIM),
            va_all[:, -na:].reshape(bs, na, A_HEADS, HEAD_DIM),
            kb_all[:, -nb:].reshape(bs, nb, B_KV_HEADS, HEAD_DIM),
            vb_all[:, -nb:].reshape(bs, nb, B_KV_HEADS, HEAD_DIM),
            u_ext[:, -(CONV_W - 1):]))

        qa, ka, va, qb, kb, vb, u = _inproj(xp, sc1, sh1, w_in_bf, tm=ROW_TILE,
                                            rows_per_batch=tp, batch_off=0)
        r3 = lambda a: a.reshape(bp, tp, a.shape[-1])
        oa = _band_attention(r3(qa), r3(ka), r3(va), bias_a_p, None, **A_BAND)
        ob = _band_attention(r3(qb), r3(kb), r3(vb), t5_p, sinks[l], **B_BAND)
        oc = _conv_tail(conv_zero, r3(u), *conv_args)
        x1p, h2, idxp, gatep = _outproj(
            oa.reshape(mp, -1), ob.reshape(mp, -1), oc.reshape(mp, -1), xp, g1, sc2, sh2,
            w_out_bf, lng1, lnb1, wr_hi, wr_lo, br, tm=ROW_TILE, rows_per_batch=tp, batch_off=0,
            h2_tail=h2s)
        states_p.append((
            r3(ka)[:, tp - A_REACH:].reshape(bp, A_REACH, A_HEADS, HEAD_DIM),
            r3(va)[:, tp - A_REACH:].reshape(bp, A_REACH, A_HEADS, HEAD_DIM),
            r3(kb)[:, tp - B_WINDOW:].reshape(bp, B_WINDOW, B_KV_HEADS, HEAD_DIM),
            r3(vb)[:, tp - B_WINDOW:].reshape(bp, B_WINDOW, B_KV_HEADS, HEAD_DIM),
            r3(u)[:, tp - (CONV_W - 1):]))

        top_idx = jnp.concatenate([idxp[:, :TOP_K], idxs[:, :TOP_K]], axis=0)
        block_e, n_used, idx_flat, n_rows = _routing_tables(top_idx, n_tok)
        picked = _moe(block_e, n_used, idx_flat, h2, w_e_in, b_e_in, w_e_out, b_e_out,
                      n_rows=n_rows, layer=l)
        xp = _combine(picked, gatep, x1p, g2, lng2, lnb2, tm=ROW_TILE, rows_per_batch=tp,
                      batch_off=0, row_off=0, n_tok=n_tok)
        xs = _combine(picked, gates, x1s, g2, lng2, lnb2, tm=ts, rows_per_batch=ts,
                      batch_off=bp, row_off=mp, n_tok=n_tok)

    a_k_p, a_v_p, b_k_p, b_v_p, conv_p = (jnp.stack(z) for z in zip(*states_p))
    a_k_s, a_v_s, b_k_s, b_v_s, conv_s = (jnp.stack(z) for z in zip(*states_s))
    return (xp.reshape(bp, tp, d), xs.reshape(bs, ts, d), a_k_p, a_v_p, b_k_p, b_v_p, conv_p,
            a_k_s, a_v_s, b_k_s, b_v_s, conv_s)
```

```python
import functools
import math

import jax
import jax.numpy as jnp
import numpy as np
from jax import lax
from jax.experimental import pallas as pl
from jax.experimental.pallas import tpu as pltpu

F32 = jnp.float32
BF16 = jnp.bfloat16

D_MODEL = 1024
DEPTH = 2
CHUNK = 64
HEAD_DIM = 64
ATTN_SCALE = HEAD_DIM ** -0.5
A_HEADS = 4
A_WIDTH = A_HEADS * HEAD_DIM
A_PREV_CHUNKS = 8
A_REACH = A_PREV_CHUNKS * CHUNK
A_REL_CLIP = 128
B_HEADS = 8
B_KV_HEADS = 2
B_GROUP = B_HEADS // B_KV_HEADS
B_WIDTH = B_HEADS * HEAD_DIM
B_KV_WIDTH = B_KV_HEADS * HEAD_DIM
B_WINDOW = 128
B_PREV_CHUNKS = B_WINDOW // CHUNK
T5_BUCKETS = 32
T5_MAX_DISTANCE = 128
C_CH = D_MODEL // 4
CONV_W = 31
CONV_HALO = 32
N_EXPERTS = 32
TOP_K = 4
D_FF = D_MODEL
SWIGLU_LIMIT = 7.0
SWIGLU_ALPHA = 1.702
DEEPNORM_ALPHA = (2 * DEPTH) ** 0.25
LN_EPS = 1e-5
NEG_INF = -1e30

LANES = 128
ROW_TILE = 512
MOE_TILE = 256
IDX_STRIDE = 1024
MOE_IDX_SLOTS = 4
A_BAND = dict(g_chunks=4, n_prev=A_PREV_CHUNKS, n_kv=A_HEADS, group=1)
B_BAND = dict(g_chunks=2, n_prev=B_PREV_CHUNKS, n_kv=B_KV_HEADS, group=B_GROUP)
VMEM_LIMIT = 56 * 1024 * 1024


def _cparams(n_axes=1, vmem=None):
    return pltpu.CompilerParams(dimension_semantics=("arbitrary",) * n_axes,
                                vmem_limit_bytes=vmem)


def _dot(a, b):
    return jnp.dot(a, b, preferred_element_type=F32)


def _layer_norm(z, g, b):
    mu = jnp.mean(z, axis=-1, keepdims=True)
    d = z - mu
    var = jnp.mean(d * d, axis=-1, keepdims=True)
    return d * lax.rsqrt(var + LN_EPS) * g + b


SUBLANES = 8
ROW_PIECES = D_MODEL // LANES


def _store_token_tiles(ref, val):
    n = val.shape[0]
    for c in range(ROW_PIECES):
        ref[pl.ds(c, n, stride=ROW_PIECES), :] = val[:, c * LANES:(c + 1) * LANES]


def _load_token_tiles(ref, n):
    return jnp.concatenate(
        [ref[pl.ds(c, n, stride=ROW_PIECES), :] for c in range(ROW_PIECES)], axis=1)


def _split_bf16(a):
    hi = a.astype(BF16)
    lo = (a - hi.astype(F32)).astype(BF16)
    return hi, lo


def _adaln_kernel(c_ref, w_ref, b_ref, o_ref):
    c = c_ref[...]
    a_hi, a_lo = _split_bf16(c * jax.nn.sigmoid(c))
    w_hi, w_lo = _split_bf16(w_ref[...])
    o_ref[...] = _dot(a_hi, w_hi) + _dot(a_lo, w_hi) + _dot(a_hi, w_lo) + b_ref[...]


def _adaln(c_all, w_all, b_all, layer):
    nb, d = c_all.shape
    n = w_all.shape[2]
    tn = 1536
    return pl.pallas_call(
        _adaln_kernel,
        out_shape=jax.ShapeDtypeStruct((nb, n), F32),
        grid=(n // tn,),
        in_specs=[pl.BlockSpec((nb, d), lambda j: (0, 0)),
                  pl.BlockSpec((None, d, tn), lambda j: (layer, 0, j)),
                  pl.BlockSpec((None, 1, tn), lambda j: (layer, 0, j))],
        out_specs=pl.BlockSpec((nb, tn), lambda j: (0, j)),
        compiler_params=_cparams(1, VMEM_LIMIT),
        name="adaln",
    )(c_all, w_all, b_all.reshape(b_all.shape[0], 1, n))


_QA = (0, A_WIDTH)
_KA = (A_WIDTH, 2 * A_WIDTH)
_VA = (2 * A_WIDTH, 3 * A_WIDTH)
_QB = (3 * A_WIDTH, 3 * A_WIDTH + B_WIDTH)
_KB = (_QB[1], _QB[1] + B_KV_WIDTH)
_VB = (_KB[1], _KB[1] + B_KV_WIDTH)
_GA = (_VB[1], _VB[1] + C_CH)
_GG = (_GA[1], _GA[1] + C_CH)


def _inproj_kernel(x_ref, sc_ref, sh_ref, w_ref,
                   qa_ref, ka_ref, va_ref, qb_ref, kb_ref, vb_ref, u_ref):
    h = (x_ref[...] * (1.0 + sc_ref[...]) + sh_ref[...]).astype(BF16)

    def proj(cols):
        return _dot(h, w_ref[:, cols[0]:cols[1]])

    qa_ref[...] = (proj(_QA) * ATTN_SCALE).astype(BF16)
    ka_ref[...] = proj(_KA)
    va_ref[...] = proj(_VA)
    qb_ref[...] = (proj(_QB) * ATTN_SCALE).astype(BF16)
    kb_ref[...] = proj(_KB)
    vb_ref[...] = proj(_VB)
    u_ref[...] = proj(_GA) * jax.nn.sigmoid(proj(_GG))


def _inproj(x, sc, sh, w_bf, *, tm, rows_per_batch, batch_off):
    m, d = x.shape
    bpb = rows_per_batch // tm
    mod_spec = pl.BlockSpec((None, 1, d), lambda i: (i // bpb + batch_off, 0, 0))
    widths = (A_WIDTH, A_WIDTH, A_WIDTH, B_WIDTH, B_KV_WIDTH, B_KV_WIDTH, C_CH)
    dtypes = (BF16, F32, F32, BF16, F32, F32, F32)
    return pl.pallas_call(
        _inproj_kernel,
        out_shape=[jax.ShapeDtypeStruct((m, w), dt) for w, dt in zip(widths, dtypes)],
        grid=(m // tm,),
        in_specs=[pl.BlockSpec((tm, d), lambda i: (i, 0)), mod_spec, mod_spec,
                  pl.BlockSpec(w_bf.shape, lambda i: (0, 0))],
        out_specs=[pl.BlockSpec((tm, w), lambda i: (i, 0)) for w in widths],
        compiler_params=_cparams(1, VMEM_LIMIT),
        name="inproj",
    )(x, sc, sh, w_bf)


def _attn_kernel(*refs, nc, cq, bw, pad, n_heads, group, use_sink):
    if use_sink:
        q_ref, k_ref, v_ref, bias_ref, sink_ref, o_ref, kp_ref, vp_ref = refs
    else:
        q_ref, k_ref, v_ref, bias_ref, o_ref, kp_ref, vp_ref = refs
        sink_ref = None
    tk = k_ref.shape[0]
    if pad:
        zeros = jnp.zeros((pad, kp_ref.shape[1]), BF16)
        kp_ref[0:pad, :] = zeros
        vp_ref[0:pad, :] = zeros
    kp_ref[pad:pad + tk, :] = k_ref[...].astype(BF16)
    vp_ref[pad:pad + tk, :] = v_ref[...].astype(BF16)

    def chunk(c):
        q0 = c * cq
        if not isinstance(c, int):
            q0 = pl.multiple_of(q0, cq)
        qt = q_ref[pl.ds(q0, cq), :]
        kt = kp_ref[pl.ds(q0, bw), :]
        vt = vp_ref[pl.ds(q0, bw), :]
        if pad:
            key_pos = lax.broadcasted_iota(jnp.int32, (cq, bw), 1)
            valid = key_pos >= pad - q0
        outs = []
        for h in range(n_heads):
            n = h // group
            qh = qt[:, h * HEAD_DIM:(h + 1) * HEAD_DIM]
            kh = kt[:, n * HEAD_DIM:(n + 1) * HEAD_DIM]
            vh = vt[:, n * HEAD_DIM:(n + 1) * HEAD_DIM]
            s = lax.dot_general(qh, kh, (((1,), (1,)), ((), ())),
                                preferred_element_type=F32)
            s = s + bias_ref[h]
            if pad:
                s = jnp.where(valid, s, NEG_INF)
            m = jnp.max(s, axis=-1, keepdims=True)
            if use_sink:
                m = jnp.maximum(m, sink_ref[h])
            e = jnp.exp(s - m)
            den = jnp.sum(e, axis=-1, keepdims=True)
            if use_sink:
                den = den + jnp.exp(sink_ref[h] - m)
            outs.append(_dot(e.astype(BF16), vh) / den)
        o_ref[pl.ds(q0, cq), :] = jnp.concatenate(outs, axis=-1).astype(o_ref.dtype)

    if nc == 1:
        chunk(0)
    else:
        def body(c, carry):
            chunk(c)
            return carry
        lax.fori_loop(0, nc, body, 0)


def _attention(q, k, v, bias, sinks, *, cq, n_prev, n_heads, group):
    b, t, qw = q.shape
    tk, kw = k.shape[1], k.shape[2]
    if n_prev is None:
        nc, bw, pad = 1, tk, 0
    else:
        nc, bw, pad = t // cq, (n_prev + 1) * cq, n_prev * cq
    use_sink = sinks is not None
    kern = functools.partial(_attn_kernel, nc=nc, cq=cq, bw=bw, pad=pad,
                             n_heads=n_heads, group=group, use_sink=use_sink)
    in_specs = [pl.BlockSpec((None, t, qw), lambda i: (i, 0, 0)),
                pl.BlockSpec((None, tk, kw), lambda i: (i, 0, 0)),
                pl.BlockSpec((None, tk, kw), lambda i: (i, 0, 0)),
                pl.BlockSpec(bias.shape, lambda i: (0, 0, 0))]
    args = [q, k, v, bias]
    if use_sink:
        in_specs.append(pl.BlockSpec(memory_space=pltpu.SMEM))
        args.append(sinks)
    return pl.pallas_call(
        kern,
        out_shape=jax.ShapeDtypeStruct((b, t, qw), BF16),
        grid=(b,),
        in_specs=in_specs,
        out_specs=pl.BlockSpec((None, t, qw), lambda i: (i, 0, 0)),
        scratch_shapes=[pltpu.VMEM((pad + tk, kw), BF16), pltpu.VMEM((pad + tk, kw), BF16)],
        compiler_params=_cparams(1, VMEM_LIMIT),
        name="attn_sink" if use_sink else "attn",
    )(*args)


def _band_attn_kernel(*refs, t, g_chunks, n_prev, n_kv, group, use_sink):
    if use_sink:
        q_ref, k_ref, v_ref, bias_ref, sink_ref, o_ref, qs_ref, kp_ref, vp_ref = refs
    else:
        q_ref, k_ref, v_ref, bias_ref, o_ref, qs_ref, kp_ref, vp_ref = refs
    pad = n_prev * CHUNK
    gq = g_chunks * CHUNK
    u = pad + gq
    m = group * gq
    hd = HEAD_DIM
    for n in range(n_kv):
        zeros = jnp.zeros((pad, hd), BF16)
        kp_ref[n, 0:pad, :] = zeros
        vp_ref[n, 0:pad, :] = zeros
        kp_ref[n, pad:pad + t, :] = k_ref[:, n * hd:(n + 1) * hd].astype(BF16)
        vp_ref[n, pad:pad + t, :] = v_ref[:, n * hd:(n + 1) * hd].astype(BF16)
    for h in range(n_kv * group):
        qs_ref[h] = q_ref[:, h * hd:(h + 1) * hd]
    key_pos = lax.broadcasted_iota(jnp.int32, (m, u), 1)
    row = lax.broadcasted_iota(jnp.int32, (m, 1), 0)

    def body(g, carry):
        q0 = pl.multiple_of(g * gq, gq)
        valid = key_pos >= pad - q0
        outs = []
        for n in range(n_kv):
            qstk = jnp.concatenate(
                [qs_ref[n * group + j, pl.ds(q0, gq), :] for j in range(group)], axis=0)
            kt = kp_ref[n, pl.ds(q0, u), :]
            vt = vp_ref[n, pl.ds(q0, u), :]
            s = lax.dot_general(qstk, kt, (((1,), (1,)), ((), ())),
                                preferred_element_type=F32)
            s = jnp.where(valid, s + bias_ref[n], NEG_INF)
            mx = jnp.max(s, axis=-1, keepdims=True)
            if use_sink:
                sink = jnp.full((m, 1), sink_ref[n * group], F32)
                for j in range(1, group):
                    sink = jnp.where(row >= j * gq, sink_ref[n * group + j], sink)
                mx = jnp.maximum(mx, sink)
            e = jnp.exp(s - mx)
            den = jnp.sum(e, axis=-1, keepdims=True)
            if use_sink:
                den = den + jnp.exp(sink - mx)
            o = _dot(e.astype(BF16), vt) / den
            outs.extend(o[j * gq:(j + 1) * gq, :] for j in range(group))
        o_ref[pl.ds(q0, gq), :] = jnp.concatenate(outs, axis=-1).astype(o_ref.dtype)
        return carry
    lax.fori_loop(0, t // gq, body, 0)


def _band_bias(head_bias, *, g_chunks, n_prev, n_kv, group):
    pad, gq = n_prev * CHUNK, g_chunks * CHUNK
    u = pad + gq
    r = np.arange(gq)[:, None]
    kk = np.arange(u)[None, :]
    lo = (r // CHUNK) * CHUNK
    in_band = (kk >= lo) & (kk < lo + pad + CHUNK)
    tile = jnp.where(in_band[None], _rel_bias_tile(head_bias, gq, u, pad), NEG_INF)
    return tile.reshape(n_kv, group * gq, u)


def _rel_bias_tile(head_bias, rows, cols, pad):
    n_off = rows + cols - 1
    vec = head_bias(np.arange(n_off) - pad - (rows - 1))
    h = vec.shape[0]
    padded = jnp.concatenate([vec, jnp.zeros((h, 1), vec.dtype)], axis=1)
    skew = jnp.tile(padded, (1, rows))[:, :rows * n_off].reshape(h, rows, n_off)
    return skew[:, :, rows - 1:rows - 1 + cols]


def _band_attention(q, k, v, bias, sinks, *, g_chunks, n_prev, n_kv, group):
    b, t, qw = q.shape
    kw = k.shape[2]
    pad = n_prev * CHUNK
    use_sink = sinks is not None
    kern = functools.partial(_band_attn_kernel, t=t, g_chunks=g_chunks, n_prev=n_prev,
                             n_kv=n_kv, group=group, use_sink=use_sink)
    in_specs = [pl.BlockSpec((None, t, qw), lambda i: (i, 0, 0)),
                pl.BlockSpec((None, t, kw), lambda i: (i, 0, 0)),
                pl.BlockSpec((None, t, kw), lambda i: (i, 0, 0)),
                pl.BlockSpec(bias.shape, lambda i: (0, 0, 0))]
    args = [q, k, v, bias]
    if use_sink:
        in_specs.append(pl.BlockSpec(memory_space=pltpu.SMEM))
        args.append(sinks)
    return pl.pallas_call(
        kern,
        out_shape=jax.ShapeDtypeStruct((b, t, qw), BF16),
        grid=(b,),
        in_specs=in_specs,
        out_specs=pl.BlockSpec((None, t, qw), lambda i: (i, 0, 0)),
        scratch_shapes=[pltpu.VMEM((n_kv * group, t, HEAD_DIM), BF16),
                        pltpu.VMEM((n_kv, pad + t, HEAD_DIM), BF16),
                        pltpu.VMEM((n_kv, pad + t, HEAD_DIM), BF16)],
        compiler_params=_cparams(1, VMEM_LIMIT),
        name="band_attn_sink" if use_sink else "band_attn",
    )(*args)


def _conv_kernel(prev_ref, u_ref, w_ref, cb_ref, g_ref, b_ref, o_ref, up_ref, *, t, tt):
    up_ref[0:CONV_HALO, :] = prev_ref[...]
    up_ref[CONV_HALO:CONV_HALO + t, :] = u_ref[...]
    lead = CONV_HALO - (CONV_W - 1)

    def tile(t0):
        win = up_ref[pl.ds(t0, tt + CONV_HALO), :]
        acc = jnp.zeros((tt, C_CH), F32)
        for j in range(CONV_W):
            acc = acc + win[j + lead:j + lead + tt, :] * w_ref[j:j + 1, :]
        y = _layer_norm(acc + cb_ref[...], g_ref[...], b_ref[...])
        o_ref[pl.ds(t0, tt), :] = (y * jax.nn.sigmoid(y)).astype(o_ref.dtype)

    if t == tt:
        tile(0)
    else:
        def body(i, carry):
            tile(pl.multiple_of(i * tt, tt))
            return carry
        lax.fori_loop(0, t // tt, body, 0)


def _conv_tail(prev, u, conv_w, conv_b, ln_g, ln_b):
    b, t, c = u.shape
    tt = min(t, 128)
    vec = lambda a: a.reshape(1, c)
    vspec = pl.BlockSpec((1, c), lambda i: (0, 0))
    return pl.pallas_call(
        functools.partial(_conv_kernel, t=t, tt=tt),
        out_shape=jax.ShapeDtypeStruct((b, t, c), BF16),
        grid=(b,),
        in_specs=[pl.BlockSpec((None, CONV_HALO, c), lambda i: (i, 0, 0)),
                  pl.BlockSpec((None, t, c), lambda i: (i, 0, 0)),
                  pl.BlockSpec((CONV_W, c), lambda i: (0, 0)),
                  vspec, vspec, vspec],
        out_specs=pl.BlockSpec((None, t, c), lambda i: (i, 0, 0)),
        scratch_shapes=[pltpu.VMEM((CONV_HALO + t, c), F32)],
        compiler_params=_cparams(1, VMEM_LIMIT),
        name="conv_tail",
    )(prev, u, conv_w, vec(conv_b), vec(ln_g), vec(ln_b))


def _outproj_tail_kernel(*refs, n_blk):
    tail_ref, h2_ref = refs[13], refs[15]
    i = pl.program_id(0)

    @pl.when(i < n_blk)
    def _():
        _outproj_kernel(*refs[:13], *refs[14:])

    @pl.when(i == n_blk)
    def _():
        h2_ref[...] = tail_ref[...]


def _outproj_kernel(oa_ref, ob_ref, oc_ref, x_ref, g1_ref, sc2_ref, sh2_ref, wo_ref,
                    lng_ref, lnb_ref, wr_hi_ref, wr_lo_ref, br_ref,
                    x1_ref, h2_ref, idx_ref, gate_ref):
    mix = (_dot(oa_ref[...], wo_ref[0:A_WIDTH, :])
           + _dot(ob_ref[...], wo_ref[A_WIDTH:A_WIDTH + B_WIDTH, :])
           + _dot(oc_ref[...], wo_ref[A_WIDTH + B_WIDTH:, :]))
    x1 = _layer_norm(DEEPNORM_ALPHA * x_ref[...] + g1_ref[...] * mix, lng_ref[...], lnb_ref[...])
    x1_ref[...] = x1
    h2 = x1 * (1.0 + sc2_ref[...]) + sh2_ref[...]
    _store_token_tiles(h2_ref, h2)
    h_hi, h_lo = _split_bf16(h2)
    logits = (_dot(h_hi, wr_hi_ref[...]) + _dot(h_lo, wr_hi_ref[...])
              + _dot(h_hi, wr_lo_ref[...]) + br_ref[...])
    lane = lax.broadcasted_iota(jnp.int32, logits.shape, 1)
    cur = jnp.where(lane < N_EXPERTS, logits, NEG_INF)
    vals, idxs = [], []
    for _ in range(TOP_K):
        m = jnp.max(cur, axis=-1, keepdims=True)
        i = jnp.min(jnp.where(cur == m, lane, LANES), axis=-1, keepdims=True)
        vals.append(m)
        idxs.append(i)
        cur = jnp.where(lane == i, NEG_INF, cur)
    es = [jnp.exp(v - vals[0]) for v in vals]
    den = es[0] + es[1] + es[2] + es[3]
    idx_out = jnp.zeros(logits.shape, jnp.int32)
    gate_out = jnp.zeros(logits.shape, F32)
    for k in range(TOP_K):
        idx_out = jnp.where(lane == k, idxs[k], idx_out)
        gate_out = jnp.where(lane == k, es[k] / den, gate_out)
    idx_ref[...] = idx_out
    gate_ref[...] = gate_out


def _outproj(oa, ob, oc, x, g1, sc2, sh2, wo_bf, ln_g, ln_b, wr_hi, wr_lo, br,
             *, tm, rows_per_batch, batch_off, h2_tail=None):
    m, d = x.shape
    bpb = rows_per_batch // tm
    n_blk = m // tm
    last = n_blk - 1
    row = lambda w: pl.BlockSpec((tm, w), lambda i: (jnp.minimum(i, last), 0))
    mod_spec = pl.BlockSpec((None, 1, d),
                            lambda i: (jnp.minimum(i, last) // bpb + batch_off, 0, 0))
    const = lambda a: pl.BlockSpec(a.shape, lambda i: (0, 0))
    in_specs = [row(A_WIDTH), row(B_WIDTH), row(C_CH), row(d), mod_spec, mod_spec, mod_spec,
                const(wo_bf), const(ln_g), const(ln_b), const(wr_hi), const(wr_lo), const(br)]
    args = [oa, ob, oc, x, g1, sc2, sh2, wo_bf, ln_g, ln_b, wr_hi, wr_lo, br]
    if h2_tail is None:
        kern, steps, h2_rows = _outproj_kernel, n_blk, m
    else:
        assert h2_tail.shape == (tm * ROW_PIECES, LANES)
        kern = functools.partial(_outproj_tail_kernel, n_blk=n_blk)
        steps, h2_rows = n_blk + 1, m + tm
        in_specs.append(const(h2_tail))
        args.append(h2_tail)
    return pl.pallas_call(
        kern,
        out_shape=[jax.ShapeDtypeStruct((m, d), F32),
                   jax.ShapeDtypeStruct((h2_rows * ROW_PIECES, LANES), F32),
                   jax.ShapeDtypeStruct((m, LANES), jnp.int32),
                   jax.ShapeDtypeStruct((m, LANES), F32)],
        grid=(steps,),
        in_specs=in_specs,
        out_specs=[row(d), pl.BlockSpec((tm * ROW_PIECES, LANES), lambda i: (i, 0)),
                   row(LANES), row(LANES)],
        compiler_params=_cparams(1, VMEM_LIMIT),
        name="outproj_route",
    )(*args)


def _moe_kernel(be_ref, nu_ref, idx_hbm, h2_hbm, w1_ref, b1_ref, w2_ref, b2_ref,
                picked_hbm, idx_smem, xbuf, obuf, w1b, w2b, act_ref, sem_idx, sem_g, sem_s,
                *, tm):
    i = pl.program_id(0)
    nu = nu_ref[0]
    rp = ROW_PIECES

    def table_copy(blk, slot):
        return pltpu.make_async_copy(
            idx_hbm.at[pl.ds(pl.multiple_of(blk * IDX_STRIDE, IDX_STRIDE), IDX_STRIDE)],
            idx_smem.at[pl.ds(pl.multiple_of(slot * IDX_STRIDE, IDX_STRIDE), IDX_STRIDE)],
            sem_idx.at[slot])

    def row_loop(body, static_rows):
        if static_rows:
            for r in range(tm):
                body(r)
        else:
            lax.fori_loop(0, tm, lambda r, c: (body(r), c)[1], 0)

    def start_gathers(tslot, bslot, static_rows=True):
        base = tslot * IDX_STRIDE

        def body(r):
            src = pl.multiple_of(idx_smem[base + r], rp)
            row0 = r * rp if isinstance(r, int) else pl.multiple_of(r * rp, rp)
            pltpu.make_async_copy(h2_hbm.at[pl.ds(src, rp), :],
                                  xbuf.at[bslot, pl.ds(row0, rp), :], sem_g.at[bslot]).start()
        row_loop(body, static_rows)

    def wait_gathers(bslot):
        pltpu.make_async_copy(h2_hbm.at[pl.ds(0, tm * rp), :], xbuf.at[bslot],
                              sem_g.at[bslot]).wait()

    def start_scatters(tslot, bslot):
        base = tslot * IDX_STRIDE + tm

        def body(r):
            dst = pl.multiple_of(idx_smem[base + r], rp)
            pltpu.make_async_copy(obuf.at[bslot, pl.ds(r * rp, rp), :],
                                  picked_hbm.at[pl.ds(dst, rp), :], sem_s.at[bslot]).start()
        row_loop(body, True)

    def wait_scatters(bslot):
        pltpu.make_async_copy(obuf.at[bslot], picked_hbm.at[pl.ds(0, tm * rp), :],
                              sem_s.at[bslot]).wait()

    @pl.when(i < nu)
    def _():
        bslot = i % 2
        tslot = i % MOE_IDX_SLOTS

        @pl.when(i == 0)
        def _():
            first = table_copy(0, 0)
            first.start()
            first.wait()
            start_gathers(0, 0, static_rows=False)

            @pl.when(nu > 1)
            def _():
                table_copy(1, 1).start()

        @pl.when(i + 2 < nu)
        def _():
            table_copy(i + 2, (i + 2) % MOE_IDX_SLOTS).start()

        @pl.when(i + 1 < nu)
        def _():
            nslot = (i + 1) % MOE_IDX_SLOTS
            table_copy(i + 1, nslot).wait()
            start_gathers(nslot, 1 - bslot)

        @pl.when(jnp.logical_or(i == 0, be_ref[i] != be_ref[jnp.maximum(i - 1, 0)]))
        def _():
            w1b[...] = w1_ref[...].astype(BF16)
            w2b[...] = w2_ref[...].astype(BF16)

        wait_gathers(bslot)
        x = _load_token_tiles(xbuf.at[bslot], tm).astype(BF16)
        x_glu = jnp.minimum(_dot(x, w1b[:, 0:D_FF]) + b1_ref[:, 0:D_FF], SWIGLU_LIMIT)
        x_lin = jnp.clip(_dot(x, w1b[:, D_FF:]) + b1_ref[:, D_FF:], -SWIGLU_LIMIT, SWIGLU_LIMIT)
        act_ref[...] = (x_glu * jax.nn.sigmoid(SWIGLU_ALPHA * x_glu) * (x_lin + 1.0)).astype(BF16)

        @pl.when(i >= 1)
        def _():
            start_scatters((i - 1) % MOE_IDX_SLOTS, 1 - bslot)

        @pl.when(i >= 2)
        def _():
            wait_scatters(bslot)

        _store_token_tiles(obuf.at[bslot], _dot(act_ref[...], w2b[...]) + b2_ref[...])

        @pl.when(i == nu - 1)
        def _():
            start_scatters(tslot, bslot)

            @pl.when(i >= 1)
            def _():
                wait_scatters(1 - bslot)
            wait_scatters(bslot)

    @pl.when(i >= nu)
    def _():
        @pl.when(i == nu)
        def _():
            obuf[0] = jnp.zeros(obuf.shape[1:], F32)
        cp = pltpu.make_async_copy(
            obuf.at[0],
            picked_hbm.at[pl.ds(pl.multiple_of(i * (tm * rp), tm * rp), tm * rp), :], sem_s.at[0])
        cp.start()
        cp.wait()


def _moe(block_e, n_used, idx_flat, h2, w1, b1, w2, b2, *, n_rows, layer):
    tm = MOE_TILE
    n_blocks = block_e.shape[0]
    d = D_MODEL
    tile_rows = tm * ROW_PIECES
    expert = lambda i, be, nu: (layer, be[i], 0, 0)
    grid_spec = pltpu.PrefetchScalarGridSpec(
        num_scalar_prefetch=2,
        grid=(n_blocks,),
        in_specs=[pl.BlockSpec(memory_space=pl.ANY),
                  pl.BlockSpec(memory_space=pl.ANY),
                  pl.BlockSpec((None, None, d, 2 * D_FF), expert),
                  pl.BlockSpec((None, None, 1, 2 * D_FF), expert),
                  pl.BlockSpec((None, None, D_FF, d), expert),
                  pl.BlockSpec((None, None, 1, d), expert)],
        out_specs=pl.BlockSpec(memory_space=pl.ANY),
        scratch_shapes=[pltpu.SMEM((MOE_IDX_SLOTS * IDX_STRIDE,), jnp.int32),
                        pltpu.VMEM((2, tile_rows, LANES), F32),
                        pltpu.VMEM((2, tile_rows, LANES), F32),
                        pltpu.VMEM((d, 2 * D_FF), BF16), pltpu.VMEM((D_FF, d), BF16),
                        pltpu.VMEM((tm, D_FF), BF16),
                        pltpu.SemaphoreType.DMA((MOE_IDX_SLOTS,)),
                        pltpu.SemaphoreType.DMA((2,)), pltpu.SemaphoreType.DMA((2,))])
    return pl.pallas_call(
        functools.partial(_moe_kernel, tm=tm),
        out_shape=jax.ShapeDtypeStruct((n_rows * ROW_PIECES, LANES), F32),
        grid_spec=grid_spec,
        compiler_params=_cparams(1, VMEM_LIMIT),
        name="moe_experts",
    )(block_e, n_used, idx_flat, h2, w1, b1.reshape(b1.shape[0], N_EXPERTS, 1, -1), w2,
      b2.reshape(b2.shape[0], N_EXPERTS, 1, -1))


def _routing_tables(top_idx, n_tok):
    tm = MOE_TILE
    n_assign = n_tok * TOP_K
    n_blocks = -(-n_assign // tm) + N_EXPERTS
    n_rows = n_blocks * tm
    i32 = jnp.int32
    flat_e = top_idx.reshape(n_assign)
    key_bits = (n_assign - 1).bit_length()
    assert N_EXPERTS << key_bits < 2 ** 31
    keys = jnp.sort(flat_e * (1 << key_bits) + jnp.arange(n_assign, dtype=i32))
    order = keys & ((1 << key_bits) - 1)
    experts = jnp.arange(N_EXPERTS, dtype=i32)
    counts = jnp.sum((flat_e[:, None] == experts[None, :]).astype(i32), axis=0)
    padded = (counts + tm - 1) // tm * tm
    pad_end = jnp.cumsum(padded)
    pad_start = pad_end - padded
    grp_start = jnp.cumsum(counts) - counts
    n_used = pad_end[-1] // tm
    blk = jnp.arange(n_blocks, dtype=i32)
    used = blk < n_used
    expert_at = lambda start: jnp.minimum(
        jnp.sum((pad_end[None, :] <= start[:, None]).astype(i32), axis=1), N_EXPERTS - 1)
    e_blk = expert_at(blk * tm)
    e_last = expert_at(((n_used - 1) * tm).reshape(1))[0]
    block_e = jnp.where(used, e_blk, e_last)
    row = blk[:, None] * tm + jnp.arange(tm, dtype=i32)[None, :]
    off = row - pad_start[e_blk][:, None]
    cnt = counts[e_blk][:, None]
    grp = grp_start[e_blk][:, None]
    valid = used[:, None] & (off < cnt)
    a = order[jnp.clip(grp + off, 0, n_assign - 1)]
    tok = a // TOP_K
    row_tok = jnp.where(valid, tok, 0)
    real_before = jnp.where(used[:, None], grp + cnt, n_assign)
    row_dst = jnp.where(valid, (a % TOP_K) * n_tok + tok, n_assign + row - real_before)
    idx = jnp.concatenate(
        [row_tok * ROW_PIECES, row_dst * ROW_PIECES,
         jnp.zeros((n_blocks, IDX_STRIDE - 2 * tm), i32)], axis=1)
    return block_e, n_used.reshape(1).astype(i32), idx.reshape(-1), n_rows


def _combine_kernel(p0_ref, p1_ref, p2_ref, p3_ref, gate_ref, x1_ref, g2_ref,
                    lng_ref, lnb_ref, o_ref):
    gate = gate_ref[...]
    n = gate.shape[0]
    y = (gate[:, 0:1] * _load_token_tiles(p0_ref, n) + gate[:, 1:2] * _load_token_tiles(p1_ref, n)
         + gate[:, 2:3] * _load_token_tiles(p2_ref, n) + gate[:, 3:4] * _load_token_tiles(p3_ref, n))
    o_ref[...] = _layer_norm(DEEPNORM_ALPHA * x1_ref[...] + g2_ref[...] * y,
                             lng_ref[...], lnb_ref[...])


def _combine(picked, gate, x1, g2, ln_g, ln_b, *, tm, rows_per_batch, batch_off, row_off, n_tok):
    m, d = x1.shape
    bpb = rows_per_batch // tm
    blk_off = row_off // tm
    k_stride = n_tok // tm
    pspec = lambda k: pl.BlockSpec((tm * ROW_PIECES, LANES),
                                   lambda i: (i + blk_off + k * k_stride, 0))
    row = lambda w: pl.BlockSpec((tm, w), lambda i: (i, 0))
    const = lambda a: pl.BlockSpec(a.shape, lambda i: (0, 0))
    return pl.pallas_call(
        _combine_kernel,
        out_shape=jax.ShapeDtypeStruct((m, d), F32),
        grid=(m // tm,),
        in_specs=[pspec(0), pspec(1), pspec(2), pspec(3), row(LANES), row(d),
                  pl.BlockSpec((None, 1, d), lambda i: (i // bpb + batch_off, 0, 0)),
                  const(ln_g), const(ln_b)],
        out_specs=row(d),
        compiler_params=_cparams(1, VMEM_LIMIT),
        name="combine_ln",
    )(picked, picked, picked, picked, gate, x1, g2, ln_g, ln_b)


def _clipped_rel_bias(rel, table):
    idx = np.clip(rel, -A_REL_CLIP, A_REL_CLIP) + A_REL_CLIP
    return jnp.moveaxis(table[idx].astype(F32), -1, 0)


def _t5_bucket(rel):
    nb = T5_BUCKETS // 2
    max_exact = nb // 2
    n = np.abs(rel)
    nf = np.maximum(n, 1).astype(np.float32)
    large = max_exact + (np.log(nf / max_exact) / math.log(T5_MAX_DISTANCE / max_exact)
                         * (nb - max_exact)).astype(np.int32)
    large = np.minimum(large, nb - 1)
    return np.where(rel > 0, nb, 0) + np.where(n < max_exact, n, large)


def _t5_rel_bias(rel, table):
    return jnp.moveaxis(table[_t5_bucket(rel)].astype(F32), -1, 0)


def kernel(x_prompt, x_sample, cache_a_k, cache_a_v, cache_b_k, cache_b_v, state_conv,
           c_prompt, c_sample, w_in, w_out, rel_bias_a, t5_bias, sinks, conv_w, conv_b,
           conv_ln_g, conv_ln_b, w_ada, b_ada, ln_g, ln_b, w_router, b_router,
           w_e_in, b_e_in, w_e_out, b_e_out):
    bp, tp, d = x_prompt.shape
    bs, ts, _ = x_sample.shape
    mp, ms = bp * tp, bs * ts
    n_tok = mp + ms
    na, nb = cache_a_k.shape[2], cache_b_k.shape[2]
    assert tp % ROW_TILE == 0 and mp % ts == 0 and ts >= CONV_W - 1 and tp >= A_REACH
    assert ms == ROW_TILE

    xp = x_prompt.reshape(mp, d)
    xs = x_sample.reshape(ms, d)
    c_all = jnp.concatenate([c_prompt, c_sample], axis=0)

    t5_p = _band_bias(lambda rel: _t5_rel_bias(rel, t5_bias), **B_BAND)
    t5_s = _rel_bias_tile(lambda rel: _t5_rel_bias(rel, t5_bias), ts, nb + ts, nb)
    conv_zero = jnp.zeros((bp, CONV_HALO, C_CH), F32)

    states_p, states_s = [], []
    for l in range(DEPTH):
        mod = _adaln(c_all, w_ada, b_ada, l)
        sh1, sc1, g1, sh2, sc2, g2 = (mod[:, j * d:(j + 1) * d].reshape(bp + bs, 1, d)
                                      for j in range(6))
        w_in_bf = w_in[l].astype(BF16)
        w_out_bf = w_out[l].astype(BF16)
        wr = jnp.pad(w_router[l], ((0, 0), (0, LANES - N_EXPERTS)))
        wr_hi = wr.astype(BF16)
        wr_lo = (wr - wr_hi.astype(F32)).astype(BF16)
        br = jnp.pad(b_router[l], (0, LANES - N_EXPERTS)).reshape(1, LANES)
        lng1, lnb1 = ln_g[l, 0].reshape(1, d), ln_b[l, 0].reshape(1, d)
        lng2, lnb2 = ln_g[l, 1].reshape(1, d), ln_b[l, 1].reshape(1, d)
        bias_a_p = _band_bias(lambda rel: _clipped_rel_bias(rel, rel_bias_a[l]), **A_BAND)
        bias_a_s = _rel_bias_tile(lambda rel: _clipped_rel_bias(rel, rel_bias_a[l]),
                                  ts, na + ts, na)
        conv_args = (conv_w[l], conv_b[l], conv_ln_g[l], conv_ln_b[l])

        qa, ka, va, qb, kb, vb, u = _inproj(xs, sc1, sh1, w_in_bf, tm=ts,
                                            rows_per_batch=ts, batch_off=bp)
        s3 = lambda a: a.reshape(bs, ts, a.shape[-1])
        ka_all = jnp.concatenate([cache_a_k[l].reshape(bs, na, A_WIDTH), s3(ka)], axis=1)
        va_all = jnp.concatenate([cache_a_v[l].reshape(bs, na, A_WIDTH), s3(va)], axis=1)
        kb_all = jnp.concatenate([cache_b_k[l].reshape(bs, nb, B_KV_WIDTH), s3(kb)], axis=1)
        vb_all = jnp.concatenate([cache_b_v[l].reshape(bs, nb, B_KV_WIDTH), s3(vb)], axis=1)
        oa = _attention(s3(qa), ka_all, va_all, bias_a_s, None,
                        cq=ts, n_prev=None, n_heads=A_HEADS, group=1)
        ob = _attention(s3(qb), kb_all, vb_all, t5_s, sinks[l],
                        cq=ts, n_prev=None, n_heads=B_HEADS, group=B_GROUP)
        prev = jnp.pad(state_conv[l], ((0, 0), (CONV_HALO - (CONV_W - 1), 0), (0, 0)))
        oc = _conv_tail(prev, s3(u), *conv_args)
        x1s, h2s, idxs, gates = _outproj(
            oa.reshape(ms, -1), ob.reshape(ms, -1), oc.reshape(ms, -1), xs, g1, sc2, sh2,
            w_out_bf, lng1, lnb1, wr_hi, wr_lo, br, tm=ts, rows_per_batch=ts, batch_off=bp)
        u_ext = jnp.concatenate([state_conv[l], s3(u)], axis=1)
        states_s.append((
            ka_all[:, -na:].reshape(bs, na, A_HEADS, HEAD_DIM),
            va_all[:, -na:].reshape(bs, na, A_HEADS, HEAD_DIM),
            kb_all[:, -nb:].reshape(bs, nb, B_KV_HEADS, HEAD_DIM),
            vb_all[:, -nb:].reshape(bs, nb, B_KV_HEADS, HEAD_DIM),
            u_ext[:, -(CONV_W - 1):]))

        qa, ka, va, qb, kb, vb, u = _inproj(xp, sc1, sh1, w_in_bf, tm=ROW_TILE,
                                            rows_per_batch=tp, batch_off=0)
        r3 = lambda a: a.reshape(bp, tp, a.shape[-1])
        oa = _band_attention(r3(qa), r3(ka), r3(va), bias_a_p, None, **A_BAND)
        ob = _band_attention(r3(qb), r3(kb), r3(vb), t5_p, sinks[l], **B_BAND)
        oc = _conv_tail(conv_zero, r3(u), *conv_args)
        x1p, h2, idxp, gatep = _outproj(
            oa.reshape(mp, -1), ob.reshape(mp, -1), oc.reshape(mp, -1), xp, g1, sc2, sh2,
            w_out_bf, lng1, lnb1, wr_hi, wr_lo, br, tm=ROW_TILE, rows_per_batch=tp, batch_off=0,
            h2_tail=h2s)
        states_p.append((
            r3(ka)[:, tp - A_REACH:].reshape(bp, A_REACH, A_HEADS, HEAD_DIM),
            r3(va)[:, tp - A_REACH:].reshape(bp, A_REACH, A_HEADS, HEAD_DIM),
            r3(kb)[:, tp - B_WINDOW:].reshape(bp, B_WINDOW, B_KV_HEADS, HEAD_DIM),
            r3(vb)[:, tp - B_WINDOW:].reshape(bp, B_WINDOW, B_KV_HEADS, HEAD_DIM),
            r3(u)[:, tp - (CONV_W - 1):]))

        top_idx = jnp.concatenate([idxp[:, :TOP_K], idxs[:, :TOP_K]], axis=0)
        block_e, n_used, idx_flat, n_rows = _routing_tables(top_idx, n_tok)
        picked = _moe(block_e, n_used, idx_flat, h2, w_e_in, b_e_in, w_e_out, b_e_out,
                      n_rows=n_rows, layer=l)
        xp = _combine(picked, gatep, x1p, g2, lng2, lnb2, tm=ROW_TILE, rows_per_batch=tp,
                      batch_off=0, row_off=0, n_tok=n_tok)
        xs = _combine(picked, gates, x1s, g2, lng2, lnb2, tm=ts, rows_per_batch=ts,
                      batch_off=bp, row_off=mp, n_tok=n_tok)

    a_k_p, a_v_p, b_k_p, b_v_p, conv_p = (jnp.stack(z) for z in zip(*states_p))
    a_k_s, a_v_s, b_k_s, b_v_s, conv_s = (jnp.stack(z) for z in zip(*states_s))
    return (xp.reshape(bp, tp, d), xs.reshape(bs, ts, d), a_k_p, a_v_p, b_k_p, b_v_p, conv_p,
            a_k_s, a_v_s, b_k_s, b_v_s, conv_s)
```

```python
import functools
import math

import jax
import jax.numpy as jnp
import numpy as np
from jax import lax
from jax.experimental import pallas as pl
from jax.experimental.pallas import tpu as pltpu

F32 = jnp.float32
BF16 = jnp.bfloat16

D_MODEL = 1024
DEPTH = 2
CHUNK = 64
HEAD_DIM = 64
ATTN_SCALE = HEAD_DIM ** -0.5
A_HEADS = 4
A_WIDTH = A_HEADS * HEAD_DIM
A_PREV_CHUNKS = 8
A_REACH = A_PREV_CHUNKS * CHUNK
A_REL_CLIP = 128
B_HEADS = 8
B_KV_HEADS = 2
B_GROUP = B_HEADS // B_KV_HEADS
B_WIDTH = B_HEADS * HEAD_DIM
B_KV_WIDTH = B_KV_HEADS * HEAD_DIM
B_WINDOW = 128
B_PREV_CHUNKS = B_WINDOW // CHUNK
T5_BUCKETS = 32
T5_MAX_DISTANCE = 128
C_CH = D_MODEL // 4
CONV_W = 31
CONV_HALO = 32
N_EXPERTS = 32
TOP_K = 4
D_FF = D_MODEL
SWIGLU_LIMIT = 7.0
SWIGLU_ALPHA = 1.702
DEEPNORM_ALPHA = (2 * DEPTH) ** 0.25
LN_EPS = 1e-5
NEG_INF = -1e30

LANES = 128
ROW_TILE = 512
MOE_TILE = 512
IDX_STRIDE = 1024
MOE_IDX_SLOTS = 4
A_BAND = dict(g_chunks=4, n_prev=A_PREV_CHUNKS, n_kv=A_HEADS, group=1)
B_BAND = dict(g_chunks=2, n_prev=B_PREV_CHUNKS, n_kv=B_KV_HEADS, group=B_GROUP)
VMEM_LIMIT = 56 * 1024 * 1024


def _cparams(n_axes=1, vmem=None):
    return pltpu.CompilerParams(dimension_semantics=("arbitrary",) * n_axes,
                                vmem_limit_bytes=vmem)


def _dot(a, b):
    return jnp.dot(a, b, preferred_element_type=F32)


def _layer_norm(z, g, b):
    mu = jnp.mean(z, axis=-1, keepdims=True)
    d = z - mu
    var = jnp.mean(d * d, axis=-1, keepdims=True)
    return d * lax.rsqrt(var + LN_EPS) * g + b


SUBLANES = 8
ROW_PIECES = D_MODEL // LANES


def _store_token_tiles(ref, val):
    n = val.shape[0]
    for c in range(ROW_PIECES):
        ref[pl.ds(c, n, stride=ROW_PIECES), :] = val[:, c * LANES:(c + 1) * LANES]


def _load_token_tiles(ref, n):
    return jnp.concatenate(
        [ref[pl.ds(c, n, stride=ROW_PIECES), :] for c in range(ROW_PIECES)], axis=1)


def _split_bf16(a):
    hi = a.astype(BF16)
    lo = (a - hi.astype(F32)).astype(BF16)
    return hi, lo


def _adaln_kernel(c_ref, w_ref, b_ref, o_ref):
    c = c_ref[...]
    a_hi, a_lo = _split_bf16(c * jax.nn.sigmoid(c))
    w_hi, w_lo = _split_bf16(w_ref[...])
    o_ref[...] = _dot(a_hi, w_hi) + _dot(a_lo, w_hi) + _dot(a_hi, w_lo) + b_ref[...]


def _adaln(c_all, w_all, b_all, layer):
    nb, d = c_all.shape
    n = w_all.shape[2]
    tn = 1536
    return pl.pallas_call(
        _adaln_kernel,
        out_shape=jax.ShapeDtypeStruct((nb, n), F32),
        grid=(n // tn,),
        in_specs=[pl.BlockSpec((nb, d), lambda j: (0, 0)),
                  pl.BlockSpec((None, d, tn), lambda j: (layer, 0, j)),
                  pl.BlockSpec((None, 1, tn), lambda j: (layer, 0, j))],
        out_specs=pl.BlockSpec((nb, tn), lambda j: (0, j)),
        compiler_params=_cparams(1, VMEM_LIMIT),
        name="adaln",
    )(c_all, w_all, b_all.reshape(b_all.shape[0], 1, n))


_QA = (0, A_WIDTH)
_KA = (A_WIDTH, 2 * A_WIDTH)
_VA = (2 * A_WIDTH, 3 * A_WIDTH)
_QB = (3 * A_WIDTH, 3 * A_WIDTH + B_WIDTH)
_KB = (_QB[1], _QB[1] + B_KV_WIDTH)
_VB = (_KB[1], _KB[1] + B_KV_WIDTH)
_GA = (_VB[1], _VB[1] + C_CH)
_GG = (_GA[1], _GA[1] + C_CH)


def _inproj_kernel(x_ref, sc_ref, sh_ref, w_ref,
                   qa_ref, ka_ref, va_ref, qb_ref, kb_ref, vb_ref, u_ref):
    h = (x_ref[...] * (1.0 + sc_ref[...]) + sh_ref[...]).astype(BF16)

    def proj(cols):
        return _dot(h, w_ref[:, cols[0]:cols[1]])

    qa_ref[...] = (proj(_QA) * ATTN_SCALE).astype(BF16)
    ka_ref[...] = proj(_KA)
    va_ref[...] = proj(_VA)
    qb_ref[...] = (proj(_QB) * ATTN_SCALE).astype(BF16)
    kb_ref[...] = proj(_KB)
    vb_ref[...] = proj(_VB)
    u_ref[...] = proj(_GA) * jax.nn.sigmoid(proj(_GG))


def _inproj(x, sc, sh, w_bf, *, tm, rows_per_batch, batch_off):
    m, d = x.shape
    bpb = rows_per_batch // tm
    mod_spec = pl.BlockSpec((None, 1, d), lambda i: (i // bpb + batch_off, 0, 0))
    widths = (A_WIDTH, A_WIDTH, A_WIDTH, B_WIDTH, B_KV_WIDTH, B_KV_WIDTH, C_CH)
    dtypes = (BF16, F32, F32, BF16, F32, F32, F32)
    return pl.pallas_call(
        _inproj_kernel,
        out_shape=[jax.ShapeDtypeStruct((m, w), dt) for w, dt in zip(widths, dtypes)],
        grid=(m // tm,),
        in_specs=[pl.BlockSpec((tm, d), lambda i: (i, 0)), mod_spec, mod_spec,
                  pl.BlockSpec(w_bf.shape, lambda i: (0, 0))],
        out_specs=[pl.BlockSpec((tm, w), lambda i: (i, 0)) for w in widths],
        compiler_params=_cparams(1, VMEM_LIMIT),
        name="inproj",
    )(x, sc, sh, w_bf)


def _attn_kernel(*refs, nc, cq, bw, pad, n_heads, group, use_sink):
    if use_sink:
        q_ref, k_ref, v_ref, bias_ref, sink_ref, o_ref, kp_ref, vp_ref = refs
    else:
        q_ref, k_ref, v_ref, bias_ref, o_ref, kp_ref, vp_ref = refs
        sink_ref = None
    tk = k_ref.shape[0]
    if pad:
        zeros = jnp.zeros((pad, kp_ref.shape[1]), BF16)
        kp_ref[0:pad, :] = zeros
        vp_ref[0:pad, :] = zeros
    kp_ref[pad:pad + tk, :] = k_ref[...].astype(BF16)
    vp_ref[pad:pad + tk, :] = v_ref[...].astype(BF16)

    def chunk(c):
        q0 = c * cq
        if not isinstance(c, int):
            q0 = pl.multiple_of(q0, cq)
        qt = q_ref[pl.ds(q0, cq), :]
        kt = kp_ref[pl.ds(q0, bw), :]
        vt = vp_ref[pl.ds(q0, bw), :]
        if pad:
            key_pos = lax.broadcasted_iota(jnp.int32, (cq, bw), 1)
            valid = key_pos >= pad - q0
        outs = []
        for h in range(n_heads):
            n = h // group
            qh = qt[:, h * HEAD_DIM:(h + 1) * HEAD_DIM]
            kh = kt[:, n * HEAD_DIM:(n + 1) * HEAD_DIM]
            vh = vt[:, n * HEAD_DIM:(n + 1) * HEAD_DIM]
            s = lax.dot_general(qh, kh, (((1,), (1,)), ((), ())),
                                preferred_element_type=F32)
            s = s + bias_ref[h]
            if pad:
                s = jnp.where(valid, s, NEG_INF)
            m = jnp.max(s, axis=-1, keepdims=True)
            if use_sink:
                m = jnp.maximum(m, sink_ref[h])
            e = jnp.exp(s - m)
            den = jnp.sum(e, axis=-1, keepdims=True)
            if use_sink:
                den = den + jnp.exp(sink_ref[h] - m)
            outs.append(_dot(e.astype(BF16), vh) / den)
        o_ref[pl.ds(q0, cq), :] = jnp.concatenate(outs, axis=-1).astype(o_ref.dtype)

    if nc == 1:
        chunk(0)
    else:
        def body(c, carry):
            chunk(c)
            return carry
        lax.fori_loop(0, nc, body, 0)


def _attention(q, k, v, bias, sinks, *, cq, n_prev, n_heads, group):
    b, t, qw = q.shape
    tk, kw = k.shape[1], k.shape[2]
    if n_prev is None:
        nc, bw, pad = 1, tk, 0
    else:
        nc, bw, pad = t // cq, (n_prev + 1) * cq, n_prev * cq
    use_sink = sinks is not None
    kern = functools.partial(_attn_kernel, nc=nc, cq=cq, bw=bw, pad=pad,
                             n_heads=n_heads, group=group, use_sink=use_sink)
    in_specs = [pl.BlockSpec((None, t, qw), lambda i: (i, 0, 0)),
                pl.BlockSpec((None, tk, kw), lambda i: (i, 0, 0)),
                pl.BlockSpec((None, tk, kw), lambda i: (i, 0, 0)),
                pl.BlockSpec(bias.shape, lambda i: (0, 0, 0))]
    args = [q, k, v, bias]
    if use_sink:
        in_specs.append(pl.BlockSpec(memory_space=pltpu.SMEM))
        args.append(sinks)
    return pl.pallas_call(
        kern,
        out_shape=jax.ShapeDtypeStruct((b, t, qw), BF16),
        grid=(b,),
        in_specs=in_specs,
        out_specs=pl.BlockSpec((None, t, qw), lambda i: (i, 0, 0)),
        scratch_shapes=[pltpu.VMEM((pad + tk, kw), BF16), pltpu.VMEM((pad + tk, kw), BF16)],
        compiler_params=_cparams(1, VMEM_LIMIT),
        name="attn_sink" if use_sink else "attn",
    )(*args)


def _band_attn_kernel(*refs, t, g_chunks, n_prev, n_kv, group, use_sink):
    if use_sink:
        q_ref, k_ref, v_ref, bias_ref, sink_ref, o_ref, qs_ref, kp_ref, vp_ref = refs
    else:
        q_ref, k_ref, v_ref, bias_ref, o_ref, qs_ref, kp_ref, vp_ref = refs
    pad = n_prev * CHUNK
    gq = g_chunks * CHUNK
    u = pad + gq
    m = group * gq
    hd = HEAD_DIM
    for n in range(n_kv):
        zeros = jnp.zeros((pad, hd), BF16)
        kp_ref[n, 0:pad, :] = zeros
        vp_ref[n, 0:pad, :] = zeros
        kp_ref[n, pad:pad + t, :] = k_ref[:, n * hd:(n + 1) * hd].astype(BF16)
        vp_ref[n, pad:pad + t, :] = v_ref[:, n * hd:(n + 1) * hd].astype(BF16)
    for h in range(n_kv * group):
        qs_ref[h] = q_ref[:, h * hd:(h + 1) * hd]
    key_pos = lax.broadcasted_iota(jnp.int32, (m, u), 1)
    row = lax.broadcasted_iota(jnp.int32, (m, 1), 0)

    def body(g, carry):
        q0 = pl.multiple_of(g * gq, gq)
        valid = key_pos >= pad - q0
        outs = []
        for n in range(n_kv):
            qstk = jnp.concatenate(
                [qs_ref[n * group + j, pl.ds(q0, gq), :] for j in range(group)], axis=0)
            kt = kp_ref[n, pl.ds(q0, u), :]
            vt = vp_ref[n, pl.ds(q0, u), :]
            s = lax.dot_general(qstk, kt, (((1,), (1,)), ((), ())),
                                preferred_element_type=F32)
            s = jnp.where(valid, s + bias_ref[n], NEG_INF)
            mx = jnp.max(s, axis=-1, keepdims=True)
            if use_sink:
                sink = jnp.full((m, 1), sink_ref[n * group], F32)
                for j in range(1, group):
                    sink = jnp.where(row >= j * gq, sink_ref[n * group + j], sink)
                mx = jnp.maximum(mx, sink)
            e = jnp.exp(s - mx)
            den = jnp.sum(e, axis=-1, keepdims=True)
            if use_sink:
                den = den + jnp.exp(sink - mx)
            o = _dot(e.astype(BF16), vt) / den
            outs.extend(o[j * gq:(j + 1) * gq, :] for j in range(group))
        o_ref[pl.ds(q0, gq), :] = jnp.concatenate(outs, axis=-1).astype(o_ref.dtype)
        return carry
    lax.fori_loop(0, t // gq, body, 0)


def _band_bias(head_bias, *, g_chunks, n_prev, n_kv, group):
    pad, gq = n_prev * CHUNK, g_chunks * CHUNK
    u = pad + gq
    r = np.arange(gq)[:, None]
    kk = np.arange(u)[None, :]
    lo = (r // CHUNK) * CHUNK
    in_band = (kk >= lo) & (kk < lo + pad + CHUNK)
    tile = jnp.where(in_band[None], _rel_bias_tile(head_bias, gq, u, pad), NEG_INF)
    return tile.reshape(n_kv, group * gq, u)


def _rel_bias_tile(head_bias, rows, cols, pad):
    n_off = rows + cols - 1
    vec = head_bias(np.arange(n_off) - pad - (rows - 1))
    h = vec.shape[0]
    padded = jnp.concatenate([vec, jnp.zeros((h, 1), vec.dtype)], axis=1)
    skew = jnp.tile(padded, (1, rows))[:, :rows * n_off].reshape(h, rows, n_off)
    return skew[:, :, rows - 1:rows - 1 + cols]


def _band_attention(q, k, v, bias, sinks, *, g_chunks, n_prev, n_kv, group):
    b, t, qw = q.shape
    kw = k.shape[2]
    pad = n_prev * CHUNK
    use_sink = sinks is not None
    kern = functools.partial(_band_attn_kernel, t=t, g_chunks=g_chunks, n_prev=n_prev,
                             n_kv=n_kv, group=group, use_sink=use_sink)
    in_specs = [pl.BlockSpec((None, t, qw), lambda i: (i, 0, 0)),
                pl.BlockSpec((None, t, kw), lambda i: (i, 0, 0)),
                pl.BlockSpec((None, t, kw), lambda i: (i, 0, 0)),
                pl.BlockSpec(bias.shape, lambda i: (0, 0, 0))]
    args = [q, k, v, bias]
    if use_sink:
        in_specs.append(pl.BlockSpec(memory_space=pltpu.SMEM))
        args.append(sinks)
    return pl.pallas_call(
        kern,
        out_shape=jax.ShapeDtypeStruct((b, t, qw), BF16),
        grid=(b,),
        in_specs=in_specs,
        out_specs=pl.BlockSpec((None, t, qw), lambda i: (i, 0, 0)),
        scratch_shapes=[pltpu.VMEM((n_kv * group, t, HEAD_DIM), BF16),
                        pltpu.VMEM((n_kv, pad + t, HEAD_DIM), BF16),
                        pltpu.VMEM((n_kv, pad + t, HEAD_DIM), BF16)],
        compiler_params=_cparams(1, VMEM_LIMIT),
        name="band_attn_sink" if use_sink else "band_attn",
    )(*args)


def _conv_kernel(prev_ref, u_ref, w_ref, cb_ref, g_ref, b_ref, o_ref, up_ref, sh_ref, *, t, tt):
    up_ref[0:CONV_HALO, :] = prev_ref[...]
    up_ref[CONV_HALO:CONV_HALO + t, :] = u_ref[...]
    lead = CONV_HALO - (CONV_W - 1)

    def tile(t0):
        win = up_ref[pl.ds(t0, tt + CONV_HALO), :]
        acc = jnp.zeros((tt, C_CH), F32)
        for phase in range(SUBLANES):
            offs = [j + lead for j in range(CONV_W) if (j + lead) % SUBLANES == phase]
            if not offs:
                continue
            span = max(offs) - phase + tt
            if phase:
                sh_ref[phase, 0:span, :] = win[phase:phase + span, :]
            for off in offs:
                a8 = off - phase
                rows = sh_ref[phase, a8:a8 + tt, :] if phase else win[a8:a8 + tt, :]
                acc = acc + rows * w_ref[off - lead:off - lead + 1, :]
        y = _layer_norm(acc + cb_ref[...], g_ref[...], b_ref[...])
        o_ref[pl.ds(t0, tt), :] = (y * jax.nn.sigmoid(y)).astype(o_ref.dtype)

    if t == tt:
        tile(0)
    else:
        def body(i, carry):
            tile(pl.multiple_of(i * tt, tt))
            return carry
        lax.fori_loop(0, t // tt, body, 0)


def _conv_tail(prev, u, conv_w, conv_b, ln_g, ln_b):
    b, t, c = u.shape
    tt = min(t, 128)
    vec = lambda a: a.reshape(1, c)
    vspec = pl.BlockSpec((1, c), lambda i: (0, 0))
    return pl.pallas_call(
        functools.partial(_conv_kernel, t=t, tt=tt),
        out_shape=jax.ShapeDtypeStruct((b, t, c), BF16),
        grid=(b,),
        in_specs=[pl.BlockSpec((None, CONV_HALO, c), lambda i: (i, 0, 0)),
                  pl.BlockSpec((None, t, c), lambda i: (i, 0, 0)),
                  pl.BlockSpec((CONV_W, c), lambda i: (0, 0)),
                  vspec, vspec, vspec],
        out_specs=pl.BlockSpec((None, t, c), lambda i: (i, 0, 0)),
        scratch_shapes=[pltpu.VMEM((CONV_HALO + t, c), F32),
                        pltpu.VMEM((SUBLANES, tt + CONV_HALO, c), F32)],
        compiler_params=_cparams(1, VMEM_LIMIT),
        name="conv_tail",
    )(prev, u, conv_w, vec(conv_b), vec(ln_g), vec(ln_b))


def _outproj_tail_kernel(*refs, n_blk):
    tail_ref, h2_ref = refs[13], refs[15]
    i = pl.program_id(0)

    @pl.when(i < n_blk)
    def _():
        _outproj_kernel(*refs[:13], *refs[14:])

    @pl.when(i == n_blk)
    def _():
        h2_ref[...] = tail_ref[...]


def _outproj_kernel(oa_ref, ob_ref, oc_ref, x_ref, g1_ref, sc2_ref, sh2_ref, wo_ref,
                    lng_ref, lnb_ref, wr_hi_ref, wr_lo_ref, br_ref,
                    x1_ref, h2_ref, idx_ref, gate_ref):
    mix = (_dot(oa_ref[...], wo_ref[0:A_WIDTH, :])
           + _dot(ob_ref[...], wo_ref[A_WIDTH:A_WIDTH + B_WIDTH, :])
           + _dot(oc_ref[...], wo_ref[A_WIDTH + B_WIDTH:, :]))
    x1 = _layer_norm(DEEPNORM_ALPHA * x_ref[...] + g1_ref[...] * mix, lng_ref[...], lnb_ref[...])
    x1_ref[...] = x1
    h2 = x1 * (1.0 + sc2_ref[...]) + sh2_ref[...]
    _store_token_tiles(h2_ref, h2)
    h_hi, h_lo = _split_bf16(h2)
    logits = (_dot(h_hi, wr_hi_ref[...]) + _dot(h_lo, wr_hi_ref[...])
              + _dot(h_hi, wr_lo_ref[...]) + br_ref[...])
    lane = lax.broadcasted_iota(jnp.int32, logits.shape, 1)
    cur = jnp.where(lane < N_EXPERTS, logits, NEG_INF)
    vals, idxs = [], []
    for _ in range(TOP_K):
        m = jnp.max(cur, axis=-1, keepdims=True)
        i = jnp.min(jnp.where(cur == m, lane, LANES), axis=-1, keepdims=True)
        vals.append(m)
        idxs.append(i)
        cur = jnp.where(lane == i, NEG_INF, cur)
    es = [jnp.exp(v - vals[0]) for v in vals]
    den = es[0] + es[1] + es[2] + es[3]
    idx_out = jnp.zeros(logits.shape, jnp.int32)
    gate_out = jnp.zeros(logits.shape, F32)
    for k in range(TOP_K):
        idx_out = jnp.where(lane == k, idxs[k], idx_out)
        gate_out = jnp.where(lane == k, es[k] / den, gate_out)
    idx_ref[...] = idx_out
    gate_ref[...] = gate_out


def _outproj(oa, ob, oc, x, g1, sc2, sh2, wo_bf, ln_g, ln_b, wr_hi, wr_lo, br,
             *, tm, rows_per_batch, batch_off, h2_tail=None):
    m, d = x.shape
    bpb = rows_per_batch // tm
    n_blk = m // tm
    last = n_blk - 1
    row = lambda w: pl.BlockSpec((tm, w), lambda i: (jnp.minimum(i, last), 0))
    mod_spec = pl.BlockSpec((None, 1, d),
                            lambda i: (jnp.minimum(i, last) // bpb + batch_off, 0, 0))
    const = lambda a: pl.BlockSpec(a.shape, lambda i: (0, 0))
    in_specs = [row(A_WIDTH), row(B_WIDTH), row(C_CH), row(d), mod_spec, mod_spec, mod_spec,
                const(wo_bf), const(ln_g), const(ln_b), const(wr_hi), const(wr_lo), const(br)]
    args = [oa, ob, oc, x, g1, sc2, sh2, wo_bf, ln_g, ln_b, wr_hi, wr_lo, br]
    if h2_tail is None:
        kern, steps, h2_rows = _outproj_kernel, n_blk, m
    else:
        assert h2_tail.shape == (tm * ROW_PIECES, LANES)
        kern = functools.partial(_outproj_tail_kernel, n_blk=n_blk)
        steps, h2_rows = n_blk + 1, m + tm
        in_specs.append(const(h2_tail))
        args.append(h2_tail)
    return pl.pallas_call(
        kern,
        out_shape=[jax.ShapeDtypeStruct((m, d), F32),
                   jax.ShapeDtypeStruct((h2_rows * ROW_PIECES, LANES), F32),
                   jax.ShapeDtypeStruct((m, LANES), jnp.int32),
                   jax.ShapeDtypeStruct((m, LANES), F32)],
        grid=(steps,),
        in_specs=in_specs,
        out_specs=[row(d), pl.BlockSpec((tm * ROW_PIECES, LANES), lambda i: (i, 0)),
                   row(LANES), row(LANES)],
        compiler_params=_cparams(1, VMEM_LIMIT),
        name="outproj_route",
    )(*args)


def _moe_kernel(be_ref, nu_ref, idx_hbm, h2_hbm, w1_ref, b1_ref, w2_ref, b2_ref,
                picked_hbm, idx_smem, xbuf, obuf, w1b, w2b, sem_idx, sem_g, sem_s, *, tm):
    i = pl.program_id(0)
    nu = nu_ref[0]
    rp = ROW_PIECES

    def table_copy(blk, slot):
        return pltpu.make_async_copy(
            idx_hbm.at[pl.ds(pl.multiple_of(blk * IDX_STRIDE, IDX_STRIDE), IDX_STRIDE)],
            idx_smem.at[pl.ds(pl.multiple_of(slot * IDX_STRIDE, IDX_STRIDE), IDX_STRIDE)],
            sem_idx.at[slot])

    def row_loop(body, static_rows):
        if static_rows:
            for r in range(tm):
                body(r)
        else:
            lax.fori_loop(0, tm, lambda r, c: (body(r), c)[1], 0)

    def start_gathers(tslot, bslot, static_rows=True):
        base = tslot * IDX_STRIDE

        def body(r):
            src = pl.multiple_of(idx_smem[base + r], rp)
            row0 = r * rp if isinstance(r, int) else pl.multiple_of(r * rp, rp)
            pltpu.make_async_copy(h2_hbm.at[pl.ds(src, rp), :],
                                  xbuf.at[bslot, pl.ds(row0, rp), :], sem_g.at[bslot]).start()
        row_loop(body, static_rows)

    def wait_gathers(bslot):
        pltpu.make_async_copy(h2_hbm.at[pl.ds(0, tm * rp), :], xbuf.at[bslot],
                              sem_g.at[bslot]).wait()

    def start_scatters(tslot, bslot):
        base = tslot * IDX_STRIDE + tm

        def body(r):
            dst = pl.multiple_of(idx_smem[base + r], rp)
            pltpu.make_async_copy(obuf.at[bslot, pl.ds(r * rp, rp), :],
                                  picked_hbm.at[pl.ds(dst, rp), :], sem_s.at[bslot]).start()
        row_loop(body, True)

    def wait_scatters(bslot):
        pltpu.make_async_copy(obuf.at[bslot], picked_hbm.at[pl.ds(0, tm * rp), :],
                              sem_s.at[bslot]).wait()

    @pl.when(i < nu)
    def _():
        bslot = i % 2
        tslot = i % MOE_IDX_SLOTS

        @pl.when(i == 0)
        def _():
            first = table_copy(0, 0)
            first.start()
            first.wait()
            start_gathers(0, 0, static_rows=False)

            @pl.when(nu > 1)
            def _():
                table_copy(1, 1).start()

        @pl.when(i + 2 < nu)
        def _():
            table_copy(i + 2, (i + 2) % MOE_IDX_SLOTS).start()

        @pl.when(i + 1 < nu)
        def _():
            nslot = (i + 1) % MOE_IDX_SLOTS
            table_copy(i + 1, nslot).wait()
            start_gathers(nslot, 1 - bslot)

        @pl.when(jnp.logical_or(i == 0, be_ref[i] != be_ref[jnp.maximum(i - 1, 0)]))
        def _():
            w1b[...] = w1_ref[...].astype(BF16)
            w2b[...] = w2_ref[...].astype(BF16)

        wait_gathers(bslot)

        @pl.when(i >= 2)
        def _():
            wait_scatters(bslot)

        x = _load_token_tiles(xbuf.at[bslot], tm).astype(BF16)
        x_glu = jnp.minimum(_dot(x, w1b[:, 0:D_FF]) + b1_ref[:, 0:D_FF], SWIGLU_LIMIT)
        x_lin = jnp.clip(_dot(x, w1b[:, D_FF:]) + b1_ref[:, D_FF:], -SWIGLU_LIMIT, SWIGLU_LIMIT)
        act = x_glu * jax.nn.sigmoid(SWIGLU_ALPHA * x_glu) * (x_lin + 1.0)
        _store_token_tiles(obuf.at[bslot], _dot(act.astype(BF16), w2b[...]) + b2_ref[...])
        start_scatters(tslot, bslot)

        @pl.when(i == nu - 1)
        def _():
            @pl.when(i >= 1)
            def _():
                wait_scatters(1 - bslot)
            wait_scatters(bslot)

    @pl.when(i >= nu)
    def _():
        @pl.when(i == nu)
        def _():
            obuf[0] = jnp.zeros(obuf.shape[1:], F32)
        cp = pltpu.make_async_copy(
            obuf.at[0],
            picked_hbm.at[pl.ds(pl.multiple_of(i * (tm * rp), tm * rp), tm * rp), :], sem_s.at[0])
        cp.start()
        cp.wait()


def _moe(block_e, n_used, idx_flat, h2, w1, b1, w2, b2, *, n_rows, layer):
    tm = MOE_TILE
    n_blocks = block_e.shape[0]
    d = D_MODEL
    tile_rows = tm * ROW_PIECES
    expert = lambda i, be, nu: (layer, be[i], 0, 0)
    grid_spec = pltpu.PrefetchScalarGridSpec(
        num_scalar_prefetch=2,
        grid=(n_blocks,),
        in_specs=[pl.BlockSpec(memory_space=pl.ANY),
                  pl.BlockSpec(memory_space=pl.ANY),
                  pl.BlockSpec((None, None, d, 2 * D_FF), expert),
                  pl.BlockSpec((None, None, 1, 2 * D_FF), expert),
                  pl.BlockSpec((None, None, D_FF, d), expert),
                  pl.BlockSpec((None, None, 1, d), expert)],
        out_specs=pl.BlockSpec(memory_space=pl.ANY),
        scratch_shapes=[pltpu.SMEM((MOE_IDX_SLOTS * IDX_STRIDE,), jnp.int32),
                        pltpu.VMEM((2, tile_rows, LANES), F32),
                        pltpu.VMEM((2, tile_rows, LANES), F32),
                        pltpu.VMEM((d, 2 * D_FF), BF16), pltpu.VMEM((D_FF, d), BF16),
                        pltpu.SemaphoreType.DMA((MOE_IDX_SLOTS,)),
                        pltpu.SemaphoreType.DMA((2,)), pltpu.SemaphoreType.DMA((2,))])
    return pl.pallas_call(
        functools.partial(_moe_kernel, tm=tm),
        out_shape=jax.ShapeDtypeStruct((n_rows * ROW_PIECES, LANES), F32),
        grid_spec=grid_spec,
        compiler_params=_cparams(1, VMEM_LIMIT),
        name="moe_experts",
    )(block_e, n_used, idx_flat, h2, w1, b1.reshape(b1.shape[0], N_EXPERTS, 1, -1), w2,
      b2.reshape(b2.shape[0], N_EXPERTS, 1, -1))


def _routing_tables(top_idx, n_tok):
    tm = MOE_TILE
    n_assign = n_tok * TOP_K
    n_blocks = -(-n_assign // tm) + N_EXPERTS
    n_rows = n_blocks * tm
    i32 = jnp.int32
    flat_e = top_idx.reshape(n_assign)
    key_bits = (n_assign - 1).bit_length()
    assert N_EXPERTS << key_bits < 2 ** 31
    keys = jnp.sort(flat_e * (1 << key_bits) + jnp.arange(n_assign, dtype=i32))
    order = keys & ((1 << key_bits) - 1)
    experts = jnp.arange(N_EXPERTS, dtype=i32)
    counts = jnp.sum((flat_e[:, None] == experts[None, :]).astype(i32), axis=0)
    padded = (counts + tm - 1) // tm * tm
    pad_end = jnp.cumsum(padded)
    pad_start = pad_end - padded
    grp_start = jnp.cumsum(counts) - counts
    n_used = pad_end[-1] // tm
    blk = jnp.arange(n_blocks, dtype=i32)
    used = blk < n_used
    expert_at = lambda start: jnp.minimum(
        jnp.sum((pad_end[None, :] <= start[:, None]).astype(i32), axis=1), N_EXPERTS - 1)
    e_blk = expert_at(blk * tm)
    e_last = expert_at(((n_used - 1) * tm).reshape(1))[0]
    block_e = jnp.where(used, e_blk, e_last)
    row = blk[:, None] * tm + jnp.arange(tm, dtype=i32)[None, :]
    off = row - pad_start[e_blk][:, None]
    cnt = counts[e_blk][:, None]
    grp = grp_start[e_blk][:, None]
    valid = used[:, None] & (off < cnt)
    a = order[jnp.clip(grp + off, 0, n_assign - 1)]
    tok = a // TOP_K
    row_tok = jnp.where(valid, tok, 0)
    real_before = jnp.where(used[:, None], grp + cnt, n_assign)
    row_dst = jnp.where(valid, (a % TOP_K) * n_tok + tok, n_assign + row - real_before)
    idx = jnp.concatenate(
        [row_tok * ROW_PIECES, row_dst * ROW_PIECES,
         jnp.zeros((n_blocks, IDX_STRIDE - 2 * tm), i32)], axis=1)
    return block_e, n_used.reshape(1).astype(i32), idx.reshape(-1), n_rows


def _combine_kernel(p0_ref, p1_ref, p2_ref, p3_ref, gate_ref, x1_ref, g2_ref,
                    lng_ref, lnb_ref, o_ref):
    gate = gate_ref[...]
    n = gate.shape[0]
    y = (gate[:, 0:1] * _load_token_tiles(p0_ref, n) + gate[:, 1:2] * _load_token_tiles(p1_ref, n)
         + gate[:, 2:3] * _load_token_tiles(p2_ref, n) + gate[:, 3:4] * _load_token_tiles(p3_ref, n))
    o_ref[...] = _layer_norm(DEEPNORM_ALPHA * x1_ref[...] + g2_ref[...] * y,
                             lng_ref[...], lnb_ref[...])


def _combine(picked, gate, x1, g2, ln_g, ln_b, *, tm, rows_per_batch, batch_off, row_off, n_tok):
    m, d = x1.shape
    bpb = rows_per_batch // tm
    blk_off = row_off // tm
    k_stride = n_tok // tm
    pspec = lambda k: pl.BlockSpec((tm * ROW_PIECES, LANES),
                                   lambda i: (i + blk_off + k * k_stride, 0))
    row = lambda w: pl.BlockSpec((tm, w), lambda i: (i, 0))
    const = lambda a: pl.BlockSpec(a.shape, lambda i: (0, 0))
    return pl.pallas_call(
        _combine_kernel,
        out_shape=jax.ShapeDtypeStruct((m, d), F32),
        grid=(m // tm,),
        in_specs=[pspec(0), pspec(1), pspec(2), pspec(3), row(LANES), row(d),
                  pl.BlockSpec((None, 1, d), lambda i: (i // bpb + batch_off, 0, 0)),
                  const(ln_g), const(ln_b)],
        out_specs=row(d),
        compiler_params=_cparams(1, VMEM_LIMIT),
        name="combine_ln",
    )(picked, picked, picked, picked, gate, x1, g2, ln_g, ln_b)


def _clipped_rel_bias(rel, table):
    idx = np.clip(rel, -A_REL_CLIP, A_REL_CLIP) + A_REL_CLIP
    return jnp.moveaxis(table[idx].astype(F32), -1, 0)


def _t5_bucket(rel):
    nb = T5_BUCKETS // 2
    max_exact = nb // 2
    n = np.abs(rel)
    nf = np.maximum(n, 1).astype(np.float32)
    large = max_exact + (np.log(nf / max_exact) / math.log(T5_MAX_DISTANCE / max_exact)
                         * (nb - max_exact)).astype(np.int32)
    large = np.minimum(large, nb - 1)
    return np.where(rel > 0, nb, 0) + np.where(n < max_exact, n, large)


def _t5_rel_bias(rel, table):
    return jnp.moveaxis(table[_t5_bucket(rel)].astype(F32), -1, 0)


def kernel(x_prompt, x_sample, cache_a_k, cache_a_v, cache_b_k, cache_b_v, state_conv,
           c_prompt, c_sample, w_in, w_out, rel_bias_a, t5_bias, sinks, conv_w, conv_b,
           conv_ln_g, conv_ln_b, w_ada, b_ada, ln_g, ln_b, w_router, b_router,
           w_e_in, b_e_in, w_e_out, b_e_out):
    bp, tp, d = x_prompt.shape
    bs, ts, _ = x_sample.shape
    mp, ms = bp * tp, bs * ts
    n_tok = mp + ms
    na, nb = cache_a_k.shape[2], cache_b_k.shape[2]
    assert tp % ROW_TILE == 0 and mp % ts == 0 and ts >= CONV_W - 1 and tp >= A_REACH
    assert ms == ROW_TILE

    xp = x_prompt.reshape(mp, d)
    xs = x_sample.reshape(ms, d)
    c_all = jnp.concatenate([c_prompt, c_sample], axis=0)

    t5_p = _band_bias(lambda rel: _t5_rel_bias(rel, t5_bias), **B_BAND)
    t5_s = _rel_bias_tile(lambda rel: _t5_rel_bias(rel, t5_bias), ts, nb + ts, nb)
    conv_zero = jnp.zeros((bp, CONV_HALO, C_CH), F32)

    states_p, states_s = [], []
    for l in range(DEPTH):
        mod = _adaln(c_all, w_ada, b_ada, l)
        sh1, sc1, g1, sh2, sc2, g2 = (mod[:, j * d:(j + 1) * d].reshape(bp + bs, 1, d)
                                      for j in range(6))
        w_in_bf = w_in[l].astype(BF16)
        w_out_bf = w_out[l].astype(BF16)
        wr = jnp.pad(w_router[l], ((0, 0), (0, LANES - N_EXPERTS)))
        wr_hi = wr.astype(BF16)
        wr_lo = (wr - wr_hi.astype(F32)).astype(BF16)
        br = jnp.pad(b_router[l], (0, LANES - N_EXPERTS)).reshape(1, LANES)
        lng1, lnb1 = ln_g[l, 0].reshape(1, d), ln_b[l, 0].reshape(1, d)
        lng2, lnb2 = ln_g[l, 1].reshape(1, d), ln_b[l, 1].reshape(1, d)
        bias_a_p = _band_bias(lambda rel: _clipped_rel_bias(rel, rel_bias_a[l]), **A_BAND)
        bias_a_s = _rel_bias_tile(lambda rel: _clipped_rel_bias(rel, rel_bias_a[l]),
                                  ts, na + ts, na)
        conv_args = (conv_w[l], conv_b[l], conv_ln_g[l], conv_ln_b[l])

        qa, ka, va, qb, kb, vb, u = _inproj(xs, sc1, sh1, w_in_bf, tm=ts,
                                            rows_per_batch=ts, batch_off=bp)
        s3 = lambda a: a.reshape(bs, ts, a.shape[-1])
        ka_all = jnp.concatenate([cache_a_k[l].reshape(bs, na, A_WIDTH), s3(ka)], axis=1)
        va_all = jnp.concatenate([cache_a_v[l].reshape(bs, na, A_WIDTH), s3(va)], axis=1)
        kb_all = jnp.concatenate([cache_b_k[l].reshape(bs, nb, B_KV_WIDTH), s3(kb)], axis=1)
        vb_all = jnp.concatenate([cache_b_v[l].reshape(bs, nb, B_KV_WIDTH), s3(vb)], axis=1)
        oa = _attention(s3(qa), ka_all, va_all, bias_a_s, None,
                        cq=ts, n_prev=None, n_heads=A_HEADS, group=1)
        ob = _attention(s3(qb), kb_all, vb_all, t5_s, sinks[l],
                        cq=ts, n_prev=None, n_heads=B_HEADS, group=B_GROUP)
        prev = jnp.pad(state_conv[l], ((0, 0), (CONV_HALO - (CONV_W - 1), 0), (0, 0)))
        oc = _conv_tail(prev, s3(u), *conv_args)
        x1s, h2s, idxs, gates = _outproj(
            oa.reshape(ms, -1), ob.reshape(ms, -1), oc.reshape(ms, -1), xs, g1, sc2, sh2,
            w_out_bf, lng1, lnb1, wr_hi, wr_lo, br, tm=ts, rows_per_batch=ts, batch_off=bp)
        u_ext = jnp.concatenate([state_conv[l], s3(u)], axis=1)
        states_s.append((
            ka_all[:, -na:].reshape(bs, na, A_HEADS, HEAD_DIM),
            va_all[:, -na:].reshape(bs, na, A_HEADS, HEAD_DIM),
            kb_all[:, -nb:].reshape(bs, nb, B_KV_HEADS, HEAD_DIM),
            vb_all[:, -nb:].reshape(bs, nb, B_KV_HEADS, HEAD_DIM),
            u_ext[:, -(CONV_W - 1):]))

        qa, ka, va, qb, kb, vb, u = _inproj(xp, sc1, sh1, w_in_bf, tm=ROW_TILE,
                                            rows_per_batch=tp, batch_off=0)
        r3 = lambda a: a.reshape(bp, tp, a.shape[-1])
        oa = _band_attention(r3(qa), r3(ka), r3(va), bias_a_p, None, **A_BAND)
        ob = _band_attention(r3(qb), r3(kb), r3(vb), t5_p, sinks[l], **B_BAND)
        oc = _conv_tail(conv_zero, r3(u), *conv_args)
        x1p, h2, idxp, gatep = _outproj(
            oa.reshape(mp, -1), ob.reshape(mp, -1), oc.reshape(mp, -1), xp, g1, sc2, sh2,
            w_out_bf, lng1, lnb1, wr_hi, wr_lo, br, tm=ROW_TILE, rows_per_batch=tp, batch_off=0,
            h2_tail=h2s)
        states_p.append((
            r3(ka)[:, tp - A_REACH:].reshape(bp, A_REACH, A_HEADS, HEAD_DIM),
            r3(va)[:, tp - A_REACH:].reshape(bp, A_REACH, A_HEADS, HEAD_DIM),
            r3(kb)[:, tp - B_WINDOW:].reshape(bp, B_WINDOW, B_KV_HEADS, HEAD_DIM),
            r3(vb)[:, tp - B_WINDOW:].reshape(bp, B_WINDOW, B_KV_HEADS, HEAD_DIM),
            r3(u)[:, tp - (CONV_W - 1):]))

        top_idx = jnp.concatenate([idxp[:, :TOP_K], idxs[:, :TOP_K]], axis=0)
        block_e, n_used, idx_flat, n_rows = _routing_tables(top_idx, n_tok)
        picked = _moe(block_e, n_used, idx_flat, h2, w_e_in, b_e_in, w_e_out, b_e_out,
                      n_rows=n_rows, layer=l)
        xp = _combine(picked, gatep, x1p, g2, lng2, lnb2, tm=ROW_TILE, rows_per_batch=tp,
                      batch_off=0, row_off=0, n_tok=n_tok)
        xs = _combine(picked, gates, x1s, g2, lng2, lnb2, tm=ts, rows_per_batch=ts,
                      batch_off=bp, row_off=mp, n_tok=n_tok)

    a_k_p, a_v_p, b_k_p, b_v_p, conv_p = (jnp.stack(z) for z in zip(*states_p))
    a_k_s, a_v_s, b_k_s, b_v_s, conv_s = (jnp.stack(z) for z in zip(*states_s))
    return (xp.reshape(bp, tp, d), xs.reshape(bs, ts, d), a_k_p, a_v_p, b_k_p, b_v_p, conv_p,
            a_k_s, a_v_s, b_k_s, b_v_s, conv_s)
```

```python
import functools
import math

import jax
import jax.numpy as jnp
import numpy as np
from jax import lax
from jax.experimental import pallas as pl
from jax.experimental.pallas import tpu as pltpu

F32 = jnp.float32
BF16 = jnp.bfloat16

D_MODEL = 1024
DEPTH = 2
CHUNK = 64
HEAD_DIM = 64
ATTN_SCALE = HEAD_DIM ** -0.5
A_HEADS = 4
A_WIDTH = A_HEADS * HEAD_DIM
A_PREV_CHUNKS = 8
A_REACH = A_PREV_CHUNKS * CHUNK
A_REL_CLIP = 128
B_HEADS = 8
B_KV_HEADS = 2
B_GROUP = B_HEADS // B_KV_HEADS
B_WIDTH = B_HEADS * HEAD_DIM
B_KV_WIDTH = B_KV_HEADS * HEAD_DIM
B_WINDOW = 128
B_PREV_CHUNKS = B_WINDOW // CHUNK
T5_BUCKETS = 32
T5_MAX_DISTANCE = 128
C_CH = D_MODEL // 4
CONV_W = 31
CONV_HALO = 32
N_EXPERTS = 32
TOP_K = 4
D_FF = D_MODEL
SWIGLU_LIMIT = 7.0
SWIGLU_ALPHA = 1.702
DEEPNORM_ALPHA = (2 * DEPTH) ** 0.25
LN_EPS = 1e-5
NEG_INF = -1e30

LANES = 128
ROW_TILE = 512
OUTPROJ_TILE = 512
SAMPLE_SEQS = 4
MOE_TILE = 256
IDX_STRIDE = 1024
MOE_IDX_SLOTS = 4
A_BAND = dict(g_chunks=4, n_prev=A_PREV_CHUNKS, n_kv=A_HEADS, group=1)
B_BAND = dict(g_chunks=2, n_prev=B_PREV_CHUNKS, n_kv=B_KV_HEADS, group=B_GROUP)
VMEM_LIMIT = 56 * 1024 * 1024


def _cparams(n_axes=1, vmem=None):
    return pltpu.CompilerParams(dimension_semantics=("arbitrary",) * n_axes,
                                vmem_limit_bytes=vmem)


def _dot(a, b):
    return jnp.dot(a, b, preferred_element_type=F32)


def _layer_norm(z, g, b):
    mu = jnp.mean(z, axis=-1, keepdims=True)
    d = z - mu
    var = jnp.mean(d * d, axis=-1, keepdims=True)
    return d * lax.rsqrt(var + LN_EPS) * g + b


SUBLANES = 8
ROW_PIECES = D_MODEL // LANES


def _store_token_tiles(ref, val):
    n = val.shape[0]
    for c in range(ROW_PIECES):
        ref[pl.ds(c, n, stride=ROW_PIECES), :] = val[:, c * LANES:(c + 1) * LANES]


def _load_token_tiles(ref, n):
    return jnp.concatenate(
        [ref[pl.ds(c, n, stride=ROW_PIECES), :] for c in range(ROW_PIECES)], axis=1)


def _split_bf16(a):
    hi = a.astype(BF16)
    lo = (a - hi.astype(F32)).astype(BF16)
    return hi, lo


def _adaln_kernel(c_ref, w_ref, b_ref, o_ref):
    c = c_ref[...]
    a_hi, a_lo = _split_bf16(c * jax.nn.sigmoid(c))
    w_hi, w_lo = _split_bf16(w_ref[...])
    o_ref[...] = _dot(a_hi, w_hi) + _dot(a_lo, w_hi) + _dot(a_hi, w_lo) + b_ref[...]


def _adaln(c_all, w_all, b_all, layer):
    nb, d = c_all.shape
    n = w_all.shape[2]
    tn = 1536
    return pl.pallas_call(
        _adaln_kernel,
        out_shape=jax.ShapeDtypeStruct((nb, n), F32),
        grid=(n // tn,),
        in_specs=[pl.BlockSpec((nb, d), lambda j: (0, 0)),
                  pl.BlockSpec((None, d, tn), lambda j: (layer, 0, j)),
                  pl.BlockSpec((None, 1, tn), lambda j: (layer, 0, j))],
        out_specs=pl.BlockSpec((nb, tn), lambda j: (0, j)),
        compiler_params=_cparams(1, VMEM_LIMIT),
        name="adaln",
    )(c_all, w_all, b_all.reshape(b_all.shape[0], 1, n))


_QA = (0, A_WIDTH)
_KA = (A_WIDTH, 2 * A_WIDTH)
_VA = (2 * A_WIDTH, 3 * A_WIDTH)
_QB = (3 * A_WIDTH, 3 * A_WIDTH + B_WIDTH)
_KB = (_QB[1], _QB[1] + B_KV_WIDTH)
_VB = (_KB[1], _KB[1] + B_KV_WIDTH)
_GA = (_VB[1], _VB[1] + C_CH)
_GG = (_GA[1], _GA[1] + C_CH)


def _inproj_kernel(x_ref, sc_ref, sh_ref, w_ref,
                   qa_ref, ka_ref, va_ref, qb_ref, kb_ref, vb_ref, u_ref):
    h = (x_ref[...] * (1.0 + sc_ref[...]) + sh_ref[...]).astype(BF16)

    def proj(cols):
        return _dot(h, w_ref[:, cols[0]:cols[1]])

    qa_ref[...] = (proj(_QA) * ATTN_SCALE).astype(BF16)
    ka_ref[...] = proj(_KA)
    va_ref[...] = proj(_VA)
    qb_ref[...] = (proj(_QB) * ATTN_SCALE).astype(BF16)
    kb_ref[...] = proj(_KB)
    vb_ref[...] = proj(_VB)
    u_ref[...] = proj(_GA) * jax.nn.sigmoid(proj(_GG))


def _mod_spec(mod, tm, d, bpb, batch_off, last=None):
    blk = (lambda i: i) if last is None else (lambda i: jnp.minimum(i, last))
    if mod.ndim == 2:
        return pl.BlockSpec((tm, d), lambda i: (blk(i), 0))
    return pl.BlockSpec((None, 1, d), lambda i: (blk(i) // bpb + batch_off, 0, 0))


def _inproj(x, sc, sh, w_bf, *, tm, rows_per_batch, batch_off):
    m, d = x.shape
    bpb = rows_per_batch // tm
    mod_spec = _mod_spec(sc, tm, d, bpb, batch_off)
    widths = (A_WIDTH, A_WIDTH, A_WIDTH, B_WIDTH, B_KV_WIDTH, B_KV_WIDTH, C_CH)
    dtypes = (BF16, F32, F32, BF16, F32, F32, F32)
    return pl.pallas_call(
        _inproj_kernel,
        out_shape=[jax.ShapeDtypeStruct((m, w), dt) for w, dt in zip(widths, dtypes)],
        grid=(m // tm,),
        in_specs=[pl.BlockSpec((tm, d), lambda i: (i, 0)), mod_spec, mod_spec,
                  pl.BlockSpec(w_bf.shape, lambda i: (0, 0))],
        out_specs=[pl.BlockSpec((tm, w), lambda i: (i, 0)) for w in widths],
        compiler_params=_cparams(1, VMEM_LIMIT),
        name="inproj",
    )(x, sc, sh, w_bf)


def _attn_kernel(*refs, nc, cq, bw, pad, n_heads, group, use_sink):
    if use_sink:
        q_ref, k_ref, v_ref, bias_ref, sink_ref, o_ref, kp_ref, vp_ref = refs
    else:
        q_ref, k_ref, v_ref, bias_ref, o_ref, kp_ref, vp_ref = refs
        sink_ref = None
    tk = k_ref.shape[0]
    if pad:
        zeros = jnp.zeros((pad, kp_ref.shape[1]), BF16)
        kp_ref[0:pad, :] = zeros
        vp_ref[0:pad, :] = zeros
    kp_ref[pad:pad + tk, :] = k_ref[...].astype(BF16)
    vp_ref[pad:pad + tk, :] = v_ref[...].astype(BF16)

    def chunk(c):
        q0 = c * cq
        if not isinstance(c, int):
            q0 = pl.multiple_of(q0, cq)
        qt = q_ref[pl.ds(q0, cq), :]
        kt = kp_ref[pl.ds(q0, bw), :]
        vt = vp_ref[pl.ds(q0, bw), :]
        if pad:
            key_pos = lax.broadcasted_iota(jnp.int32, (cq, bw), 1)
            valid = key_pos >= pad - q0
        outs = []
        for h in range(n_heads):
            n = h // group
            qh = qt[:, h * HEAD_DIM:(h + 1) * HEAD_DIM]
            kh = kt[:, n * HEAD_DIM:(n + 1) * HEAD_DIM]
            vh = vt[:, n * HEAD_DIM:(n + 1) * HEAD_DIM]
            s = lax.dot_general(qh, kh, (((1,), (1,)), ((), ())),
                                preferred_element_type=F32)
            s = s + bias_ref[h]
            if pad:
                s = jnp.where(valid, s, NEG_INF)
            m = jnp.max(s, axis=-1, keepdims=True)
            if use_sink:
                m = jnp.maximum(m, sink_ref[h])
            e = jnp.exp(s - m)
            den = jnp.sum(e, axis=-1, keepdims=True)
            if use_sink:
                den = den + jnp.exp(sink_ref[h] - m)
            outs.append(_dot(e.astype(BF16), vh) / den)
        o_ref[pl.ds(q0, cq), :] = jnp.concatenate(outs, axis=-1).astype(o_ref.dtype)

    if nc == 1:
        chunk(0)
    else:
        def body(c, carry):
            chunk(c)
            return carry
        lax.fori_loop(0, nc, body, 0)


def _attn_multi_kernel(*refs, n_seq, use_sink, **kw):
    shared = (3, 4) if use_sink else (3,)
    for b in range(n_seq):
        _attn_kernel(*[r if j in shared else r.at[b] for j, r in enumerate(refs)],
                     use_sink=use_sink, **kw)


def _attention(q, k, v, bias, sinks, *, cq, n_prev, n_heads, group, n_seq=1):
    b, t, qw = q.shape
    tk, kw = k.shape[1], k.shape[2]
    if n_prev is None:
        nc, bw, pad = 1, tk, 0
    else:
        nc, bw, pad = t // cq, (n_prev + 1) * cq, n_prev * cq
    use_sink = sinks is not None
    kern = functools.partial(_attn_multi_kernel, n_seq=n_seq, nc=nc, cq=cq, bw=bw, pad=pad,
                             n_heads=n_heads, group=group, use_sink=use_sink)
    in_specs = [pl.BlockSpec((n_seq, t, qw), lambda i: (i, 0, 0)),
                pl.BlockSpec((n_seq, tk, kw), lambda i: (i, 0, 0)),
                pl.BlockSpec((n_seq, tk, kw), lambda i: (i, 0, 0)),
                pl.BlockSpec(bias.shape, lambda i: (0, 0, 0))]
    args = [q, k, v, bias]
    if use_sink:
        in_specs.append(pl.BlockSpec(memory_space=pltpu.SMEM))
        args.append(sinks)
    return pl.pallas_call(
        kern,
        out_shape=jax.ShapeDtypeStruct((b, t, qw), BF16),
        grid=(b // n_seq,),
        in_specs=in_specs,
        out_specs=pl.BlockSpec((n_seq, t, qw), lambda i: (i, 0, 0)),
        scratch_shapes=[pltpu.VMEM((n_seq, pad + tk, kw), BF16),
                        pltpu.VMEM((n_seq, pad + tk, kw), BF16)],
        compiler_params=_cparams(1, VMEM_LIMIT),
        name="attn_sink" if use_sink else "attn",
    )(*args)


def _band_attn_kernel(*refs, t, g_chunks, n_prev, n_kv, group, use_sink):
    if use_sink:
        q_ref, k_ref, v_ref, bias_ref, sink_ref, o_ref, qs_ref, kp_ref, vp_ref = refs
    else:
        q_ref, k_ref, v_ref, bias_ref, o_ref, qs_ref, kp_ref, vp_ref = refs
    pad = n_prev * CHUNK
    gq = g_chunks * CHUNK
    u = pad + gq
    m = group * gq
    hd = HEAD_DIM
    for n in range(n_kv):
        zeros = jnp.zeros((pad, hd), BF16)
        kp_ref[n, 0:pad, :] = zeros
        vp_ref[n, 0:pad, :] = zeros
        kp_ref[n, pad:pad + t, :] = k_ref[:, n * hd:(n + 1) * hd].astype(BF16)
        vp_ref[n, pad:pad + t, :] = v_ref[:, n * hd:(n + 1) * hd].astype(BF16)
    for h in range(n_kv * group):
        qs_ref[h] = q_ref[:, h * hd:(h + 1) * hd]
    key_pos = lax.broadcasted_iota(jnp.int32, (m, u), 1)
    row = lax.broadcasted_iota(jnp.int32, (m, 1), 0)

    def body(g, carry):
        q0 = pl.multiple_of(g * gq, gq)
        valid = key_pos >= pad - q0
        outs = []
        for n in range(n_kv):
            qstk = jnp.concatenate(
                [qs_ref[n * group + j, pl.ds(q0, gq), :] for j in range(group)], axis=0)
            kt = kp_ref[n, pl.ds(q0, u), :]
            vt = vp_ref[n, pl.ds(q0, u), :]
            s = lax.dot_general(qstk, kt, (((1,), (1,)), ((), ())),
                                preferred_element_type=F32)
            s = jnp.where(valid, s + bias_ref[n], NEG_INF)
            mx = jnp.max(s, axis=-1, keepdims=True)
            if use_sink:
                sink = jnp.full((m, 1), sink_ref[n * group], F32)
                for j in range(1, group):
                    sink = jnp.where(row >= j * gq, sink_ref[n * group + j], sink)
                mx = jnp.maximum(mx, sink)
            e = jnp.exp(s - mx)
            den = jnp.sum(e, axis=-1, keepdims=True)
            if use_sink:
                den = den + jnp.exp(sink - mx)
            o = _dot(e.astype(BF16), vt) / den
            outs.extend(o[j * gq:(j + 1) * gq, :] for j in range(group))
        o_ref[pl.ds(q0, gq), :] = jnp.concatenate(outs, axis=-1).astype(o_ref.dtype)
        return carry
    lax.fori_loop(0, t // gq, body, 0)


def _band_bias(head_bias, *, g_chunks, n_prev, n_kv, group):
    pad, gq = n_prev * CHUNK, g_chunks * CHUNK
    u = pad + gq
    r = np.arange(gq)[:, None]
    kk = np.arange(u)[None, :]
    lo = (r // CHUNK) * CHUNK
    in_band = (kk >= lo) & (kk < lo + pad + CHUNK)
    tile = jnp.where(in_band[None], _rel_bias_tile(head_bias, gq, u, pad), NEG_INF)
    return tile.reshape(n_kv, group * gq, u)


def _rel_bias_tile(head_bias, rows, cols, pad):
    n_off = rows + cols - 1
    vec = head_bias(np.arange(n_off) - pad - (rows - 1))
    h = vec.shape[0]
    padded = jnp.concatenate([vec, jnp.zeros((h, 1), vec.dtype)], axis=1)
    skew = jnp.tile(padded, (1, rows))[:, :rows * n_off].reshape(h, rows, n_off)
    return skew[:, :, rows - 1:rows - 1 + cols]


def _band_attention(q, k, v, bias, sinks, *, g_chunks, n_prev, n_kv, group):
    b, t, qw = q.shape
    kw = k.shape[2]
    pad = n_prev * CHUNK
    use_sink = sinks is not None
    kern = functools.partial(_band_attn_kernel, t=t, g_chunks=g_chunks, n_prev=n_prev,
                             n_kv=n_kv, group=group, use_sink=use_sink)
    in_specs = [pl.BlockSpec((None, t, qw), lambda i: (i, 0, 0)),
                pl.BlockSpec((None, t, kw), lambda i: (i, 0, 0)),
                pl.BlockSpec((None, t, kw), lambda i: (i, 0, 0)),
                pl.BlockSpec(bias.shape, lambda i: (0, 0, 0))]
    args = [q, k, v, bias]
    if use_sink:
        in_specs.append(pl.BlockSpec(memory_space=pltpu.SMEM))
        args.append(sinks)
    return pl.pallas_call(
        kern,
        out_shape=jax.ShapeDtypeStruct((b, t, qw), BF16),
        grid=(b,),
        in_specs=in_specs,
        out_specs=pl.BlockSpec((None, t, qw), lambda i: (i, 0, 0)),
        scratch_shapes=[pltpu.VMEM((n_kv * group, t, HEAD_DIM), BF16),
                        pltpu.VMEM((n_kv, pad + t, HEAD_DIM), BF16),
                        pltpu.VMEM((n_kv, pad + t, HEAD_DIM), BF16)],
        compiler_params=_cparams(1, VMEM_LIMIT),
        name="band_attn_sink" if use_sink else "band_attn",
    )(*args)


def _conv_kernel(prev_ref, u_ref, w_ref, cb_ref, g_ref, b_ref, o_ref, up_ref, sh_ref, *, t, tt):
    up_ref[0:CONV_HALO, :] = prev_ref[...]
    up_ref[CONV_HALO:CONV_HALO + t, :] = u_ref[...]
    lead = CONV_HALO - (CONV_W - 1)

    def tile(t0):
        win = up_ref[pl.ds(t0, tt + CONV_HALO), :]
        acc = jnp.zeros((tt, C_CH), F32)
        for phase in range(SUBLANES):
            offs = [j + lead for j in range(CONV_W) if (j + lead) % SUBLANES == phase]
            if not offs:
                continue
            span = max(offs) - phase + tt
            if phase:
                sh_ref[phase, 0:span, :] = win[phase:phase + span, :]
            for off in offs:
                a8 = off - phase
                rows = sh_ref[phase, a8:a8 + tt, :] if phase else win[a8:a8 + tt, :]
                acc = acc + rows * w_ref[off - lead:off - lead + 1, :]
        y = _layer_norm(acc + cb_ref[...], g_ref[...], b_ref[...])
        o_ref[pl.ds(t0, tt), :] = (y * jax.nn.sigmoid(y)).astype(o_ref.dtype)

    if t == tt:
        tile(0)
    else:
        def body(i, carry):
            tile(pl.multiple_of(i * tt, tt))
            return carry
        lax.fori_loop(0, t // tt, body, 0)


def _conv_tail(prev, u, conv_w, conv_b, ln_g, ln_b):
    b, t, c = u.shape
    tt = min(t, 128)
    vec = lambda a: a.reshape(1, c)
    vspec = pl.BlockSpec((1, c), lambda i: (0, 0))
    return pl.pallas_call(
        functools.partial(_conv_kernel, t=t, tt=tt),
        out_shape=jax.ShapeDtypeStruct((b, t, c), BF16),
        grid=(b,),
        in_specs=[pl.BlockSpec((None, CONV_HALO, c), lambda i: (i, 0, 0)),
                  pl.BlockSpec((None, t, c), lambda i: (i, 0, 0)),
                  pl.BlockSpec((CONV_W, c), lambda i: (0, 0)),
                  vspec, vspec, vspec],
        out_specs=pl.BlockSpec((None, t, c), lambda i: (i, 0, 0)),
        scratch_shapes=[pltpu.VMEM((CONV_HALO + t, c), F32),
                        pltpu.VMEM((SUBLANES, tt + CONV_HALO, c), F32)],
        compiler_params=_cparams(1, VMEM_LIMIT),
        name="conv_tail",
    )(prev, u, conv_w, vec(conv_b), vec(ln_g), vec(ln_b))


def _outproj_tail_kernel(*refs, n_blk):
    tail_ref, h2_ref = refs[13], refs[15]
    i = pl.program_id(0)

    @pl.when(i < n_blk)
    def _():
        _outproj_kernel(*refs[:13], *refs[14:])

    @pl.when(i >= n_blk)
    def _():
        h2_ref[...] = tail_ref[...]


def _outproj_kernel(oa_ref, ob_ref, oc_ref, x_ref, g1_ref, sc2_ref, sh2_ref, wo_ref,
                    lng_ref, lnb_ref, wr_hi_ref, wr_lo_ref, br_ref,
                    x1_ref, h2_ref, idx_ref, gate_ref):
    mix = (_dot(oa_ref[...], wo_ref[0:A_WIDTH, :])
           + _dot(ob_ref[...], wo_ref[A_WIDTH:A_WIDTH + B_WIDTH, :])
           + _dot(oc_ref[...], wo_ref[A_WIDTH + B_WIDTH:, :]))
    x1 = _layer_norm(DEEPNORM_ALPHA * x_ref[...] + g1_ref[...] * mix, lng_ref[...], lnb_ref[...])
    x1_ref[...] = x1
    h2 = x1 * (1.0 + sc2_ref[...]) + sh2_ref[...]
    _store_token_tiles(h2_ref, h2)
    h_hi, h_lo = _split_bf16(h2)
    logits = (_dot(h_hi, wr_hi_ref[...]) + _dot(h_lo, wr_hi_ref[...])
              + _dot(h_hi, wr_lo_ref[...]) + br_ref[...])
    lane = lax.broadcasted_iota(jnp.int32, logits.shape, 1)
    cur = jnp.where(lane < N_EXPERTS, logits, NEG_INF)
    vals, idxs = [], []
    for _ in range(TOP_K):
        m = jnp.max(cur, axis=-1, keepdims=True)
        i = jnp.min(jnp.where(cur == m, lane, LANES), axis=-1, keepdims=True)
        vals.append(m)
        idxs.append(i)
        cur = jnp.where(lane == i, NEG_INF, cur)
    es = [jnp.exp(v - vals[0]) for v in vals]
    den = es[0] + es[1] + es[2] + es[3]
    idx_out = jnp.zeros(logits.shape, jnp.int32)
    gate_out = jnp.zeros(logits.shape, F32)
    for k in range(TOP_K):
        idx_out = jnp.where(lane == k, idxs[k], idx_out)
        gate_out = jnp.where(lane == k, es[k] / den, gate_out)
    idx_ref[...] = idx_out
    gate_ref[...] = gate_out


def _outproj(oa, ob, oc, x, g1, sc2, sh2, wo_bf, ln_g, ln_b, wr_hi, wr_lo, br,
             *, tm, rows_per_batch, batch_off, h2_tail=None):
    m, d = x.shape
    bpb = rows_per_batch // tm
    n_blk = m // tm
    last = n_blk - 1
    row = lambda w: pl.BlockSpec((tm, w), lambda i: (jnp.minimum(i, last), 0))
    mod_spec = _mod_spec(g1, tm, d, bpb, batch_off, last)
    const = lambda a: pl.BlockSpec(a.shape, lambda i: (0, 0))
    in_specs = [row(A_WIDTH), row(B_WIDTH), row(C_CH), row(d), mod_spec, mod_spec, mod_spec,
                const(wo_bf), const(ln_g), const(ln_b), const(wr_hi), const(wr_lo), const(br)]
    args = [oa, ob, oc, x, g1, sc2, sh2, wo_bf, ln_g, ln_b, wr_hi, wr_lo, br]
    if h2_tail is None:
        kern, steps, h2_rows = _outproj_kernel, n_blk, m
    else:
        tail_tokens = h2_tail.shape[0] // ROW_PIECES
        assert h2_tail.shape[1] == LANES and tail_tokens % tm == 0
        kern = functools.partial(_outproj_tail_kernel, n_blk=n_blk)
        steps, h2_rows = n_blk + tail_tokens // tm, m + tail_tokens
        in_specs.append(pl.BlockSpec((tm * ROW_PIECES, LANES),
                                     lambda i: (jnp.maximum(i - n_blk, 0), 0)))
        args.append(h2_tail)
    return pl.pallas_call(
        kern,
        out_shape=[jax.ShapeDtypeStruct((m, d), F32),
                   jax.ShapeDtypeStruct((h2_rows * ROW_PIECES, LANES), F32),
                   jax.ShapeDtypeStruct((m, LANES), jnp.int32),
                   jax.ShapeDtypeStruct((m, LANES), F32)],
        grid=(steps,),
        in_specs=in_specs,
        out_specs=[row(d), pl.BlockSpec((tm * ROW_PIECES, LANES), lambda i: (i, 0)),
                   row(LANES), row(LANES)],
        compiler_params=_cparams(1, VMEM_LIMIT),
        name="outproj_route",
    )(*args)


def _moe_kernel(be_ref, nu_ref, idx_hbm, h2_hbm, w1_ref, b1_ref, w2_ref, b2_ref,
                picked_hbm, idx_smem, xbuf, obuf, w1b, w2b, sem_idx, sem_g, sem_s, *, tm):
    i = pl.program_id(0)
    nu = nu_ref[0]
    rp = ROW_PIECES

    def table_copy(blk, slot):
        return pltpu.make_async_copy(
            idx_hbm.at[pl.ds(pl.multiple_of(blk * IDX_STRIDE, IDX_STRIDE), IDX_STRIDE)],
            idx_smem.at[pl.ds(pl.multiple_of(slot * IDX_STRIDE, IDX_STRIDE), IDX_STRIDE)],
            sem_idx.at[slot])

    def row_loop(body, static_rows):
        if static_rows:
            for r in range(tm):
                body(r)
        else:
            lax.fori_loop(0, tm, lambda r, c: (body(r), c)[1], 0)

    def start_gathers(tslot, bslot, static_rows=True):
        base = tslot * IDX_STRIDE

        def body(r):
            src = pl.multiple_of(idx_smem[base + r], rp)
            row0 = r * rp if isinstance(r, int) else pl.multiple_of(r * rp, rp)
            pltpu.make_async_copy(h2_hbm.at[pl.ds(src, rp), :],
                                  xbuf.at[bslot, pl.ds(row0, rp), :], sem_g.at[bslot]).start()
        row_loop(body, static_rows)

    def wait_gathers(bslot):
        pltpu.make_async_copy(h2_hbm.at[pl.ds(0, tm * rp), :], xbuf.at[bslot],
                              sem_g.at[bslot]).wait()

    def start_scatters(tslot, bslot):
        base = tslot * IDX_STRIDE + tm

        def body(r):
            dst = pl.multiple_of(idx_smem[base + r], rp)
            pltpu.make_async_copy(obuf.at[bslot, pl.ds(r * rp, rp), :],
                                  picked_hbm.at[pl.ds(dst, rp), :], sem_s.at[bslot]).start()
        row_loop(body, True)

    def wait_scatters(bslot):
        pltpu.make_async_copy(obuf.at[bslot], picked_hbm.at[pl.ds(0, tm * rp), :],
                              sem_s.at[bslot]).wait()

    @pl.when(i < nu)
    def _():
        bslot = i % 2
        tslot = i % MOE_IDX_SLOTS

        @pl.when(i == 0)
        def _():
            first = table_copy(0, 0)
            first.start()
            first.wait()
            start_gathers(0, 0, static_rows=False)

            @pl.when(nu > 1)
            def _():
                table_copy(1, 1).start()

        @pl.when(i + 2 < nu)
        def _():
            table_copy(i + 2, (i + 2) % MOE_IDX_SLOTS).start()

        @pl.when(i + 1 < nu)
        def _():
            nslot = (i + 1) % MOE_IDX_SLOTS
            table_copy(i + 1, nslot).wait()
            start_gathers(nslot, 1 - bslot)

        @pl.when(jnp.logical_or(i == 0, be_ref[i] != be_ref[jnp.maximum(i - 1, 0)]))
        def _():
            w1b[...] = w1_ref[...].astype(BF16)
            w2b[...] = w2_ref[...].astype(BF16)

        wait_gathers(bslot)

        @pl.when(i >= 2)
        def _():
            wait_scatters(bslot)

        x = _load_token_tiles(xbuf.at[bslot], tm).astype(BF16)
        x_glu = jnp.minimum(_dot(x, w1b[:, 0:D_FF]) + b1_ref[:, 0:D_FF], SWIGLU_LIMIT)
        x_lin = jnp.clip(_dot(x, w1b[:, D_FF:]) + b1_ref[:, D_FF:], -SWIGLU_LIMIT, SWIGLU_LIMIT)
        act = x_glu * jax.nn.sigmoid(SWIGLU_ALPHA * x_glu) * (x_lin + 1.0)
        _store_token_tiles(obuf.at[bslot], _dot(act.astype(BF16), w2b[...]) + b2_ref[...])
        start_scatters(tslot, bslot)

        @pl.when(i == nu - 1)
        def _():
            @pl.when(i >= 1)
            def _():
                wait_scatters(1 - bslot)
            wait_scatters(bslot)

    @pl.when(i >= nu)
    def _():
        @pl.when(i == nu)
        def _():
            obuf[0] = jnp.zeros(obuf.shape[1:], F32)
        cp = pltpu.make_async_copy(
            obuf.at[0],
            picked_hbm.at[pl.ds(pl.multiple_of(i * (tm * rp), tm * rp), tm * rp), :], sem_s.at[0])
        cp.start()
        cp.wait()


def _moe(block_e, n_used, idx_flat, h2, w1, b1, w2, b2, *, n_rows, layer):
    tm = MOE_TILE
    n_blocks = block_e.shape[0]
    d = D_MODEL
    tile_rows = tm * ROW_PIECES
    expert = lambda i, be, nu: (layer, be[i], 0, 0)
    grid_spec = pltpu.PrefetchScalarGridSpec(
        num_scalar_prefetch=2,
        grid=(n_blocks,),
        in_specs=[pl.BlockSpec(memory_space=pl.ANY),
                  pl.BlockSpec(memory_space=pl.ANY),
                  pl.BlockSpec((None, None, d, 2 * D_FF), expert),
                  pl.BlockSpec((None, None, 1, 2 * D_FF), expert),
                  pl.BlockSpec((None, None, D_FF, d), expert),
                  pl.BlockSpec((None, None, 1, d), expert)],
        out_specs=pl.BlockSpec(memory_space=pl.ANY),
        scratch_shapes=[pltpu.SMEM((MOE_IDX_SLOTS * IDX_STRIDE,), jnp.int32),
                        pltpu.VMEM((2, tile_rows, LANES), F32),
                        pltpu.VMEM((2, tile_rows, LANES), F32),
                        pltpu.VMEM((d, 2 * D_FF), BF16), pltpu.VMEM((D_FF, d), BF16),
                        pltpu.SemaphoreType.DMA((MOE_IDX_SLOTS,)),
                        pltpu.SemaphoreType.DMA((2,)), pltpu.SemaphoreType.DMA((2,))])
    return pl.pallas_call(
        functools.partial(_moe_kernel, tm=tm),
        out_shape=jax.ShapeDtypeStruct((n_rows * ROW_PIECES, LANES), F32),
        grid_spec=grid_spec,
        compiler_params=_cparams(1, VMEM_LIMIT),
        name="moe_experts",
    )(block_e, n_used, idx_flat, h2, w1, b1.reshape(b1.shape[0], N_EXPERTS, 1, -1), w2,
      b2.reshape(b2.shape[0], N_EXPERTS, 1, -1))


def _routing_tables(top_idx, n_tok):
    tm = MOE_TILE
    n_assign = n_tok * TOP_K
    n_blocks = -(-n_assign // tm) + N_EXPERTS
    n_rows = n_blocks * tm
    i32 = jnp.int32
    flat_e = top_idx.reshape(n_assign)
    key_bits = (n_assign - 1).bit_length()
    assert N_EXPERTS << key_bits < 2 ** 31
    keys = jnp.sort(flat_e * (1 << key_bits) + jnp.arange(n_assign, dtype=i32))
    order = keys & ((1 << key_bits) - 1)
    experts = jnp.arange(N_EXPERTS, dtype=i32)
    counts = jnp.sum((flat_e[:, None] == experts[None, :]).astype(i32), axis=0)
    padded = (counts + tm - 1) // tm * tm
    pad_end = jnp.cumsum(padded)
    pad_start = pad_end - padded
    grp_start = jnp.cumsum(counts) - counts
    n_used = pad_end[-1] // tm
    blk = jnp.arange(n_blocks, dtype=i32)
    used = blk < n_used
    expert_at = lambda start: jnp.minimum(
        jnp.sum((pad_end[None, :] <= start[:, None]).astype(i32), axis=1), N_EXPERTS - 1)
    e_blk = expert_at(blk * tm)
    e_last = expert_at(((n_used - 1) * tm).reshape(1))[0]
    block_e = jnp.where(used, e_blk, e_last)
    row = blk[:, None] * tm + jnp.arange(tm, dtype=i32)[None, :]
    off = row - pad_start[e_blk][:, None]
    cnt = counts[e_blk][:, None]
    grp = grp_start[e_blk][:, None]
    valid = used[:, None] & (off < cnt)
    a = order[jnp.clip(grp + off, 0, n_assign - 1)]
    tok = a // TOP_K
    row_tok = jnp.where(valid, tok, 0)
    real_before = jnp.where(used[:, None], grp + cnt, n_assign)
    row_dst = jnp.where(valid, (a % TOP_K) * n_tok + tok, n_assign + row - real_before)
    idx = jnp.concatenate(
        [row_tok * ROW_PIECES, row_dst * ROW_PIECES,
         jnp.zeros((n_blocks, IDX_STRIDE - 2 * tm), i32)], axis=1)
    return block_e, n_used.reshape(1).astype(i32), idx.reshape(-1), n_rows


def _combine_kernel(p0_ref, p1_ref, p2_ref, p3_ref, gate_ref, x1_ref, g2_ref,
                    lng_ref, lnb_ref, o_ref):
    gate = gate_ref[...]
    n = gate.shape[0]
    y = (gate[:, 0:1] * _load_token_tiles(p0_ref, n) + gate[:, 1:2] * _load_token_tiles(p1_ref, n)
         + gate[:, 2:3] * _load_token_tiles(p2_ref, n) + gate[:, 3:4] * _load_token_tiles(p3_ref, n))
    o_ref[...] = _layer_norm(DEEPNORM_ALPHA * x1_ref[...] + g2_ref[...] * y,
                             lng_ref[...], lnb_ref[...])


def _combine(picked, gate, x1, g2, ln_g, ln_b, *, tm, rows_per_batch, batch_off, row_off, n_tok):
    m, d = x1.shape
    bpb = rows_per_batch // tm
    blk_off = row_off // tm
    k_stride = n_tok // tm
    pspec = lambda k: pl.BlockSpec((tm * ROW_PIECES, LANES),
                                   lambda i: (i + blk_off + k * k_stride, 0))
    row = lambda w: pl.BlockSpec((tm, w), lambda i: (i, 0))
    const = lambda a: pl.BlockSpec(a.shape, lambda i: (0, 0))
    return pl.pallas_call(
        _combine_kernel,
        out_shape=jax.ShapeDtypeStruct((m, d), F32),
        grid=(m // tm,),
        in_specs=[pspec(0), pspec(1), pspec(2), pspec(3), row(LANES), row(d),
                  _mod_spec(g2, tm, d, bpb, batch_off), const(ln_g), const(ln_b)],
        out_specs=row(d),
        compiler_params=_cparams(1, VMEM_LIMIT),
        name="combine_ln",
    )(picked, picked, picked, picked, gate, x1, g2, ln_g, ln_b)


def _clipped_rel_bias(rel, table):
    idx = np.clip(rel, -A_REL_CLIP, A_REL_CLIP) + A_REL_CLIP
    return jnp.moveaxis(table[idx].astype(F32), -1, 0)


def _t5_bucket(rel):
    nb = T5_BUCKETS // 2
    max_exact = nb // 2
    n = np.abs(rel)
    nf = np.maximum(n, 1).astype(np.float32)
    large = max_exact + (np.log(nf / max_exact) / math.log(T5_MAX_DISTANCE / max_exact)
                         * (nb - max_exact)).astype(np.int32)
    large = np.minimum(large, nb - 1)
    return np.where(rel > 0, nb, 0) + np.where(n < max_exact, n, large)


def _t5_rel_bias(rel, table):
    return jnp.moveaxis(table[_t5_bucket(rel)].astype(F32), -1, 0)


def kernel(x_prompt, x_sample, cache_a_k, cache_a_v, cache_b_k, cache_b_v, state_conv,
           c_prompt, c_sample, w_in, w_out, rel_bias_a, t5_bias, sinks, conv_w, conv_b,
           conv_ln_g, conv_ln_b, w_ada, b_ada, ln_g, ln_b, w_router, b_router,
           w_e_in, b_e_in, w_e_out, b_e_out):
    bp, tp, d = x_prompt.shape
    bs, ts, _ = x_sample.shape
    mp, ms = bp * tp, bs * ts
    n_tok = mp + ms
    na, nb = cache_a_k.shape[2], cache_b_k.shape[2]
    assert tp % ROW_TILE == 0 and mp % ts == 0 and ts >= CONV_W - 1 and tp >= A_REACH
    assert ms % OUTPROJ_TILE == 0 and tp % OUTPROJ_TILE == 0
    assert mp % ms == 0 and bs % SAMPLE_SEQS == 0

    xp = x_prompt.reshape(mp, d)
    xs = x_sample.reshape(ms, d)
    c_all = jnp.concatenate([c_prompt, c_sample], axis=0)

    t5_p = _band_bias(lambda rel: _t5_rel_bias(rel, t5_bias), **B_BAND)
    t5_s = _rel_bias_tile(lambda rel: _t5_rel_bias(rel, t5_bias), ts, nb + ts, nb)
    conv_zero = jnp.zeros((bp, CONV_HALO, C_CH), F32)

    states_p, states_s = [], []
    for l in range(DEPTH):
        mod = _adaln(c_all, w_ada, b_ada, l)
        sh1, sc1, g1, sh2, sc2, g2 = (mod[:, j * d:(j + 1) * d].reshape(bp + bs, 1, d)
                                      for j in range(6))
        w_in_bf = w_in[l].astype(BF16)
        w_out_bf = w_out[l].astype(BF16)
        wr = jnp.pad(w_router[l], ((0, 0), (0, LANES - N_EXPERTS)))
        wr_hi = wr.astype(BF16)
        wr_lo = (wr - wr_hi.astype(F32)).astype(BF16)
        br = jnp.pad(b_router[l], (0, LANES - N_EXPERTS)).reshape(1, LANES)
        lng1, lnb1 = ln_g[l, 0].reshape(1, d), ln_b[l, 0].reshape(1, d)
        lng2, lnb2 = ln_g[l, 1].reshape(1, d), ln_b[l, 1].reshape(1, d)
        bias_a_p = _band_bias(lambda rel: _clipped_rel_bias(rel, rel_bias_a[l]), **A_BAND)
        bias_a_s = _rel_bias_tile(lambda rel: _clipped_rel_bias(rel, rel_bias_a[l]),
                                  ts, na + ts, na)
        conv_args = (conv_w[l], conv_b[l], conv_ln_g[l], conv_ln_b[l])

        per_row = lambda a: jnp.broadcast_to(a[bp:], (bs, ts, d)).reshape(ms, d)
        qa, ka, va, qb, kb, vb, u = _inproj(xs, per_row(sc1), per_row(sh1), w_in_bf, tm=ms,
                                            rows_per_batch=ms, batch_off=0)
        s3 = lambda a: a.reshape(bs, ts, a.shape[-1])
        ka_all = jnp.concatenate([cache_a_k[l].reshape(bs, na, A_WIDTH), s3(ka)], axis=1)
        va_all = jnp.concatenate([cache_a_v[l].reshape(bs, na, A_WIDTH), s3(va)], axis=1)
        kb_all = jnp.concatenate([cache_b_k[l].reshape(bs, nb, B_KV_WIDTH), s3(kb)], axis=1)
        vb_all = jnp.concatenate([cache_b_v[l].reshape(bs, nb, B_KV_WIDTH), s3(vb)], axis=1)
        oa = _attention(s3(qa), ka_all, va_all, bias_a_s, None,
                        cq=ts, n_prev=None, n_heads=A_HEADS, group=1, n_seq=SAMPLE_SEQS)
        ob = _attention(s3(qb), kb_all, vb_all, t5_s, sinks[l],
                        cq=ts, n_prev=None, n_heads=B_HEADS, group=B_GROUP, n_seq=SAMPLE_SEQS)
        prev = jnp.pad(state_conv[l], ((0, 0), (CONV_HALO - (CONV_W - 1), 0), (0, 0)))
        oc = _conv_tail(prev, s3(u), *conv_args)
        x1s, h2s, idxs, gates = _outproj(
            oa.reshape(ms, -1), ob.reshape(ms, -1), oc.reshape(ms, -1), xs,
            per_row(g1), per_row(sc2), per_row(sh2),
            w_out_bf, lng1, lnb1, wr_hi, wr_lo, br, tm=ms, rows_per_batch=ms, batch_off=0)
        u_ext = jnp.concatenate([state_conv[l], s3(u)], axis=1)
        states_s.append((
            ka_all[:, -na:].reshape(bs, na, A_HEADS, HEAD_DIM),
            va_all[:, -na:].reshape(bs, na, A_HEADS, HEAD_DIM),
            kb_all[:, -nb:].reshape(bs, nb, B_KV_HEADS, HEAD_DIM),
            vb_all[:, -nb:].reshape(bs, nb, B_KV_HEADS, HEAD_DIM),
            u_ext[:, -(CONV_W - 1):]))

        qa, ka, va, qb, kb, vb, u = _inproj(xp, sc1, sh1, w_in_bf, tm=ROW_TILE,
                                            rows_per_batch=tp, batch_off=0)
        r3 = lambda a: a.reshape(bp, tp, a.shape[-1])
        oa = _band_attention(r3(qa), r3(ka), r3(va), bias_a_p, None, **A_BAND)
        ob = _band_attention(r3(qb), r3(kb), r3(vb), t5_p, sinks[l], **B_BAND)
        oc = _conv_tail(conv_zero, r3(u), *conv_args)
        x1p, h2, idxp, gatep = _outproj(
            oa.reshape(mp, -1), ob.reshape(mp, -1), oc.reshape(mp, -1), xp, g1, sc2, sh2,
            w_out_bf, lng1, lnb1, wr_hi, wr_lo, br, tm=OUTPROJ_TILE, rows_per_batch=tp, batch_off=0,
            h2_tail=h2s)
        states_p.append((
            r3(ka)[:, tp - A_REACH:].reshape(bp, A_REACH, A_HEADS, HEAD_DIM),
            r3(va)[:, tp - A_REACH:].reshape(bp, A_REACH, A_HEADS, HEAD_DIM),
            r3(kb)[:, tp - B_WINDOW:].reshape(bp, B_WINDOW, B_KV_HEADS, HEAD_DIM),
            r3(vb)[:, tp - B_WINDOW:].reshape(bp, B_WINDOW, B_KV_HEADS, HEAD_DIM),
            r3(u)[:, tp - (CONV_W - 1):]))

        top_idx = jnp.concatenate([idxp[:, :TOP_K], idxs[:, :TOP_K]], axis=0)
        block_e, n_used, idx_flat, n_rows = _routing_tables(top_idx, n_tok)
        picked = _moe(block_e, n_used, idx_flat, h2, w_e_in, b_e_in, w_e_out, b_e_out,
                      n_rows=n_rows, layer=l)
        xp = _combine(picked, gatep, x1p, g2, lng2, lnb2, tm=ROW_TILE, rows_per_batch=tp,
                      batch_off=0, row_off=0, n_tok=n_tok)
        xs = _combine(picked, gates, x1s, per_row(g2), lng2, lnb2, tm=ms, rows_per_batch=ms,
                      batch_off=0, row_off=mp, n_tok=n_tok)

    a_k_p, a_v_p, b_k_p, b_v_p, conv_p = (jnp.stack(z) for z in zip(*states_p))
    a_k_s, a_v_s, b_k_s, b_v_s, conv_s = (jnp.stack(z) for z in zip(*states_s))
    return (xp.reshape(bp, tp, d), xs.reshape(bs, ts, d), a_k_p, a_v_p, b_k_p, b_v_p, conv_p,
            a_k_s, a_v_s, b_k_s, b_v_s, conv_s)
```

```python
import functools
import math

import jax
import jax.numpy as jnp
import numpy as np
from jax import lax
from jax.experimental import pallas as pl
from jax.experimental.pallas import tpu as pltpu

F32 = jnp.float32
BF16 = jnp.bfloat16

D_MODEL = 1024
DEPTH = 2
CHUNK = 64
HEAD_DIM = 64
ATTN_SCALE = HEAD_DIM ** -0.5
A_HEADS = 4
A_WIDTH = A_HEADS * HEAD_DIM
A_PREV_CHUNKS = 8
A_REACH = A_PREV_CHUNKS * CHUNK
A_REL_CLIP = 128
B_HEADS = 8
B_KV_HEADS = 2
B_GROUP = B_HEADS // B_KV_HEADS
B_WIDTH = B_HEADS * HEAD_DIM
B_KV_WIDTH = B_KV_HEADS * HEAD_DIM
B_WINDOW = 128
B_PREV_CHUNKS = B_WINDOW // CHUNK
T5_BUCKETS = 32
T5_MAX_DISTANCE = 128
C_CH = D_MODEL // 4
CONV_W = 31
CONV_HALO = 32
N_EXPERTS = 32
TOP_K = 4
D_FF = D_MODEL
SWIGLU_LIMIT = 7.0
SWIGLU_ALPHA = 1.702
DEEPNORM_ALPHA = (2 * DEPTH) ** 0.25
LN_EPS = 1e-5
NEG_INF = -1e30

LANES = 128
ROW_TILE = 512
OUTPROJ_TILE = 512
SAMPLE_SEQS = 4
MOE_TILE = 256
IDX_STRIDE = 1024
MOE_IDX_SLOTS = 4
A_BAND = dict(g_chunks=4, n_prev=A_PREV_CHUNKS, n_kv=A_HEADS, group=1)
B_BAND = dict(g_chunks=2, n_prev=B_PREV_CHUNKS, n_kv=B_KV_HEADS, group=B_GROUP)
VMEM_LIMIT = 56 * 1024 * 1024


def _cparams(n_axes=1, vmem=None):
    return pltpu.CompilerParams(dimension_semantics=("arbitrary",) * n_axes,
                                vmem_limit_bytes=vmem)


def _dot(a, b):
    return jnp.dot(a, b, preferred_element_type=F32)


def _layer_norm(z, g, b):
    mu = jnp.mean(z, axis=-1, keepdims=True)
    d = z - mu
    var = jnp.mean(d * d, axis=-1, keepdims=True)
    return d * lax.rsqrt(var + LN_EPS) * g + b


SUBLANES = 8
ROW_PIECES = D_MODEL // LANES


def _store_token_tiles(ref, val):
    n = val.shape[0]
    for c in range(ROW_PIECES):
        ref[pl.ds(c, n, stride=ROW_PIECES), :] = val[:, c * LANES:(c + 1) * LANES]


def _load_token_tiles(ref, n):
    return jnp.concatenate(
        [ref[pl.ds(c, n, stride=ROW_PIECES), :] for c in range(ROW_PIECES)], axis=1)


def _split_bf16(a):
    hi = a.astype(BF16)
    lo = (a - hi.astype(F32)).astype(BF16)
    return hi, lo


def _adaln_kernel(c_ref, w_ref, b_ref, o_ref):
    c = c_ref[...]
    a_hi, a_lo = _split_bf16(c * jax.nn.sigmoid(c))
    w_hi, w_lo = _split_bf16(w_ref[...])
    o_ref[...] = _dot(a_hi, w_hi) + _dot(a_lo, w_hi) + _dot(a_hi, w_lo) + b_ref[...]


def _adaln(c_all, w_all, b_all, layer):
    nb, d = c_all.shape
    n = w_all.shape[2]
    tn = 1536
    return pl.pallas_call(
        _adaln_kernel,
        out_shape=jax.ShapeDtypeStruct((nb, n), F32),
        grid=(n // tn,),
        in_specs=[pl.BlockSpec((nb, d), lambda j: (0, 0)),
                  pl.BlockSpec((None, d, tn), lambda j: (layer, 0, j)),
                  pl.BlockSpec((None, 1, tn), lambda j: (layer, 0, j))],
        out_specs=pl.BlockSpec((nb, tn), lambda j: (0, j)),
        compiler_params=_cparams(1, VMEM_LIMIT),
        name="adaln",
    )(c_all, w_all, b_all.reshape(b_all.shape[0], 1, n))


_QA = (0, A_WIDTH)
_KA = (A_WIDTH, 2 * A_WIDTH)
_VA = (2 * A_WIDTH, 3 * A_WIDTH)
_QB = (3 * A_WIDTH, 3 * A_WIDTH + B_WIDTH)
_KB = (_QB[1], _QB[1] + B_KV_WIDTH)
_VB = (_KB[1], _KB[1] + B_KV_WIDTH)
_GA = (_VB[1], _VB[1] + C_CH)
_GG = (_GA[1], _GA[1] + C_CH)


def _inproj_kernel(x_ref, sc_ref, sh_ref, w_ref,
                   qa_ref, ka_ref, va_ref, qb_ref, kb_ref, vb_ref, u_ref):
    h = (x_ref[...] * (1.0 + sc_ref[...]) + sh_ref[...]).astype(BF16)

    def proj(cols):
        return _dot(h, w_ref[:, cols[0]:cols[1]])

    qa_ref[...] = (proj(_QA) * ATTN_SCALE).astype(BF16)
    ka_ref[...] = proj(_KA)
    va_ref[...] = proj(_VA)
    qb_ref[...] = (proj(_QB) * ATTN_SCALE).astype(BF16)
    kb_ref[...] = proj(_KB)
    vb_ref[...] = proj(_VB)
    u_ref[...] = proj(_GA) * jax.nn.sigmoid(proj(_GG))


def _mod_spec(mod, tm, d, bpb, batch_off, last=None):
    blk = (lambda i: i) if last is None else (lambda i: jnp.minimum(i, last))
    if mod.ndim == 2:
        return pl.BlockSpec((tm, d), lambda i: (blk(i), 0))
    return pl.BlockSpec((None, 1, d), lambda i: (blk(i) // bpb + batch_off, 0, 0))


def _inproj(x, sc, sh, w_bf, *, tm, rows_per_batch, batch_off):
    m, d = x.shape
    bpb = rows_per_batch // tm
    mod_spec = _mod_spec(sc, tm, d, bpb, batch_off)
    widths = (A_WIDTH, A_WIDTH, A_WIDTH, B_WIDTH, B_KV_WIDTH, B_KV_WIDTH, C_CH)
    dtypes = (BF16, F32, F32, BF16, F32, F32, F32)
    return pl.pallas_call(
        _inproj_kernel,
        out_shape=[jax.ShapeDtypeStruct((m, w), dt) for w, dt in zip(widths, dtypes)],
        grid=(m // tm,),
        in_specs=[pl.BlockSpec((tm, d), lambda i: (i, 0)), mod_spec, mod_spec,
                  pl.BlockSpec(w_bf.shape, lambda i: (0, 0))],
        out_specs=[pl.BlockSpec((tm, w), lambda i: (i, 0)) for w in widths],
        compiler_params=_cparams(1, VMEM_LIMIT),
        name="inproj",
    )(x, sc, sh, w_bf)


def _cache_attn_kernel(*refs, n_seq, n_heads, group, use_sink):
    if use_sink:
        q_ref, k_ref, v_ref, bias_ref, sink_ref, o_ref = refs
    else:
        q_ref, k_ref, v_ref, bias_ref, o_ref = refs
    for b in range(n_seq):
        qt = q_ref[b]
        kt = k_ref[b].astype(BF16)
        vt = v_ref[b].astype(BF16)
        outs = []
        for h in range(n_heads):
            n = h // group
            qh = qt[:, h * HEAD_DIM:(h + 1) * HEAD_DIM]
            kh = kt[:, n * HEAD_DIM:(n + 1) * HEAD_DIM]
            vh = vt[:, n * HEAD_DIM:(n + 1) * HEAD_DIM]
            s = lax.dot_general(qh, kh, (((1,), (1,)), ((), ())),
                                preferred_element_type=F32)
            s = s + bias_ref[h]
            m = jnp.max(s, axis=-1, keepdims=True)
            if use_sink:
                m = jnp.maximum(m, sink_ref[h])
            e = jnp.exp(s - m)
            den = jnp.sum(e, axis=-1, keepdims=True)
            if use_sink:
                den = den + jnp.exp(sink_ref[h] - m)
            outs.append(_dot(e.astype(BF16), vh) / den)
        o_ref[b] = jnp.concatenate(outs, axis=-1).astype(o_ref.dtype)


def _cache_attention(q, k, v, bias, sinks, *, n_heads, group, n_seq):
    b, t, qw = q.shape
    tk, kw = k.shape[1], k.shape[2]
    use_sink = sinks is not None
    kern = functools.partial(_cache_attn_kernel, n_seq=n_seq, n_heads=n_heads, group=group,
                             use_sink=use_sink)
    in_specs = [pl.BlockSpec((n_seq, t, qw), lambda i: (i, 0, 0)),
                pl.BlockSpec((n_seq, tk, kw), lambda i: (i, 0, 0)),
                pl.BlockSpec((n_seq, tk, kw), lambda i: (i, 0, 0)),
                pl.BlockSpec(bias.shape, lambda i: (0, 0, 0))]
    args = [q, k, v, bias]
    if use_sink:
        in_specs.append(pl.BlockSpec(memory_space=pltpu.SMEM))
        args.append(sinks)
    return pl.pallas_call(
        kern,
        out_shape=jax.ShapeDtypeStruct((b, t, qw), BF16),
        grid=(b // n_seq,),
        in_specs=in_specs,
        out_specs=pl.BlockSpec((n_seq, t, qw), lambda i: (i, 0, 0)),
        compiler_params=_cparams(1, VMEM_LIMIT),
        name="cache_attn_sink" if use_sink else "cache_attn",
    )(*args)


def _band_attn_kernel(*refs, t, g_chunks, n_prev, n_kv, group, use_sink):
    if use_sink:
        q_ref, k_ref, v_ref, bias_ref, sink_ref, o_ref, qs_ref, kp_ref, vp_ref = refs
    else:
        q_ref, k_ref, v_ref, bias_ref, o_ref, qs_ref, kp_ref, vp_ref = refs
    pad = n_prev * CHUNK
    gq = g_chunks * CHUNK
    u = pad + gq
    m = group * gq
    hd = HEAD_DIM
    for n in range(n_kv):
        zeros = jnp.zeros((pad, hd), BF16)
        kp_ref[n, 0:pad, :] = zeros
        vp_ref[n, 0:pad, :] = zeros
        kp_ref[n, pad:pad + t, :] = k_ref[:, n * hd:(n + 1) * hd].astype(BF16)
        vp_ref[n, pad:pad + t, :] = v_ref[:, n * hd:(n + 1) * hd].astype(BF16)
    for h in range(n_kv * group):
        qs_ref[h] = q_ref[:, h * hd:(h + 1) * hd]
    key_pos = lax.broadcasted_iota(jnp.int32, (m, u), 1)
    row = lax.broadcasted_iota(jnp.int32, (m, 1), 0)

    def body(g, carry):
        q0 = pl.multiple_of(g * gq, gq)
        valid = key_pos >= pad - q0
        outs = []
        for n in range(n_kv):
            qstk = jnp.concatenate(
                [qs_ref[n * group + j, pl.ds(q0, gq), :] for j in range(group)], axis=0)
            kt = kp_ref[n, pl.ds(q0, u), :]
            vt = vp_ref[n, pl.ds(q0, u), :]
            s = lax.dot_general(qstk, kt, (((1,), (1,)), ((), ())),
                                preferred_element_type=F32)
            s = jnp.where(valid, s + bias_ref[n], NEG_INF)
            mx = jnp.max(s, axis=-1, keepdims=True)
            if use_sink:
                sink = jnp.full((m, 1), sink_ref[n * group], F32)
                for j in range(1, group):
                    sink = jnp.where(row >= j * gq, sink_ref[n * group + j], sink)
                mx = jnp.maximum(mx, sink)
            e = jnp.exp(s - mx)
            den = jnp.sum(e, axis=-1, keepdims=True)
            if use_sink:
                den = den + jnp.exp(sink - mx)
            o = _dot(e.astype(BF16), vt) / den
            outs.extend(o[j * gq:(j + 1) * gq, :] for j in range(group))
        o_ref[pl.ds(q0, gq), :] = jnp.concatenate(outs, axis=-1).astype(o_ref.dtype)
        return carry
    lax.fori_loop(0, t // gq, body, 0)


def _band_bias(head_bias, *, g_chunks, n_prev, n_kv, group):
    pad, gq = n_prev * CHUNK, g_chunks * CHUNK
    u = pad + gq
    r = np.arange(gq)[:, None]
    kk = np.arange(u)[None, :]
    lo = (r // CHUNK) * CHUNK
    in_band = (kk >= lo) & (kk < lo + pad + CHUNK)
    tile = jnp.where(in_band[None], _rel_bias_tile(head_bias, gq, u, pad), NEG_INF)
    return tile.reshape(n_kv, group * gq, u)


def _rel_bias_tile(head_bias, rows, cols, pad):
    n_off = rows + cols - 1
    vec = head_bias(np.arange(n_off) - pad - (rows - 1))
    h = vec.shape[0]
    padded = jnp.concatenate([vec, jnp.zeros((h, 1), vec.dtype)], axis=1)
    skew = jnp.tile(padded, (1, rows))[:, :rows * n_off].reshape(h, rows, n_off)
    return skew[:, :, rows - 1:rows - 1 + cols]


def _band_attention(q, k, v, bias, sinks, *, g_chunks, n_prev, n_kv, group):
    b, t, qw = q.shape
    kw = k.shape[2]
    pad = n_prev * CHUNK
    use_sink = sinks is not None
    kern = functools.partial(_band_attn_kernel, t=t, g_chunks=g_chunks, n_prev=n_prev,
                             n_kv=n_kv, group=group, use_sink=use_sink)
    in_specs = [pl.BlockSpec((None, t, qw), lambda i: (i, 0, 0)),
                pl.BlockSpec((None, t, kw), lambda i: (i, 0, 0)),
                pl.BlockSpec((None, t, kw), lambda i: (i, 0, 0)),
                pl.BlockSpec(bias.shape, lambda i: (0, 0, 0))]
    args = [q, k, v, bias]
    if use_sink:
        in_specs.append(pl.BlockSpec(memory_space=pltpu.SMEM))
        args.append(sinks)
    return pl.pallas_call(
        kern,
        out_shape=jax.ShapeDtypeStruct((b, t, qw), BF16),
        grid=(b,),
        in_specs=in_specs,
        out_specs=pl.BlockSpec((None, t, qw), lambda i: (i, 0, 0)),
        scratch_shapes=[pltpu.VMEM((n_kv * group, t, HEAD_DIM), BF16),
                        pltpu.VMEM((n_kv, pad + t, HEAD_DIM), BF16),
                        pltpu.VMEM((n_kv, pad + t, HEAD_DIM), BF16)],
        compiler_params=_cparams(1, VMEM_LIMIT),
        name="band_attn_sink" if use_sink else "band_attn",
    )(*args)


def _conv_kernel(prev_ref, u_ref, w_ref, cb_ref, g_ref, b_ref, o_ref, up_ref, sh_ref, *, t, tt):
    up_ref[0:CONV_HALO, :] = prev_ref[...]
    up_ref[CONV_HALO:CONV_HALO + t, :] = u_ref[...]
    lead = CONV_HALO - (CONV_W - 1)

    def tile(t0):
        win = up_ref[pl.ds(t0, tt + CONV_HALO), :]
        acc = jnp.zeros((tt, C_CH), F32)
        for phase in range(SUBLANES):
            offs = [j + lead for j in range(CONV_W) if (j + lead) % SUBLANES == phase]
            if not offs:
                continue
            span = max(offs) - phase + tt
            if phase:
                sh_ref[phase, 0:span, :] = win[phase:phase + span, :]
            for off in offs:
                a8 = off - phase
                rows = sh_ref[phase, a8:a8 + tt, :] if phase else win[a8:a8 + tt, :]
                acc = acc + rows * w_ref[off - lead:off - lead + 1, :]
        y = _layer_norm(acc + cb_ref[...], g_ref[...], b_ref[...])
        o_ref[pl.ds(t0, tt), :] = (y * jax.nn.sigmoid(y)).astype(o_ref.dtype)

    if t == tt:
        tile(0)
    else:
        def body(i, carry):
            tile(pl.multiple_of(i * tt, tt))
            return carry
        lax.fori_loop(0, t // tt, body, 0)


def _conv_tail(prev, u, conv_w, conv_b, ln_g, ln_b):
    b, t, c = u.shape
    tt = min(t, 128)
    vec = lambda a: a.reshape(1, c)
    vspec = pl.BlockSpec((1, c), lambda i: (0, 0))
    return pl.pallas_call(
        functools.partial(_conv_kernel, t=t, tt=tt),
        out_shape=jax.ShapeDtypeStruct((b, t, c), BF16),
        grid=(b,),
        in_specs=[pl.BlockSpec((None, CONV_HALO, c), lambda i: (i, 0, 0)),
                  pl.BlockSpec((None, t, c), lambda i: (i, 0, 0)),
                  pl.BlockSpec((CONV_W, c), lambda i: (0, 0)),
                  vspec, vspec, vspec],
        out_specs=pl.BlockSpec((None, t, c), lambda i: (i, 0, 0)),
        scratch_shapes=[pltpu.VMEM((CONV_HALO + t, c), F32),
                        pltpu.VMEM((SUBLANES, tt + CONV_HALO, c), F32)],
        compiler_params=_cparams(1, VMEM_LIMIT),
        name="conv_tail",
    )(prev, u, conv_w, vec(conv_b), vec(ln_g), vec(ln_b))


def _outproj_tail_kernel(*refs, n_blk):
    tail_ref, h2_ref = refs[13], refs[15]
    i = pl.program_id(0)

    @pl.when(i < n_blk)
    def _():
        _outproj_kernel(*refs[:13], *refs[14:])

    @pl.when(i >= n_blk)
    def _():
        h2_ref[...] = tail_ref[...]


def _outproj_kernel(oa_ref, ob_ref, oc_ref, x_ref, g1_ref, sc2_ref, sh2_ref, wo_ref,
                    lng_ref, lnb_ref, wr_hi_ref, wr_lo_ref, br_ref,
                    x1_ref, h2_ref, idx_ref, gate_ref):
    mix = (_dot(oa_ref[...], wo_ref[0:A_WIDTH, :])
           + _dot(ob_ref[...], wo_ref[A_WIDTH:A_WIDTH + B_WIDTH, :])
           + _dot(oc_ref[...], wo_ref[A_WIDTH + B_WIDTH:, :]))
    x1 = _layer_norm(DEEPNORM_ALPHA * x_ref[...] + g1_ref[...] * mix, lng_ref[...], lnb_ref[...])
    x1_ref[...] = x1
    h2 = x1 * (1.0 + sc2_ref[...]) + sh2_ref[...]
    _store_token_tiles(h2_ref, h2)
    h_hi, h_lo = _split_bf16(h2)
    logits = (_dot(h_hi, wr_hi_ref[...]) + _dot(h_lo, wr_hi_ref[...])
              + _dot(h_hi, wr_lo_ref[...]) + br_ref[...])
    lane = lax.broadcasted_iota(jnp.int32, logits.shape, 1)
    cur = jnp.where(lane < N_EXPERTS, logits, NEG_INF)
    vals, idxs = [], []
    for _ in range(TOP_K):
        m = jnp.max(cur, axis=-1, keepdims=True)
        i = jnp.min(jnp.where(cur == m, lane, LANES), axis=-1, keepdims=True)
        vals.append(m)
        idxs.append(i)
        cur = jnp.where(lane == i, NEG_INF, cur)
    es = [jnp.exp(v - vals[0]) for v in vals]
    den = es[0] + es[1] + es[2] + es[3]
    idx_out = jnp.zeros(logits.shape, jnp.int32)
    gate_out = jnp.zeros(logits.shape, F32)
    for k in range(TOP_K):
        idx_out = jnp.where(lane == k, idxs[k], idx_out)
        gate_out = jnp.where(lane == k, es[k] / den, gate_out)
    idx_ref[...] = idx_out
    gate_ref[...] = gate_out


def _outproj(oa, ob, oc, x, g1, sc2, sh2, wo_bf, ln_g, ln_b, wr_hi, wr_lo, br,
             *, tm, rows_per_batch, batch_off, h2_tail=None):
    m, d = x.shape
    bpb = rows_per_batch // tm
    n_blk = m // tm
    last = n_blk - 1
    row = lambda w: pl.BlockSpec((tm, w), lambda i: (jnp.minimum(i, last), 0))
    mod_spec = _mod_spec(g1, tm, d, bpb, batch_off, last)
    const = lambda a: pl.BlockSpec(a.shape, lambda i: (0, 0))
    in_specs = [row(A_WIDTH), row(B_WIDTH), row(C_CH), row(d), mod_spec, mod_spec, mod_spec,
                const(wo_bf), const(ln_g), const(ln_b), const(wr_hi), const(wr_lo), const(br)]
    args = [oa, ob, oc, x, g1, sc2, sh2, wo_bf, ln_g, ln_b, wr_hi, wr_lo, br]
    if h2_tail is None:
        kern, steps, h2_rows = _outproj_kernel, n_blk, m
    else:
        tail_tokens = h2_tail.shape[0] // ROW_PIECES
        assert h2_tail.shape[1] == LANES and tail_tokens % tm == 0
        kern = functools.partial(_outproj_tail_kernel, n_blk=n_blk)
        steps, h2_rows = n_blk + tail_tokens // tm, m + tail_tokens
        in_specs.append(pl.BlockSpec((tm * ROW_PIECES, LANES),
                                     lambda i: (jnp.maximum(i - n_blk, 0), 0)))
        args.append(h2_tail)
    return pl.pallas_call(
        kern,
        out_shape=[jax.ShapeDtypeStruct((m, d), F32),
                   jax.ShapeDtypeStruct((h2_rows * ROW_PIECES, LANES), F32),
                   jax.ShapeDtypeStruct((m, LANES), jnp.int32),
                   jax.ShapeDtypeStruct((m, LANES), F32)],
        grid=(steps,),
        in_specs=in_specs,
        out_specs=[row(d), pl.BlockSpec((tm * ROW_PIECES, LANES), lambda i: (i, 0)),
                   row(LANES), row(LANES)],
        compiler_params=_cparams(1, VMEM_LIMIT),
        name="outproj_route",
    )(*args)


def _moe_kernel(be_ref, nu_ref, idx_hbm, h2_hbm, w1_ref, b1_ref, w2_ref, b2_ref,
                picked_hbm, idx_smem, xbuf, obuf, w1b, w2b, sem_idx, sem_g, sem_s, *, tm):
    i = pl.program_id(0)
    nu = nu_ref[0]
    rp = ROW_PIECES

    def table_copy(blk, slot):
        return pltpu.make_async_copy(
            idx_hbm.at[pl.ds(pl.multiple_of(blk * IDX_STRIDE, IDX_STRIDE), IDX_STRIDE)],
            idx_smem.at[pl.ds(pl.multiple_of(slot * IDX_STRIDE, IDX_STRIDE), IDX_STRIDE)],
            sem_idx.at[slot])

    def row_loop(body, static_rows):
        if static_rows:
            for r in range(tm):
                body(r)
        else:
            lax.fori_loop(0, tm, lambda r, c: (body(r), c)[1], 0)

    def start_gathers(tslot, bslot, static_rows=True):
        base = tslot * IDX_STRIDE

        def body(r):
            src = pl.multiple_of(idx_smem[base + r], rp)
            row0 = r * rp if isinstance(r, int) else pl.multiple_of(r * rp, rp)
            pltpu.make_async_copy(h2_hbm.at[pl.ds(src, rp), :],
                                  xbuf.at[bslot, pl.ds(row0, rp), :], sem_g.at[bslot]).start()
        row_loop(body, static_rows)

    def wait_gathers(bslot):
        pltpu.make_async_copy(h2_hbm.at[pl.ds(0, tm * rp), :], xbuf.at[bslot],
                              sem_g.at[bslot]).wait()

    def start_scatters(tslot, bslot):
        base = tslot * IDX_STRIDE + tm

        def body(r):
            dst = pl.multiple_of(idx_smem[base + r], rp)
            pltpu.make_async_copy(obuf.at[bslot, pl.ds(r * rp, rp), :],
                                  picked_hbm.at[pl.ds(dst, rp), :], sem_s.at[bslot]).start()
        row_loop(body, True)

    def wait_scatters(bslot):
        pltpu.make_async_copy(obuf.at[bslot], picked_hbm.at[pl.ds(0, tm * rp), :],
                              sem_s.at[bslot]).wait()

    @pl.when(i < nu)
    def _():
        bslot = i % 2
        tslot = i % MOE_IDX_SLOTS

        @pl.when(i == 0)
        def _():
            first = table_copy(0, 0)
            first.start()
            first.wait()
            start_gathers(0, 0, static_rows=False)

            @pl.when(nu > 1)
            def _():
                table_copy(1, 1).start()

        @pl.when(i + 2 < nu)
        def _():
            table_copy(i + 2, (i + 2) % MOE_IDX_SLOTS).start()

        @pl.when(i + 1 < nu)
        def _():
            nslot = (i + 1) % MOE_IDX_SLOTS
            table_copy(i + 1, nslot).wait()
            start_gathers(nslot, 1 - bslot)

        @pl.when(jnp.logical_or(i == 0, be_ref[i] != be_ref[jnp.maximum(i - 1, 0)]))
        def _():
            w1b[...] = w1_ref[...].astype(BF16)
            w2b[...] = w2_ref[...].astype(BF16)

        wait_gathers(bslot)

        @pl.when(i >= 2)
        def _():
            wait_scatters(bslot)

        x = _load_token_tiles(xbuf.at[bslot], tm).astype(BF16)
        x_glu = jnp.minimum(_dot(x, w1b[:, 0:D_FF]) + b1_ref[:, 0:D_FF], SWIGLU_LIMIT)
        x_lin = jnp.clip(_dot(x, w1b[:, D_FF:]) + b1_ref[:, D_FF:], -SWIGLU_LIMIT, SWIGLU_LIMIT)
        act = x_glu * jax.nn.sigmoid(SWIGLU_ALPHA * x_glu) * (x_lin + 1.0)
        _store_token_tiles(obuf.at[bslot], _dot(act.astype(BF16), w2b[...]) + b2_ref[...])
        start_scatters(tslot, bslot)

        @pl.when(i == nu - 1)
        def _():
            @pl.when(i >= 1)
            def _():
                wait_scatters(1 - bslot)
            wait_scatters(bslot)

    @pl.when(i >= nu)
    def _():
        @pl.when(i == nu)
        def _():
            obuf[0] = jnp.zeros(obuf.shape[1:], F32)
        cp = pltpu.make_async_copy(
            obuf.at[0],
            picked_hbm.at[pl.ds(pl.multiple_of(i * (tm * rp), tm * rp), tm * rp), :], sem_s.at[0])
        cp.start()
        cp.wait()


def _moe(block_e, n_used, idx_flat, h2, w1, b1, w2, b2, *, n_rows, layer):
    tm = MOE_TILE
    n_blocks = block_e.shape[0]
    d = D_MODEL
    tile_rows = tm * ROW_PIECES
    expert = lambda i, be, nu: (layer, be[i], 0, 0)
    grid_spec = pltpu.PrefetchScalarGridSpec(
        num_scalar_prefetch=2,
        grid=(n_blocks,),
        in_specs=[pl.BlockSpec(memory_space=pl.ANY),
                  pl.BlockSpec(memory_space=pl.ANY),
                  pl.BlockSpec((None, None, d, 2 * D_FF), expert),
                  pl.BlockSpec((None, None, 1, 2 * D_FF), expert),
                  pl.BlockSpec((None, None, D_FF, d), expert),
                  pl.BlockSpec((None, None, 1, d), expert)],
        out_specs=pl.BlockSpec(memory_space=pl.ANY),
        scratch_shapes=[pltpu.SMEM((MOE_IDX_SLOTS * IDX_STRIDE,), jnp.int32),
                        pltpu.VMEM((2, tile_rows, LANES), F32),
                        pltpu.VMEM((2, tile_rows, LANES), F32),
                        pltpu.VMEM((d, 2 * D_FF), BF16), pltpu.VMEM((D_FF, d), BF16),
                        pltpu.SemaphoreType.DMA((MOE_IDX_SLOTS,)),
                        pltpu.SemaphoreType.DMA((2,)), pltpu.SemaphoreType.DMA((2,))])
    return pl.pallas_call(
        functools.partial(_moe_kernel, tm=tm),
        out_shape=jax.ShapeDtypeStruct((n_rows * ROW_PIECES, LANES), F32),
        grid_spec=grid_spec,
        compiler_params=_cparams(1, VMEM_LIMIT),
        name="moe_experts",
    )(block_e, n_used, idx_flat, h2, w1, b1.reshape(b1.shape[0], N_EXPERTS, 1, -1), w2,
      b2.reshape(b2.shape[0], N_EXPERTS, 1, -1))


def _routing_tables(top_idx, n_tok):
    tm = MOE_TILE
    n_assign = n_tok * TOP_K
    n_blocks = -(-n_assign // tm) + N_EXPERTS
    n_rows = n_blocks * tm
    i32 = jnp.int32
    flat_e = top_idx.reshape(n_assign)
    key_bits = (n_assign - 1).bit_length()
    assert N_EXPERTS << key_bits < 2 ** 31
    keys = jnp.sort(flat_e * (1 << key_bits) + jnp.arange(n_assign, dtype=i32))
    order = keys & ((1 << key_bits) - 1)
    experts = jnp.arange(N_EXPERTS, dtype=i32)
    counts = jnp.sum((flat_e[:, None] == experts[None, :]).astype(i32), axis=0)
    padded = (counts + tm - 1) // tm * tm
    pad_end = jnp.cumsum(padded)
    pad_start = pad_end - padded
    grp_start = jnp.cumsum(counts) - counts
    n_used = pad_end[-1] // tm
    blk = jnp.arange(n_blocks, dtype=i32)
    used = blk < n_used
    expert_at = lambda start: jnp.minimum(
        jnp.sum((pad_end[None, :] <= start[:, None]).astype(i32), axis=1), N_EXPERTS - 1)
    e_blk = expert_at(blk * tm)
    e_last = expert_at(((n_used - 1) * tm).reshape(1))[0]
    block_e = jnp.where(used, e_blk, e_last)
    row = blk[:, None] * tm + jnp.arange(tm, dtype=i32)[None, :]
    off = row - pad_start[e_blk][:, None]
    cnt = counts[e_blk][:, None]
    grp = grp_start[e_blk][:, None]
    valid = used[:, None] & (off < cnt)
    a = order[jnp.clip(grp + off, 0, n_assign - 1)]
    tok = a // TOP_K
    row_tok = jnp.where(valid, tok, 0)
    real_before = jnp.where(used[:, None], grp + cnt, n_assign)
    row_dst = jnp.where(valid, (a % TOP_K) * n_tok + tok, n_assign + row - real_before)
    idx = jnp.concatenate(
        [row_tok * ROW_PIECES, row_dst * ROW_PIECES,
         jnp.zeros((n_blocks, IDX_STRIDE - 2 * tm), i32)], axis=1)
    return block_e, n_used.reshape(1).astype(i32), idx.reshape(-1), n_rows


def _combine_kernel(p0_ref, p1_ref, p2_ref, p3_ref, gate_ref, x1_ref, g2_ref,
                    lng_ref, lnb_ref, o_ref):
    gate = gate_ref[...]
    n = gate.shape[0]
    y = (gate[:, 0:1] * _load_token_tiles(p0_ref, n) + gate[:, 1:2] * _load_token_tiles(p1_ref, n)
         + gate[:, 2:3] * _load_token_tiles(p2_ref, n) + gate[:, 3:4] * _load_token_tiles(p3_ref, n))
    o_ref[...] = _layer_norm(DEEPNORM_ALPHA * x1_ref[...] + g2_ref[...] * y,
                             lng_ref[...], lnb_ref[...])


def _combine(picked, gate, x1, g2, ln_g, ln_b, *, tm, rows_per_batch, batch_off, row_off, n_tok):
    m, d = x1.shape
    bpb = rows_per_batch // tm
    blk_off = row_off // tm
    k_stride = n_tok // tm
    pspec = lambda k: pl.BlockSpec((tm * ROW_PIECES, LANES),
                                   lambda i: (i + blk_off + k * k_stride, 0))
    row = lambda w: pl.BlockSpec((tm, w), lambda i: (i, 0))
    const = lambda a: pl.BlockSpec(a.shape, lambda i: (0, 0))
    return pl.pallas_call(
        _combine_kernel,
        out_shape=jax.ShapeDtypeStruct((m, d), F32),
        grid=(m // tm,),
        in_specs=[pspec(0), pspec(1), pspec(2), pspec(3), row(LANES), row(d),
                  _mod_spec(g2, tm, d, bpb, batch_off), const(ln_g), const(ln_b)],
        out_specs=row(d),
        compiler_params=_cparams(1, VMEM_LIMIT),
        name="combine_ln",
    )(picked, picked, picked, picked, gate, x1, g2, ln_g, ln_b)


def _clipped_rel_bias(rel, table):
    idx = np.clip(rel, -A_REL_CLIP, A_REL_CLIP) + A_REL_CLIP
    return jnp.moveaxis(table[idx].astype(F32), -1, 0)


def _t5_bucket(rel):
    nb = T5_BUCKETS // 2
    max_exact = nb // 2
    n = np.abs(rel)
    nf = np.maximum(n, 1).astype(np.float32)
    large = max_exact + (np.log(nf / max_exact) / math.log(T5_MAX_DISTANCE / max_exact)
                         * (nb - max_exact)).astype(np.int32)
    large = np.minimum(large, nb - 1)
    return np.where(rel > 0, nb, 0) + np.where(n < max_exact, n, large)


def _t5_rel_bias(rel, table):
    return jnp.moveaxis(table[_t5_bucket(rel)].astype(F32), -1, 0)


def kernel(x_prompt, x_sample, cache_a_k, cache_a_v, cache_b_k, cache_b_v, state_conv,
           c_prompt, c_sample, w_in, w_out, rel_bias_a, t5_bias, sinks, conv_w, conv_b,
           conv_ln_g, conv_ln_b, w_ada, b_ada, ln_g, ln_b, w_router, b_router,
           w_e_in, b_e_in, w_e_out, b_e_out):
    bp, tp, d = x_prompt.shape
    bs, ts, _ = x_sample.shape
    mp, ms = bp * tp, bs * ts
    n_tok = mp + ms
    na, nb = cache_a_k.shape[2], cache_b_k.shape[2]
    assert tp % ROW_TILE == 0 and mp % ts == 0 and ts >= CONV_W - 1 and tp >= A_REACH
    assert ms % OUTPROJ_TILE == 0 and tp % OUTPROJ_TILE == 0
    assert mp % ms == 0 and bs % SAMPLE_SEQS == 0

    xp = x_prompt.reshape(mp, d)
    xs = x_sample.reshape(ms, d)
    c_all = jnp.concatenate([c_prompt, c_sample], axis=0)

    t5_p = _band_bias(lambda rel: _t5_rel_bias(rel, t5_bias), **B_BAND)
    t5_s = _rel_bias_tile(lambda rel: _t5_rel_bias(rel, t5_bias), ts, nb + ts, nb)
    conv_zero = jnp.zeros((bp, CONV_HALO, C_CH), F32)

    states_p, states_s = [], []
    for l in range(DEPTH):
        mod = _adaln(c_all, w_ada, b_ada, l)
        sh1, sc1, g1, sh2, sc2, g2 = (mod[:, j * d:(j + 1) * d].reshape(bp + bs, 1, d)
                                      for j in range(6))
        w_in_bf = w_in[l].astype(BF16)
        w_out_bf = w_out[l].astype(BF16)
        wr = jnp.pad(w_router[l], ((0, 0), (0, LANES - N_EXPERTS)))
        wr_hi = wr.astype(BF16)
        wr_lo = (wr - wr_hi.astype(F32)).astype(BF16)
        br = jnp.pad(b_router[l], (0, LANES - N_EXPERTS)).reshape(1, LANES)
        lng1, lnb1 = ln_g[l, 0].reshape(1, d), ln_b[l, 0].reshape(1, d)
        lng2, lnb2 = ln_g[l, 1].reshape(1, d), ln_b[l, 1].reshape(1, d)
        bias_a_p = _band_bias(lambda rel: _clipped_rel_bias(rel, rel_bias_a[l]), **A_BAND)
        bias_a_s = _rel_bias_tile(lambda rel: _clipped_rel_bias(rel, rel_bias_a[l]),
                                  ts, na + ts, na)
        conv_args = (conv_w[l], conv_b[l], conv_ln_g[l], conv_ln_b[l])

        per_row = lambda a: jnp.broadcast_to(a[bp:], (bs, ts, d)).reshape(ms, d)
        qa, ka, va, qb, kb, vb, u = _inproj(xs, per_row(sc1), per_row(sh1), w_in_bf, tm=ms,
                                            rows_per_batch=ms, batch_off=0)
        s3 = lambda a: a.reshape(bs, ts, a.shape[-1])
        ka_all = jnp.concatenate([cache_a_k[l].reshape(bs, na, A_WIDTH), s3(ka)], axis=1)
        va_all = jnp.concatenate([cache_a_v[l].reshape(bs, na, A_WIDTH), s3(va)], axis=1)
        kb_all = jnp.concatenate([cache_b_k[l].reshape(bs, nb, B_KV_WIDTH), s3(kb)], axis=1)
        vb_all = jnp.concatenate([cache_b_v[l].reshape(bs, nb, B_KV_WIDTH), s3(vb)], axis=1)
        oa = _cache_attention(s3(qa), ka_all, va_all, bias_a_s, None,
                              n_heads=A_HEADS, group=1, n_seq=SAMPLE_SEQS)
        ob = _cache_attention(s3(qb), kb_all, vb_all, t5_s, sinks[l],
                              n_heads=B_HEADS, group=B_GROUP, n_seq=SAMPLE_SEQS)
        prev = jnp.pad(state_conv[l], ((0, 0), (CONV_HALO - (CONV_W - 1), 0), (0, 0)))
        oc = _conv_tail(prev, s3(u), *conv_args)
        x1s, h2s, idxs, gates = _outproj(
            oa.reshape(ms, -1), ob.reshape(ms, -1), oc.reshape(ms, -1), xs,
            per_row(g1), per_row(sc2), per_row(sh2),
            w_out_bf, lng1, lnb1, wr_hi, wr_lo, br, tm=ms, rows_per_batch=ms, batch_off=0)
        u_ext = jnp.concatenate([state_conv[l], s3(u)], axis=1)
        states_s.append((
            ka_all[:, -na:].reshape(bs, na, A_HEADS, HEAD_DIM),
            va_all[:, -na:].reshape(bs, na, A_HEADS, HEAD_DIM),
            kb_all[:, -nb:].reshape(bs, nb, B_KV_HEADS, HEAD_DIM),
            vb_all[:, -nb:].reshape(bs, nb, B_KV_HEADS, HEAD_DIM),
            u_ext[:, -(CONV_W - 1):]))

        qa, ka, va, qb, kb, vb, u = _inproj(xp, sc1, sh1, w_in_bf, tm=ROW_TILE,
                                            rows_per_batch=tp, batch_off=0)
        r3 = lambda a: a.reshape(bp, tp, a.shape[-1])
        oa = _band_attention(r3(qa), r3(ka), r3(va), bias_a_p, None, **A_BAND)
        ob = _band_attention(r3(qb), r3(kb), r3(vb), t5_p, sinks[l], **B_BAND)
        oc = _conv_tail(conv_zero, r3(u), *conv_args)
        x1p, h2, idxp, gatep = _outproj(
            oa.reshape(mp, -1), ob.reshape(mp, -1), oc.reshape(mp, -1), xp, g1, sc2, sh2,
            w_out_bf, lng1, lnb1, wr_hi, wr_lo, br, tm=OUTPROJ_TILE, rows_per_batch=tp, batch_off=0,
            h2_tail=h2s)
        states_p.append((
            r3(ka)[:, tp - A_REACH:].reshape(bp, A_REACH, A_HEADS, HEAD_DIM),
            r3(va)[:, tp - A_REACH:].reshape(bp, A_REACH, A_HEADS, HEAD_DIM),
            r3(kb)[:, tp - B_WINDOW:].reshape(bp, B_WINDOW, B_KV_HEADS, HEAD_DIM),
            r3(vb)[:, tp - B_WINDOW:].reshape(bp, B_WINDOW, B_KV_HEADS, HEAD_DIM),
            r3(u)[:, tp - (CONV_W - 1):]))

        top_idx = jnp.concatenate([idxp[:, :TOP_K], idxs[:, :TOP_K]], axis=0)
        block_e, n_used, idx_flat, n_rows = _routing_tables(top_idx, n_tok)
        picked = _moe(block_e, n_used, idx_flat, h2, w_e_in, b_e_in, w_e_out, b_e_out,
                      n_rows=n_rows, layer=l)
        xp = _combine(picked, gatep, x1p, g2, lng2, lnb2, tm=ROW_TILE, rows_per_batch=tp,
                      batch_off=0, row_off=0, n_tok=n_tok)
        xs = _combine(picked, gates, x1s, per_row(g2), lng2, lnb2, tm=ms, rows_per_batch=ms,
                      batch_off=0, row_off=mp, n_tok=n_tok)

    a_k_p, a_v_p, b_k_p, b_v_p, conv_p = (jnp.stack(z) for z in zip(*states_p))
    a_k_s, a_v_s, b_k_s, b_v_s, conv_s = (jnp.stack(z) for z in zip(*states_s))
    return (xp.reshape(bp, tp, d), xs.reshape(bs, ts, d), a_k_p, a_v_p, b_k_p, b_v_p, conv_p,
            a_k_s, a_v_s, b_k_s, b_v_s, conv_s)
```

```python
import functools
import math

import jax
import jax.numpy as jnp
import numpy as np
from jax import lax
from jax.experimental import pallas as pl
from jax.experimental.pallas import tpu as pltpu

F32 = jnp.float32
BF16 = jnp.bfloat16

D_MODEL = 1024
DEPTH = 2
CHUNK = 64
HEAD_DIM = 64
ATTN_SCALE = HEAD_DIM ** -0.5
A_HEADS = 4
A_WIDTH = A_HEADS * HEAD_DIM
A_PREV_CHUNKS = 8
A_REACH = A_PREV_CHUNKS * CHUNK
A_REL_CLIP = 128
B_HEADS = 8
B_KV_HEADS = 2
B_GROUP = B_HEADS // B_KV_HEADS
B_WIDTH = B_HEADS * HEAD_DIM
B_KV_WIDTH = B_KV_HEADS * HEAD_DIM
B_WINDOW = 128
B_PREV_CHUNKS = B_WINDOW // CHUNK
T5_BUCKETS = 32
T5_MAX_DISTANCE = 128
C_CH = D_MODEL // 4
CONV_W = 31
CONV_HALO = 32
N_EXPERTS = 32
TOP_K = 4
D_FF = D_MODEL
SWIGLU_LIMIT = 7.0
SWIGLU_ALPHA = 1.702
DEEPNORM_ALPHA = (2 * DEPTH) ** 0.25
LN_EPS = 1e-5
NEG_INF = -1e30

LANES = 128
ROW_TILE = 512
OUTPROJ_TILE = 512
SAMPLE_SEQS = 4
MOE_TILE = 256
IDX_STRIDE = 1024
MOE_IDX_SLOTS = 4
A_BAND = dict(g_chunks=4, n_prev=A_PREV_CHUNKS, n_kv=A_HEADS, group=1)
B_BAND = dict(g_chunks=2, n_prev=B_PREV_CHUNKS, n_kv=B_KV_HEADS, group=B_GROUP)
VMEM_LIMIT = 56 * 1024 * 1024


def _cparams(n_axes=1, vmem=None):
    return pltpu.CompilerParams(dimension_semantics=("arbitrary",) * n_axes,
                                vmem_limit_bytes=vmem)


def _dot(a, b):
    return jnp.dot(a, b, preferred_element_type=F32)


def _layer_norm(z, g, b):
    mu = jnp.mean(z, axis=-1, keepdims=True)
    d = z - mu
    var = jnp.mean(d * d, axis=-1, keepdims=True)
    return d * lax.rsqrt(var + LN_EPS) * g + b


SUBLANES = 8
ROW_PIECES = D_MODEL // LANES


def _store_token_tiles(ref, val):
    n = val.shape[0]
    for c in range(ROW_PIECES):
        ref[pl.ds(c, n, stride=ROW_PIECES), :] = val[:, c * LANES:(c + 1) * LANES]


def _load_token_tiles(ref, n):
    return jnp.concatenate(
        [ref[pl.ds(c, n, stride=ROW_PIECES), :] for c in range(ROW_PIECES)], axis=1)


def _split_bf16(a):
    hi = a.astype(BF16)
    lo = (a - hi.astype(F32)).astype(BF16)
    return hi, lo


def _adaln_kernel(c_ref, w_ref, b_ref, o_ref):
    c = c_ref[...]
    a_hi, a_lo = _split_bf16(c * jax.nn.sigmoid(c))
    w_hi, w_lo = _split_bf16(w_ref[...])
    o_ref[...] = _dot(a_hi, w_hi) + _dot(a_lo, w_hi) + _dot(a_hi, w_lo) + b_ref[...]


def _adaln(c_all, w_all, b_all, layer):
    nb, d = c_all.shape
    n = w_all.shape[2]
    tn = 1536
    return pl.pallas_call(
        _adaln_kernel,
        out_shape=jax.ShapeDtypeStruct((nb, n), F32),
        grid=(n // tn,),
        in_specs=[pl.BlockSpec((nb, d), lambda j: (0, 0)),
                  pl.BlockSpec((None, d, tn), lambda j: (layer, 0, j)),
                  pl.BlockSpec((None, 1, tn), lambda j: (layer, 0, j))],
        out_specs=pl.BlockSpec((nb, tn), lambda j: (0, j)),
        compiler_params=_cparams(1, VMEM_LIMIT),
        name="adaln",
    )(c_all, w_all, b_all.reshape(b_all.shape[0], 1, n))


_QA = (0, A_WIDTH)
_KA = (A_WIDTH, 2 * A_WIDTH)
_VA = (2 * A_WIDTH, 3 * A_WIDTH)
_QB = (3 * A_WIDTH, 3 * A_WIDTH + B_WIDTH)
_KB = (_QB[1], _QB[1] + B_KV_WIDTH)
_VB = (_KB[1], _KB[1] + B_KV_WIDTH)
_GA = (_VB[1], _VB[1] + C_CH)
_GG = (_GA[1], _GA[1] + C_CH)


def _inproj_kernel(x_ref, sc_ref, sh_ref, w_ref,
                   qa_ref, ka_ref, va_ref, qb_ref, kb_ref, vb_ref, u_ref):
    h = (x_ref[...] * (1.0 + sc_ref[...]) + sh_ref[...]).astype(BF16)

    def proj(cols):
        return _dot(h, w_ref[:, cols[0]:cols[1]])

    qa_ref[...] = (proj(_QA) * ATTN_SCALE).astype(BF16)
    ka_ref[...] = proj(_KA)
    va_ref[...] = proj(_VA)
    qb_ref[...] = (proj(_QB) * ATTN_SCALE).astype(BF16)
    kb_ref[...] = proj(_KB)
    vb_ref[...] = proj(_VB)
    u_ref[...] = proj(_GA) * jax.nn.sigmoid(proj(_GG))


def _mod_spec(mod, tm, d, bpb, batch_off, last=None):
    blk = (lambda i: i) if last is None else (lambda i: jnp.minimum(i, last))
    if mod.ndim == 2:
        return pl.BlockSpec((tm, d), lambda i: (blk(i), 0))
    return pl.BlockSpec((None, 1, d), lambda i: (blk(i) // bpb + batch_off, 0, 0))


def _inproj(x, sc, sh, w_bf, *, tm, rows_per_batch, batch_off):
    m, d = x.shape
    bpb = rows_per_batch // tm
    mod_spec = _mod_spec(sc, tm, d, bpb, batch_off)
    widths = (A_WIDTH, A_WIDTH, A_WIDTH, B_WIDTH, B_KV_WIDTH, B_KV_WIDTH, C_CH)
    dtypes = (BF16, F32, F32, BF16, F32, F32, F32)
    return pl.pallas_call(
        _inproj_kernel,
        out_shape=[jax.ShapeDtypeStruct((m, w), dt) for w, dt in zip(widths, dtypes)],
        grid=(m // tm,),
        in_specs=[pl.BlockSpec((tm, d), lambda i: (i, 0)), mod_spec, mod_spec,
                  pl.BlockSpec(w_bf.shape, lambda i: (0, 0))],
        out_specs=[pl.BlockSpec((tm, w), lambda i: (i, 0)) for w in widths],
        compiler_params=_cparams(1, VMEM_LIMIT),
        name="inproj",
    )(x, sc, sh, w_bf)


def _cache_attn_kernel(*refs, n_seq, n_heads, group, use_sink):
    if use_sink:
        q_ref, k_ref, v_ref, bias_ref, sink_ref, o_ref = refs
    else:
        q_ref, k_ref, v_ref, bias_ref, o_ref = refs
    for b in range(n_seq):
        qt = q_ref[b]
        kt = k_ref[b].astype(BF16)
        vt = v_ref[b].astype(BF16)
        outs = []
        for h in range(n_heads):
            n = h // group
            qh = qt[:, h * HEAD_DIM:(h + 1) * HEAD_DIM]
            kh = kt[:, n * HEAD_DIM:(n + 1) * HEAD_DIM]
            vh = vt[:, n * HEAD_DIM:(n + 1) * HEAD_DIM]
            s = lax.dot_general(qh, kh, (((1,), (1,)), ((), ())),
                                preferred_element_type=F32)
            s = s + bias_ref[h]
            m = jnp.max(s, axis=-1, keepdims=True)
            if use_sink:
                m = jnp.maximum(m, sink_ref[h])
            e = jnp.exp(s - m)
            den = jnp.sum(e, axis=-1, keepdims=True)
            if use_sink:
                den = den + jnp.exp(sink_ref[h] - m)
            outs.append(_dot(e.astype(BF16), vh) / den)
        o_ref[b] = jnp.concatenate(outs, axis=-1).astype(o_ref.dtype)


def _cache_attention(q, k, v, bias, sinks, *, n_heads, group, n_seq):
    b, t, qw = q.shape
    tk, kw = k.shape[1], k.shape[2]
    use_sink = sinks is not None
    kern = functools.partial(_cache_attn_kernel, n_seq=n_seq, n_heads=n_heads, group=group,
                             use_sink=use_sink)
    in_specs = [pl.BlockSpec((n_seq, t, qw), lambda i: (i, 0, 0)),
                pl.BlockSpec((n_seq, tk, kw), lambda i: (i, 0, 0)),
                pl.BlockSpec((n_seq, tk, kw), lambda i: (i, 0, 0)),
                pl.BlockSpec(bias.shape, lambda i: (0, 0, 0))]
    args = [q, k, v, bias]
    if use_sink:
        in_specs.append(pl.BlockSpec(memory_space=pltpu.SMEM))
        args.append(sinks)
    return pl.pallas_call(
        kern,
        out_shape=jax.ShapeDtypeStruct((b, t, qw), BF16),
        grid=(b // n_seq,),
        in_specs=in_specs,
        out_specs=pl.BlockSpec((n_seq, t, qw), lambda i: (i, 0, 0)),
        compiler_params=_cparams(1, VMEM_LIMIT),
        name="cache_attn_sink" if use_sink else "cache_attn",
    )(*args)


def _band_attn_kernel(*refs, t, g_chunks, n_prev, n_kv, group, use_sink):
    if use_sink:
        q_ref, k_ref, v_ref, bias_ref, sink_ref, o_ref, qs_ref, kp_ref, vp_ref = refs
    else:
        q_ref, k_ref, v_ref, bias_ref, o_ref, qs_ref, kp_ref, vp_ref = refs
    pad = n_prev * CHUNK
    gq = g_chunks * CHUNK
    u = pad + gq
    m = group * gq
    hd = HEAD_DIM
    for n in range(n_kv):
        zeros = jnp.zeros((pad, hd), BF16)
        kp_ref[n, 0:pad, :] = zeros
        vp_ref[n, 0:pad, :] = zeros
        kp_ref[n, pad:pad + t, :] = k_ref[:, n * hd:(n + 1) * hd].astype(BF16)
        vp_ref[n, pad:pad + t, :] = v_ref[:, n * hd:(n + 1) * hd].astype(BF16)
    for h in range(n_kv * group):
        qs_ref[h] = q_ref[:, h * hd:(h + 1) * hd]
    key_pos = lax.broadcasted_iota(jnp.int32, (m, u), 1)
    row = lax.broadcasted_iota(jnp.int32, (m, 1), 0)

    def body(g, carry):
        q0 = pl.multiple_of(g * gq, gq)
        valid = key_pos >= pad - q0
        outs = []
        for n in range(n_kv):
            qstk = jnp.concatenate(
                [qs_ref[n * group + j, pl.ds(q0, gq), :] for j in range(group)], axis=0)
            kt = kp_ref[n, pl.ds(q0, u), :]
            vt = vp_ref[n, pl.ds(q0, u), :]
            s = lax.dot_general(qstk, kt, (((1,), (1,)), ((), ())),
                                preferred_element_type=F32)
            s = jnp.where(valid, s + bias_ref[n], NEG_INF)
            mx = jnp.max(s, axis=-1, keepdims=True)
            if use_sink:
                sink = jnp.full((m, 1), sink_ref[n * group], F32)
                for j in range(1, group):
                    sink = jnp.where(row >= j * gq, sink_ref[n * group + j], sink)
                mx = jnp.maximum(mx, sink)
            e = jnp.exp(s - mx)
            den = jnp.sum(e, axis=-1, keepdims=True)
            if use_sink:
                den = den + jnp.exp(sink - mx)
            o = _dot(e.astype(BF16), vt) / den
            outs.extend(o[j * gq:(j + 1) * gq, :] for j in range(group))
        o_ref[pl.ds(q0, gq), :] = jnp.concatenate(outs, axis=-1).astype(o_ref.dtype)
        return carry
    lax.fori_loop(0, t // gq, body, 0)


def _band_bias(head_bias, *, g_chunks, n_prev, n_kv, group):
    pad, gq = n_prev * CHUNK, g_chunks * CHUNK
    u = pad + gq
    r = np.arange(gq)[:, None]
    kk = np.arange(u)[None, :]
    lo = (r // CHUNK) * CHUNK
    in_band = (kk >= lo) & (kk < lo + pad + CHUNK)
    tile = jnp.where(in_band[None], _rel_bias_tile(head_bias, gq, u, pad), NEG_INF)
    return tile.reshape(n_kv, group * gq, u)


def _rel_bias_tile(head_bias, rows, cols, pad):
    n_off = rows + cols - 1
    vec = head_bias(np.arange(n_off) - pad - (rows - 1))
    h = vec.shape[0]
    padded = jnp.concatenate([vec, jnp.zeros((h, 1), vec.dtype)], axis=1)
    skew = jnp.tile(padded, (1, rows))[:, :rows * n_off].reshape(h, rows, n_off)
    return skew[:, :, rows - 1:rows - 1 + cols]


def _band_attention(q, k, v, bias, sinks, *, g_chunks, n_prev, n_kv, group):
    b, t, qw = q.shape
    kw = k.shape[2]
    pad = n_prev * CHUNK
    use_sink = sinks is not None
    kern = functools.partial(_band_attn_kernel, t=t, g_chunks=g_chunks, n_prev=n_prev,
                             n_kv=n_kv, group=group, use_sink=use_sink)
    in_specs = [pl.BlockSpec((None, t, qw), lambda i: (i, 0, 0)),
                pl.BlockSpec((None, t, kw), lambda i: (i, 0, 0)),
                pl.BlockSpec((None, t, kw), lambda i: (i, 0, 0)),
                pl.BlockSpec(bias.shape, lambda i: (0, 0, 0))]
    args = [q, k, v, bias]
    if use_sink:
        in_specs.append(pl.BlockSpec(memory_space=pltpu.SMEM))
        args.append(sinks)
    return pl.pallas_call(
        kern,
        out_shape=jax.ShapeDtypeStruct((b, t, qw), BF16),
        grid=(b,),
        in_specs=in_specs,
        out_specs=pl.BlockSpec((None, t, qw), lambda i: (i, 0, 0)),
        scratch_shapes=[pltpu.VMEM((n_kv * group, t, HEAD_DIM), BF16),
                        pltpu.VMEM((n_kv, pad + t, HEAD_DIM), BF16),
                        pltpu.VMEM((n_kv, pad + t, HEAD_DIM), BF16)],
        compiler_params=_cparams(1, VMEM_LIMIT),
        name="band_attn_sink" if use_sink else "band_attn",
    )(*args)


def _conv_kernel(prev_ref, u_ref, w_ref, cb_ref, g_ref, b_ref, o_ref, up_ref, sh_ref, *, t, tt):
    up_ref[0:CONV_HALO, :] = prev_ref[...]
    up_ref[CONV_HALO:CONV_HALO + t, :] = u_ref[...]
    lead = CONV_HALO - (CONV_W - 1)

    def tile(t0):
        win = up_ref[pl.ds(t0, tt + CONV_HALO), :]
        acc = jnp.zeros((tt, C_CH), F32)
        for phase in range(SUBLANES):
            offs = [j + lead for j in range(CONV_W) if (j + lead) % SUBLANES == phase]
            if not offs:
                continue
            span = max(offs) - phase + tt
            if phase:
                sh_ref[phase, 0:span, :] = win[phase:phase + span, :]
            for off in offs:
                a8 = off - phase
                rows = sh_ref[phase, a8:a8 + tt, :] if phase else win[a8:a8 + tt, :]
                acc = acc + rows * w_ref[off - lead:off - lead + 1, :]
        y = _layer_norm(acc + cb_ref[...], g_ref[...], b_ref[...])
        o_ref[pl.ds(t0, tt), :] = (y * jax.nn.sigmoid(y)).astype(o_ref.dtype)

    if t == tt:
        tile(0)
    else:
        def body(i, carry):
            tile(pl.multiple_of(i * tt, tt))
            return carry
        lax.fori_loop(0, t // tt, body, 0)


def _conv_tail(prev, u, conv_w, conv_b, ln_g, ln_b):
    b, t, c = u.shape
    tt = min(t, 128)
    vec = lambda a: a.reshape(1, c)
    vspec = pl.BlockSpec((1, c), lambda i: (0, 0))
    return pl.pallas_call(
        functools.partial(_conv_kernel, t=t, tt=tt),
        out_shape=jax.ShapeDtypeStruct((b, t, c), BF16),
        grid=(b,),
        in_specs=[pl.BlockSpec((None, CONV_HALO, c), lambda i: (i, 0, 0)),
                  pl.BlockSpec((None, t, c), lambda i: (i, 0, 0)),
                  pl.BlockSpec((CONV_W, c), lambda i: (0, 0)),
                  vspec, vspec, vspec],
        out_specs=pl.BlockSpec((None, t, c), lambda i: (i, 0, 0)),
        scratch_shapes=[pltpu.VMEM((CONV_HALO + t, c), F32),
                        pltpu.VMEM((SUBLANES, tt + CONV_HALO, c), F32)],
        compiler_params=_cparams(1, VMEM_LIMIT),
        name="conv_tail",
    )(prev, u, conv_w, vec(conv_b), vec(ln_g), vec(ln_b))


def _outproj_tail_kernel(*refs, n_blk):
    tail_ref, h2_ref = refs[13], refs[15]
    i = pl.program_id(0)

    @pl.when(i < n_blk)
    def _():
        _outproj_kernel(*refs[:13], *refs[14:])

    @pl.when(i >= n_blk)
    def _():
        h2_ref[...] = tail_ref[...]


def _outproj_kernel(oa_ref, ob_ref, oc_ref, x_ref, g1_ref, sc2_ref, sh2_ref, wo_ref,
                    lng_ref, lnb_ref, wr_hi_ref, wr_lo_ref, br_ref,
                    x1_ref, h2_ref, idx_ref, gate_ref):
    mix = (_dot(oa_ref[...], wo_ref[0:A_WIDTH, :])
           + _dot(ob_ref[...], wo_ref[A_WIDTH:A_WIDTH + B_WIDTH, :])
           + _dot(oc_ref[...], wo_ref[A_WIDTH + B_WIDTH:, :]))
    x1 = _layer_norm(DEEPNORM_ALPHA * x_ref[...] + g1_ref[...] * mix, lng_ref[...], lnb_ref[...])
    x1_ref[...] = x1
    h2 = x1 * (1.0 + sc2_ref[...]) + sh2_ref[...]
    _store_token_tiles(h2_ref, h2)
    h_hi, h_lo = _split_bf16(h2)
    logits = (_dot(h_hi, wr_hi_ref[...]) + _dot(h_lo, wr_hi_ref[...])
              + _dot(h_hi, wr_lo_ref[...]) + br_ref[...])
    lane = lax.broadcasted_iota(jnp.int32, logits.shape, 1)
    lane_f = lane.astype(F32)
    cur = jnp.where(lane < N_EXPERTS, logits, NEG_INF)
    vals, idxs = [], []
    for _ in range(TOP_K):
        m = jnp.max(cur, axis=-1, keepdims=True)
        i = jnp.min(jnp.where(cur == m, lane_f, float(LANES)), axis=-1, keepdims=True)
        vals.append(m)
        idxs.append(i)
        cur = jnp.where(lane_f == i, NEG_INF, cur)
    es = [jnp.exp(v - vals[0]) for v in vals]
    den = es[0] + es[1] + es[2] + es[3]
    idx_out = jnp.zeros(logits.shape, jnp.int32)
    gate_out = jnp.zeros(logits.shape, F32)
    for k in range(TOP_K):
        idx_out = jnp.where(lane == k, idxs[k].astype(jnp.int32), idx_out)
        gate_out = jnp.where(lane == k, es[k] / den, gate_out)
    idx_ref[...] = idx_out
    gate_ref[...] = gate_out


def _outproj(oa, ob, oc, x, g1, sc2, sh2, wo_bf, ln_g, ln_b, wr_hi, wr_lo, br,
             *, tm, rows_per_batch, batch_off, h2_tail=None):
    m, d = x.shape
    bpb = rows_per_batch // tm
    n_blk = m // tm
    last = n_blk - 1
    row = lambda w: pl.BlockSpec((tm, w), lambda i: (jnp.minimum(i, last), 0))
    mod_spec = _mod_spec(g1, tm, d, bpb, batch_off, last)
    const = lambda a: pl.BlockSpec(a.shape, lambda i: (0, 0))
    in_specs = [row(A_WIDTH), row(B_WIDTH), row(C_CH), row(d), mod_spec, mod_spec, mod_spec,
                const(wo_bf), const(ln_g), const(ln_b), const(wr_hi), const(wr_lo), const(br)]
    args = [oa, ob, oc, x, g1, sc2, sh2, wo_bf, ln_g, ln_b, wr_hi, wr_lo, br]
    if h2_tail is None:
        kern, steps, h2_rows = _outproj_kernel, n_blk, m
    else:
        tail_tokens = h2_tail.shape[0] // ROW_PIECES
        assert h2_tail.shape[1] == LANES and tail_tokens % tm == 0
        kern = functools.partial(_outproj_tail_kernel, n_blk=n_blk)
        steps, h2_rows = n_blk + tail_tokens // tm, m + tail_tokens
        in_specs.append(pl.BlockSpec((tm * ROW_PIECES, LANES),
                                     lambda i: (jnp.maximum(i - n_blk, 0), 0)))
        args.append(h2_tail)
    return pl.pallas_call(
        kern,
        out_shape=[jax.ShapeDtypeStruct((m, d), F32),
                   jax.ShapeDtypeStruct((h2_rows * ROW_PIECES, LANES), F32),
                   jax.ShapeDtypeStruct((m, LANES), jnp.int32),
                   jax.ShapeDtypeStruct((m, LANES), F32)],
        grid=(steps,),
        in_specs=in_specs,
        out_specs=[row(d), pl.BlockSpec((tm * ROW_PIECES, LANES), lambda i: (i, 0)),
                   row(LANES), row(LANES)],
        compiler_params=_cparams(1, VMEM_LIMIT),
        name="outproj_route",
    )(*args)


def _moe_kernel(be_ref, nu_ref, idx_hbm, h2_hbm, w1_ref, b1_ref, w2_ref, b2_ref,
                picked_hbm, idx_smem, xbuf, obuf, w1b, w2b, sem_idx, sem_g, sem_s, *, tm):
    i = pl.program_id(0)
    nu = nu_ref[0]
    rp = ROW_PIECES

    def table_copy(blk, slot):
        return pltpu.make_async_copy(
            idx_hbm.at[pl.ds(pl.multiple_of(blk * IDX_STRIDE, IDX_STRIDE), IDX_STRIDE)],
            idx_smem.at[pl.ds(pl.multiple_of(slot * IDX_STRIDE, IDX_STRIDE), IDX_STRIDE)],
            sem_idx.at[slot])

    def row_loop(body, static_rows):
        if static_rows:
            for r in range(tm):
                body(r)
        else:
            lax.fori_loop(0, tm, lambda r, c: (body(r), c)[1], 0)

    def start_gathers(tslot, bslot, static_rows=True):
        base = tslot * IDX_STRIDE

        def body(r):
            src = pl.multiple_of(idx_smem[base + r], rp)
            row0 = r * rp if isinstance(r, int) else pl.multiple_of(r * rp, rp)
            pltpu.make_async_copy(h2_hbm.at[pl.ds(src, rp), :],
                                  xbuf.at[bslot, pl.ds(row0, rp), :], sem_g.at[bslot]).start()
        row_loop(body, static_rows)

    def wait_gathers(bslot):
        pltpu.make_async_copy(h2_hbm.at[pl.ds(0, tm * rp), :], xbuf.at[bslot],
                              sem_g.at[bslot]).wait()

    def start_scatters(tslot, bslot):
        base = tslot * IDX_STRIDE + tm

        def body(r):
            dst = pl.multiple_of(idx_smem[base + r], rp)
            pltpu.make_async_copy(obuf.at[bslot, pl.ds(r * rp, rp), :],
                                  picked_hbm.at[pl.ds(dst, rp), :], sem_s.at[bslot]).start()
        row_loop(body, True)

    def wait_scatters(bslot):
        pltpu.make_async_copy(obuf.at[bslot], picked_hbm.at[pl.ds(0, tm * rp), :],
                              sem_s.at[bslot]).wait()

    @pl.when(i < nu)
    def _():
        bslot = i % 2
        tslot = i % MOE_IDX_SLOTS

        @pl.when(i == 0)
        def _():
            first = table_copy(0, 0)
            first.start()
            first.wait()
            start_gathers(0, 0, static_rows=False)

            @pl.when(nu > 1)
            def _():
                table_copy(1, 1).start()

        @pl.when(i + 2 < nu)
        def _():
            table_copy(i + 2, (i + 2) % MOE_IDX_SLOTS).start()

        @pl.when(i + 1 < nu)
        def _():
            nslot = (i + 1) % MOE_IDX_SLOTS
            table_copy(i + 1, nslot).wait()
            start_gathers(nslot, 1 - bslot)

        @pl.when(jnp.logical_or(i == 0, be_ref[i] != be_ref[jnp.maximum(i - 1, 0)]))
        def _():
            w1b[...] = w1_ref[...].astype(BF16)
            w2b[...] = w2_ref[...].astype(BF16)

        wait_gathers(bslot)

        @pl.when(i >= 2)
        def _():
            wait_scatters(bslot)

        x = _load_token_tiles(xbuf.at[bslot], tm).astype(BF16)
        x_glu = jnp.minimum(_dot(x, w1b[:, 0:D_FF]) + b1_ref[:, 0:D_FF], SWIGLU_LIMIT)
        x_lin = jnp.clip(_dot(x, w1b[:, D_FF:]) + b1_ref[:, D_FF:], -SWIGLU_LIMIT, SWIGLU_LIMIT)
        act = x_glu * jax.nn.sigmoid(SWIGLU_ALPHA * x_glu) * (x_lin + 1.0)
        _store_token_tiles(obuf.at[bslot], _dot(act.astype(BF16), w2b[...]) + b2_ref[...])
        start_scatters(tslot, bslot)

        @pl.when(i == nu - 1)
        def _():
            @pl.when(i >= 1)
            def _():
                wait_scatters(1 - bslot)
            wait_scatters(bslot)

    @pl.when(i >= nu)
    def _():
        @pl.when(i == nu)
        def _():
            obuf[0] = jnp.zeros(obuf.shape[1:], F32)
        cp = pltpu.make_async_copy(
            obuf.at[0],
            picked_hbm.at[pl.ds(pl.multiple_of(i * (tm * rp), tm * rp), tm * rp), :], sem_s.at[0])
        cp.start()
        cp.wait()


def _moe(block_e, n_used, idx_flat, h2, w1, b1, w2, b2, *, n_rows, layer):
    tm = MOE_TILE
    n_blocks = block_e.shape[0]
    d = D_MODEL
    tile_rows = tm * ROW_PIECES
    expert = lambda i, be, nu: (layer, be[i], 0, 0)
    grid_spec = pltpu.PrefetchScalarGridSpec(
        num_scalar_prefetch=2,
        grid=(n_blocks,),
        in_specs=[pl.BlockSpec(memory_space=pl.ANY),
                  pl.BlockSpec(memory_space=pl.ANY),
                  pl.BlockSpec((None, None, d, 2 * D_FF), expert),
                  pl.BlockSpec((None, None, 1, 2 * D_FF), expert),
                  pl.BlockSpec((None, None, D_FF, d), expert),
                  pl.BlockSpec((None, None, 1, d), expert)],
        out_specs=pl.BlockSpec(memory_space=pl.ANY),
        scratch_shapes=[pltpu.SMEM((MOE_IDX_SLOTS * IDX_STRIDE,), jnp.int32),
                        pltpu.VMEM((2, tile_rows, LANES), F32),
                        pltpu.VMEM((2, tile_rows, LANES), F32),
                        pltpu.VMEM((d, 2 * D_FF), BF16), pltpu.VMEM((D_FF, d), BF16),
                        pltpu.SemaphoreType.DMA((MOE_IDX_SLOTS,)),
                        pltpu.SemaphoreType.DMA((2,)), pltpu.SemaphoreType.DMA((2,))])
    return pl.pallas_call(
        functools.partial(_moe_kernel, tm=tm),
        out_shape=jax.ShapeDtypeStruct((n_rows * ROW_PIECES, LANES), F32),
        grid_spec=grid_spec,
        compiler_params=_cparams(1, VMEM_LIMIT),
        name="moe_experts",
    )(block_e, n_used, idx_flat, h2, w1, b1.reshape(b1.shape[0], N_EXPERTS, 1, -1), w2,
      b2.reshape(b2.shape[0], N_EXPERTS, 1, -1))


def _routing_tables(top_idx, n_tok):
    tm = MOE_TILE
    n_assign = n_tok * TOP_K
    n_blocks = -(-n_assign // tm) + N_EXPERTS
    n_rows = n_blocks * tm
    i32 = jnp.int32
    flat_e = top_idx.reshape(n_assign)
    key_bits = (n_assign - 1).bit_length()
    assert N_EXPERTS << key_bits < 2 ** 31
    keys = jnp.sort(flat_e * (1 << key_bits) + jnp.arange(n_assign, dtype=i32))
    order = keys & ((1 << key_bits) - 1)
    experts = jnp.arange(N_EXPERTS, dtype=i32)
    counts = jnp.sum((flat_e[:, None] == experts[None, :]).astype(i32), axis=0)
    padded = (counts + tm - 1) // tm * tm
    pad_end = jnp.cumsum(padded)
    pad_start = pad_end - padded
    grp_start = jnp.cumsum(counts) - counts
    n_used = pad_end[-1] // tm
    blk = jnp.arange(n_blocks, dtype=i32)
    used = blk < n_used
    expert_at = lambda start: jnp.minimum(
        jnp.sum((pad_end[None, :] <= start[:, None]).astype(i32), axis=1), N_EXPERTS - 1)
    e_blk = expert_at(blk * tm)
    e_last = expert_at(((n_used - 1) * tm).reshape(1))[0]
    block_e = jnp.where(used, e_blk, e_last)
    row = blk[:, None] * tm + jnp.arange(tm, dtype=i32)[None, :]
    off = row - pad_start[e_blk][:, None]
    cnt = counts[e_blk][:, None]
    grp = grp_start[e_blk][:, None]
    valid = used[:, None] & (off < cnt)
    a = order[jnp.clip(grp + off, 0, n_assign - 1)]
    tok = a // TOP_K
    row_tok = jnp.where(valid, tok, 0)
    real_before = jnp.where(used[:, None], grp + cnt, n_assign)
    row_dst = jnp.where(valid, (a % TOP_K) * n_tok + tok, n_assign + row - real_before)
    idx = jnp.concatenate(
        [row_tok * ROW_PIECES, row_dst * ROW_PIECES,
         jnp.zeros((n_blocks, IDX_STRIDE - 2 * tm), i32)], axis=1)
    return block_e, n_used.reshape(1).astype(i32), idx.reshape(-1), n_rows


def _combine_kernel(p0_ref, p1_ref, p2_ref, p3_ref, gate_ref, x1_ref, g2_ref,
                    lng_ref, lnb_ref, o_ref):
    gate = gate_ref[...]
    n = gate.shape[0]
    y = (gate[:, 0:1] * _load_token_tiles(p0_ref, n) + gate[:, 1:2] * _load_token_tiles(p1_ref, n)
         + gate[:, 2:3] * _load_token_tiles(p2_ref, n) + gate[:, 3:4] * _load_token_tiles(p3_ref, n))
    o_ref[...] = _layer_norm(DEEPNORM_ALPHA * x1_ref[...] + g2_ref[...] * y,
                             lng_ref[...], lnb_ref[...])


def _combine(picked, gate, x1, g2, ln_g, ln_b, *, tm, rows_per_batch, batch_off, row_off, n_tok):
    m, d = x1.shape
    bpb = rows_per_batch // tm
    blk_off = row_off // tm
    k_stride = n_tok // tm
    pspec = lambda k: pl.BlockSpec((tm * ROW_PIECES, LANES),
                                   lambda i: (i + blk_off + k * k_stride, 0))
    row = lambda w: pl.BlockSpec((tm, w), lambda i: (i, 0))
    const = lambda a: pl.BlockSpec(a.shape, lambda i: (0, 0))
    return pl.pallas_call(
        _combine_kernel,
        out_shape=jax.ShapeDtypeStruct((m, d), F32),
        grid=(m // tm,),
        in_specs=[pspec(0), pspec(1), pspec(2), pspec(3), row(LANES), row(d),
                  _mod_spec(g2, tm, d, bpb, batch_off), const(ln_g), const(ln_b)],
        out_specs=row(d),
        compiler_params=_cparams(1, VMEM_LIMIT),
        name="combine_ln",
    )(picked, picked, picked, picked, gate, x1, g2, ln_g, ln_b)


def _clipped_rel_bias(rel, table):
    idx = np.clip(rel, -A_REL_CLIP, A_REL_CLIP) + A_REL_CLIP
    return jnp.moveaxis(table[idx].astype(F32), -1, 0)


def _t5_bucket(rel):
    nb = T5_BUCKETS // 2
    max_exact = nb // 2
    n = np.abs(rel)
    nf = np.maximum(n, 1).astype(np.float32)
    large = max_exact + (np.log(nf / max_exact) / math.log(T5_MAX_DISTANCE / max_exact)
                         * (nb - max_exact)).astype(np.int32)
    large = np.minimum(large, nb - 1)
    return np.where(rel > 0, nb, 0) + np.where(n < max_exact, n, large)


def _t5_rel_bias(rel, table):
    return jnp.moveaxis(table[_t5_bucket(rel)].astype(F32), -1, 0)


def kernel(x_prompt, x_sample, cache_a_k, cache_a_v, cache_b_k, cache_b_v, state_conv,
           c_prompt, c_sample, w_in, w_out, rel_bias_a, t5_bias, sinks, conv_w, conv_b,
           conv_ln_g, conv_ln_b, w_ada, b_ada, ln_g, ln_b, w_router, b_router,
           w_e_in, b_e_in, w_e_out, b_e_out):
    bp, tp, d = x_prompt.shape
    bs, ts, _ = x_sample.shape
    mp, ms = bp * tp, bs * ts
    n_tok = mp + ms
    na, nb = cache_a_k.shape[2], cache_b_k.shape[2]
    assert tp % ROW_TILE == 0 and mp % ts == 0 and ts >= CONV_W - 1 and tp >= A_REACH
    assert ms % OUTPROJ_TILE == 0 and tp % OUTPROJ_TILE == 0
    assert mp % ms == 0 and bs % SAMPLE_SEQS == 0

    xp = x_prompt.reshape(mp, d)
    xs = x_sample.reshape(ms, d)
    c_all = jnp.concatenate([c_prompt, c_sample], axis=0)

    t5_p = _band_bias(lambda rel: _t5_rel_bias(rel, t5_bias), **B_BAND)
    t5_s = _rel_bias_tile(lambda rel: _t5_rel_bias(rel, t5_bias), ts, nb + ts, nb)
    conv_zero = jnp.zeros((bp, CONV_HALO, C_CH), F32)

    states_p, states_s = [], []
    for l in range(DEPTH):
        mod = _adaln(c_all, w_ada, b_ada, l)
        sh1, sc1, g1, sh2, sc2, g2 = (mod[:, j * d:(j + 1) * d].reshape(bp + bs, 1, d)
                                      for j in range(6))
        w_in_bf = w_in[l].astype(BF16)
        w_out_bf = w_out[l].astype(BF16)
        wr = jnp.pad(w_router[l], ((0, 0), (0, LANES - N_EXPERTS)))
        wr_hi = wr.astype(BF16)
        wr_lo = (wr - wr_hi.astype(F32)).astype(BF16)
        br = jnp.pad(b_router[l], (0, LANES - N_EXPERTS)).reshape(1, LANES)
        lng1, lnb1 = ln_g[l, 0].reshape(1, d), ln_b[l, 0].reshape(1, d)
        lng2, lnb2 = ln_g[l, 1].reshape(1, d), ln_b[l, 1].reshape(1, d)
        bias_a_p = _band_bias(lambda rel: _clipped_rel_bias(rel, rel_bias_a[l]), **A_BAND)
        bias_a_s = _rel_bias_tile(lambda rel: _clipped_rel_bias(rel, rel_bias_a[l]),
                                  ts, na + ts, na)
        conv_args = (conv_w[l], conv_b[l], conv_ln_g[l], conv_ln_b[l])

        per_row = lambda a: jnp.broadcast_to(a[bp:], (bs, ts, d)).reshape(ms, d)
        qa, ka, va, qb, kb, vb, u = _inproj(xs, per_row(sc1), per_row(sh1), w_in_bf, tm=ms,
                                            rows_per_batch=ms, batch_off=0)
        s3 = lambda a: a.reshape(bs, ts, a.shape[-1])
        ka_all = jnp.concatenate([cache_a_k[l].reshape(bs, na, A_WIDTH), s3(ka)], axis=1)
        va_all = jnp.concatenate([cache_a_v[l].reshape(bs, na, A_WIDTH), s3(va)], axis=1)
        kb_all = jnp.concatenate([cache_b_k[l].reshape(bs, nb, B_KV_WIDTH), s3(kb)], axis=1)
        vb_all = jnp.concatenate([cache_b_v[l].reshape(bs, nb, B_KV_WIDTH), s3(vb)], axis=1)
        oa = _cache_attention(s3(qa), ka_all, va_all, bias_a_s, None,
                              n_heads=A_HEADS, group=1, n_seq=SAMPLE_SEQS)
        ob = _cache_attention(s3(qb), kb_all, vb_all, t5_s, sinks[l],
                              n_heads=B_HEADS, group=B_GROUP, n_seq=SAMPLE_SEQS)
        prev = jnp.pad(state_conv[l], ((0, 0), (CONV_HALO - (CONV_W - 1), 0), (0, 0)))
        oc = _conv_tail(prev, s3(u), *conv_args)
        x1s, h2s, idxs, gates = _outproj(
            oa.reshape(ms, -1), ob.reshape(ms, -1), oc.reshape(ms, -1), xs,
            per_row(g1), per_row(sc2), per_row(sh2),
            w_out_bf, lng1, lnb1, wr_hi, wr_lo, br, tm=ms, rows_per_batch=ms, batch_off=0)
        u_ext = jnp.concatenate([state_conv[l], s3(u)], axis=1)
        states_s.append((
            ka_all[:, -na:].reshape(bs, na, A_HEADS, HEAD_DIM),
            va_all[:, -na:].reshape(bs, na, A_HEADS, HEAD_DIM),
            kb_all[:, -nb:].reshape(bs, nb, B_KV_HEADS, HEAD_DIM),
            vb_all[:, -nb:].reshape(bs, nb, B_KV_HEADS, HEAD_DIM),
            u_ext[:, -(CONV_W - 1):]))

        qa, ka, va, qb, kb, vb, u = _inproj(xp, sc1, sh1, w_in_bf, tm=ROW_TILE,
                                            rows_per_batch=tp, batch_off=0)
        r3 = lambda a: a.reshape(bp, tp, a.shape[-1])
        oa = _band_attention(r3(qa), r3(ka), r3(va), bias_a_p, None, **A_BAND)
        ob = _band_attention(r3(qb), r3(kb), r3(vb), t5_p, sinks[l], **B_BAND)
        oc = _conv_tail(conv_zero, r3(u), *conv_args)
        x1p, h2, idxp, gatep = _outproj(
            oa.reshape(mp, -1), ob.reshape(mp, -1), oc.reshape(mp, -1), xp, g1, sc2, sh2,
            w_out_bf, lng1, lnb1, wr_hi, wr_lo, br, tm=OUTPROJ_TILE, rows_per_batch=tp, batch_off=0,
            h2_tail=h2s)
        states_p.append((
            r3(ka)[:, tp - A_REACH:].reshape(bp, A_REACH, A_HEADS, HEAD_DIM),
            r3(va)[:, tp - A_REACH:].reshape(bp, A_REACH, A_HEADS, HEAD_DIM),
            r3(kb)[:, tp - B_WINDOW:].reshape(bp, B_WINDOW, B_KV_HEADS, HEAD_DIM),
            r3(vb)[:, tp - B_WINDOW:].reshape(bp, B_WINDOW, B_KV_HEADS, HEAD_DIM),
            r3(u)[:, tp - (CONV_W - 1):]))

        top_idx = jnp.concatenate([idxp[:, :TOP_K], idxs[:, :TOP_K]], axis=0)
        block_e, n_used, idx_flat, n_rows = _routing_tables(top_idx, n_tok)
        picked = _moe(block_e, n_used, idx_flat, h2, w_e_in, b_e_in, w_e_out, b_e_out,
                      n_rows=n_rows, layer=l)
        xp = _combine(picked, gatep, x1p, g2, lng2, lnb2, tm=ROW_TILE, rows_per_batch=tp,
                      batch_off=0, row_off=0, n_tok=n_tok)
        xs = _combine(picked, gates, x1s, per_row(g2), lng2, lnb2, tm=ms, rows_per_batch=ms,
                      batch_off=0, row_off=mp, n_tok=n_tok)

    a_k_p, a_v_p, b_k_p, b_v_p, conv_p = (jnp.stack(z) for z in zip(*states_p))
    a_k_s, a_v_s, b_k_s, b_v_s, conv_s = (jnp.stack(z) for z in zip(*states_s))
    return (xp.reshape(bp, tp, d), xs.reshape(bs, ts, d), a_k_p, a_v_p, b_k_p, b_v_p, conv_p,
            a_k_s, a_v_s, b_k_s, b_v_s, conv_s)
```

```python
import functools
import math

import jax
import jax.numpy as jnp
import numpy as np
from jax import lax
from jax.experimental import pallas as pl
from jax.experimental.pallas import tpu as pltpu

F32 = jnp.float32
BF16 = jnp.bfloat16

D_MODEL = 1024
DEPTH = 2
CHUNK = 64
HEAD_DIM = 64
ATTN_SCALE = HEAD_DIM ** -0.5
A_HEADS = 4
A_WIDTH = A_HEADS * HEAD_DIM
A_PREV_CHUNKS = 8
A_REACH = A_PREV_CHUNKS * CHUNK
A_REL_CLIP = 128
B_HEADS = 8
B_KV_HEADS = 2
B_GROUP = B_HEADS // B_KV_HEADS
B_WIDTH = B_HEADS * HEAD_DIM
B_KV_WIDTH = B_KV_HEADS * HEAD_DIM
B_WINDOW = 128
B_PREV_CHUNKS = B_WINDOW // CHUNK
T5_BUCKETS = 32
T5_MAX_DISTANCE = 128
C_CH = D_MODEL // 4
CONV_W = 31
CONV_HALO = 32
N_EXPERTS = 32
TOP_K = 4
D_FF = D_MODEL
SWIGLU_LIMIT = 7.0
SWIGLU_ALPHA = 1.702
DEEPNORM_ALPHA = (2 * DEPTH) ** 0.25
LN_EPS = 1e-5
NEG_INF = -1e30

LANES = 128
ROW_TILE = 512
OUTPROJ_TILE = 512
SAMPLE_SEQS = 4
MOE_TILE = 256
IDX_STRIDE = 1024
MOE_IDX_SLOTS = 4
A_BAND = dict(g_chunks=4, n_prev=A_PREV_CHUNKS, n_kv=A_HEADS, group=1)
B_BAND = dict(g_chunks=2, n_prev=B_PREV_CHUNKS, n_kv=B_KV_HEADS, group=B_GROUP)
VMEM_LIMIT = 56 * 1024 * 1024


def _cparams(n_axes=1, vmem=None):
    return pltpu.CompilerParams(dimension_semantics=("arbitrary",) * n_axes,
                                vmem_limit_bytes=vmem)


def _dot(a, b):
    return jnp.dot(a, b, preferred_element_type=F32)


def _layer_norm(z, g, b):
    mu = jnp.mean(z, axis=-1, keepdims=True)
    d = z - mu
    var = jnp.mean(d * d, axis=-1, keepdims=True)
    return d * lax.rsqrt(var + LN_EPS) * g + b


SUBLANES = 8
ROW_PIECES = D_MODEL // LANES


def _store_token_tiles(ref, val):
    n = val.shape[0]
    for c in range(ROW_PIECES):
        ref[pl.ds(c, n, stride=ROW_PIECES), :] = val[:, c * LANES:(c + 1) * LANES]


def _load_token_tiles(ref, n):
    return jnp.concatenate(
        [ref[pl.ds(c, n, stride=ROW_PIECES), :] for c in range(ROW_PIECES)], axis=1)


def _split_bf16(a):
    hi = a.astype(BF16)
    lo = (a - hi.astype(F32)).astype(BF16)
    return hi, lo


def _adaln_kernel(c_ref, w_ref, b_ref, o_ref):
    c = c_ref[...]
    a_hi, a_lo = _split_bf16(c * jax.nn.sigmoid(c))
    w_hi, w_lo = _split_bf16(w_ref[...])
    o_ref[...] = _dot(a_hi, w_hi) + _dot(a_lo, w_hi) + _dot(a_hi, w_lo) + b_ref[...]


def _adaln(c_all, w_all, b_all, layer):
    nb, d = c_all.shape
    n = w_all.shape[2]
    tn = 1536
    return pl.pallas_call(
        _adaln_kernel,
        out_shape=jax.ShapeDtypeStruct((nb, n), F32),
        grid=(n // tn,),
        in_specs=[pl.BlockSpec((nb, d), lambda j: (0, 0)),
                  pl.BlockSpec((None, d, tn), lambda j: (layer, 0, j)),
                  pl.BlockSpec((None, 1, tn), lambda j: (layer, 0, j))],
        out_specs=pl.BlockSpec((nb, tn), lambda j: (0, j)),
        compiler_params=_cparams(1, VMEM_LIMIT),
        name="adaln",
    )(c_all, w_all, b_all.reshape(b_all.shape[0], 1, n))


_QA = (0, A_WIDTH)
_KA = (A_WIDTH, 2 * A_WIDTH)
_VA = (2 * A_WIDTH, 3 * A_WIDTH)
_QB = (3 * A_WIDTH, 3 * A_WIDTH + B_WIDTH)
_KB = (_QB[1], _QB[1] + B_KV_WIDTH)
_VB = (_KB[1], _KB[1] + B_KV_WIDTH)
_GA = (_VB[1], _VB[1] + C_CH)
_GG = (_GA[1], _GA[1] + C_CH)


def _inproj_kernel(x_ref, sc_ref, sh_ref, w_ref,
                   qa_ref, ka_ref, va_ref, qb_ref, kb_ref, vb_ref, u_ref):
    h = (x_ref[...] * (1.0 + sc_ref[...]) + sh_ref[...]).astype(BF16)

    def proj(cols):
        return _dot(h, w_ref[:, cols[0]:cols[1]])

    qa_ref[...] = (proj(_QA) * ATTN_SCALE).astype(BF16)
    ka_ref[...] = proj(_KA)
    va_ref[...] = proj(_VA)
    qb_ref[...] = (proj(_QB) * ATTN_SCALE).astype(BF16)
    kb_ref[...] = proj(_KB)
    vb_ref[...] = proj(_VB)
    u_ref[...] = proj(_GA) * jax.nn.sigmoid(proj(_GG))


def _mod_spec(mod, tm, d, bpb, batch_off, last=None):
    blk = (lambda i: i) if last is None else (lambda i: jnp.minimum(i, last))
    if mod.ndim == 2:
        return pl.BlockSpec((tm, d), lambda i: (blk(i), 0))
    return pl.BlockSpec((None, 1, d), lambda i: (blk(i) // bpb + batch_off, 0, 0))


def _inproj(x, sc, sh, w_bf, *, tm, rows_per_batch, batch_off):
    m, d = x.shape
    bpb = rows_per_batch // tm
    mod_spec = _mod_spec(sc, tm, d, bpb, batch_off)
    widths = (A_WIDTH, A_WIDTH, A_WIDTH, B_WIDTH, B_KV_WIDTH, B_KV_WIDTH, C_CH)
    dtypes = (BF16, F32, F32, BF16, F32, F32, F32)
    return pl.pallas_call(
        _inproj_kernel,
        out_shape=[jax.ShapeDtypeStruct((m, w), dt) for w, dt in zip(widths, dtypes)],
        grid=(m // tm,),
        in_specs=[pl.BlockSpec((tm, d), lambda i: (i, 0)), mod_spec, mod_spec,
                  pl.BlockSpec(w_bf.shape, lambda i: (0, 0))],
        out_specs=[pl.BlockSpec((tm, w), lambda i: (i, 0)) for w in widths],
        compiler_params=_cparams(1, VMEM_LIMIT),
        name="inproj",
    )(x, sc, sh, w_bf)


def _cache_attn_kernel(*refs, n_seq, n_heads, group, use_sink):
    if use_sink:
        q_ref, k_ref, v_ref, bias_ref, sink_ref, o_ref = refs
    else:
        q_ref, k_ref, v_ref, bias_ref, o_ref = refs
    for b in range(n_seq):
        qt = q_ref[b]
        kt = k_ref[b].astype(BF16)
        vt = v_ref[b].astype(BF16)
        outs = []
        for h in range(n_heads):
            n = h // group
            qh = qt[:, h * HEAD_DIM:(h + 1) * HEAD_DIM]
            kh = kt[:, n * HEAD_DIM:(n + 1) * HEAD_DIM]
            vh = vt[:, n * HEAD_DIM:(n + 1) * HEAD_DIM]
            s = lax.dot_general(qh, kh, (((1,), (1,)), ((), ())),
                                preferred_element_type=F32)
            s = s + bias_ref[h]
            m = jnp.max(s, axis=-1, keepdims=True)
            if use_sink:
                m = jnp.maximum(m, sink_ref[h])
            e = jnp.exp(s - m)
            den = jnp.sum(e, axis=-1, keepdims=True)
            if use_sink:
                den = den + jnp.exp(sink_ref[h] - m)
            outs.append(_dot(e.astype(BF16), vh) / den)
        o_ref[b] = jnp.concatenate(outs, axis=-1).astype(o_ref.dtype)


def _cache_attention(q, k, v, bias, sinks, *, n_heads, group, n_seq):
    b, t, qw = q.shape
    tk, kw = k.shape[1], k.shape[2]
    use_sink = sinks is not None
    kern = functools.partial(_cache_attn_kernel, n_seq=n_seq, n_heads=n_heads, group=group,
                             use_sink=use_sink)
    in_specs = [pl.BlockSpec((n_seq, t, qw), lambda i: (i, 0, 0)),
                pl.BlockSpec((n_seq, tk, kw), lambda i: (i, 0, 0)),
                pl.BlockSpec((n_seq, tk, kw), lambda i: (i, 0, 0)),
                pl.BlockSpec(bias.shape, lambda i: (0, 0, 0))]
    args = [q, k, v, bias]
    if use_sink:
        in_specs.append(pl.BlockSpec(memory_space=pltpu.SMEM))
        args.append(sinks)
    return pl.pallas_call(
        kern,
        out_shape=jax.ShapeDtypeStruct((b, t, qw), BF16),
        grid=(b // n_seq,),
        in_specs=in_specs,
        out_specs=pl.BlockSpec((n_seq, t, qw), lambda i: (i, 0, 0)),
        compiler_params=_cparams(1, VMEM_LIMIT),
        name="cache_attn_sink" if use_sink else "cache_attn",
    )(*args)


def _band_attn_kernel(*refs, t, g_chunks, n_prev, n_kv, group, use_sink):
    if use_sink:
        q_ref, k_ref, v_ref, bias_ref, sink_ref, o_ref, qs_ref, kp_ref, vp_ref = refs
    else:
        q_ref, k_ref, v_ref, bias_ref, o_ref, qs_ref, kp_ref, vp_ref = refs
    pad = n_prev * CHUNK
    gq = g_chunks * CHUNK
    u = pad + gq
    m = group * gq
    hd = HEAD_DIM
    for n in range(n_kv):
        zeros = jnp.zeros((pad, hd), BF16)
        kp_ref[n, 0:pad, :] = zeros
        vp_ref[n, 0:pad, :] = zeros
        kp_ref[n, pad:pad + t, :] = k_ref[:, n * hd:(n + 1) * hd].astype(BF16)
        vp_ref[n, pad:pad + t, :] = v_ref[:, n * hd:(n + 1) * hd].astype(BF16)
    for h in range(n_kv * group):
        qs_ref[h] = q_ref[:, h * hd:(h + 1) * hd]
    key_pos = lax.broadcasted_iota(jnp.int32, (m, u), 1)
    row = lax.broadcasted_iota(jnp.int32, (m, 1), 0)

    def body(g, carry):
        q0 = pl.multiple_of(g * gq, gq)
        valid = key_pos >= pad - q0
        outs = []
        for n in range(n_kv):
            qstk = jnp.concatenate(
                [qs_ref[n * group + j, pl.ds(q0, gq), :] for j in range(group)], axis=0)
            kt = kp_ref[n, pl.ds(q0, u), :]
            vt = vp_ref[n, pl.ds(q0, u), :]
            s = lax.dot_general(qstk, kt, (((1,), (1,)), ((), ())),
                                preferred_element_type=F32)
            s = jnp.where(valid, s + bias_ref[n], NEG_INF)
            mx = jnp.max(s, axis=-1, keepdims=True)
            if use_sink:
                sink = jnp.full((m, 1), sink_ref[n * group], F32)
                for j in range(1, group):
                    sink = jnp.where(row >= j * gq, sink_ref[n * group + j], sink)
                mx = jnp.maximum(mx, sink)
            e = jnp.exp(s - mx)
            den = jnp.sum(e, axis=-1, keepdims=True)
            if use_sink:
                den = den + jnp.exp(sink - mx)
            o = _dot(e.astype(BF16), vt) / den
            outs.extend(o[j * gq:(j + 1) * gq, :] for j in range(group))
        o_ref[pl.ds(q0, gq), :] = jnp.concatenate(outs, axis=-1).astype(o_ref.dtype)
        return carry
    lax.fori_loop(0, t // gq, body, 0)


def _band_bias(head_bias, *, g_chunks, n_prev, n_kv, group):
    pad, gq = n_prev * CHUNK, g_chunks * CHUNK
    u = pad + gq
    r = np.arange(gq)[:, None]
    kk = np.arange(u)[None, :]
    lo = (r // CHUNK) * CHUNK
    in_band = (kk >= lo) & (kk < lo + pad + CHUNK)
    tile = jnp.where(in_band[None], _rel_bias_tile(head_bias, gq, u, pad), NEG_INF)
    return tile.reshape(n_kv, group * gq, u)


def _rel_bias_tile(head_bias, rows, cols, pad):
    n_off = rows + cols - 1
    vec = head_bias(np.arange(n_off) - pad - (rows - 1))
    h = vec.shape[0]
    padded = jnp.concatenate([vec, jnp.zeros((h, 1), vec.dtype)], axis=1)
    skew = jnp.tile(padded, (1, rows))[:, :rows * n_off].reshape(h, rows, n_off)
    return skew[:, :, rows - 1:rows - 1 + cols]


def _band_attention(q, k, v, bias, sinks, *, g_chunks, n_prev, n_kv, group):
    b, t, qw = q.shape
    kw = k.shape[2]
    pad = n_prev * CHUNK
    use_sink = sinks is not None
    kern = functools.partial(_band_attn_kernel, t=t, g_chunks=g_chunks, n_prev=n_prev,
                             n_kv=n_kv, group=group, use_sink=use_sink)
    in_specs = [pl.BlockSpec((None, t, qw), lambda i: (i, 0, 0)),
                pl.BlockSpec((None, t, kw), lambda i: (i, 0, 0)),
                pl.BlockSpec((None, t, kw), lambda i: (i, 0, 0)),
                pl.BlockSpec(bias.shape, lambda i: (0, 0, 0))]
    args = [q, k, v, bias]
    if use_sink:
        in_specs.append(pl.BlockSpec(memory_space=pltpu.SMEM))
        args.append(sinks)
    return pl.pallas_call(
        kern,
        out_shape=jax.ShapeDtypeStruct((b, t, qw), BF16),
        grid=(b,),
        in_specs=in_specs,
        out_specs=pl.BlockSpec((None, t, qw), lambda i: (i, 0, 0)),
        scratch_shapes=[pltpu.VMEM((n_kv * group, t, HEAD_DIM), BF16),
                        pltpu.VMEM((n_kv, pad + t, HEAD_DIM), BF16),
                        pltpu.VMEM((n_kv, pad + t, HEAD_DIM), BF16)],
        compiler_params=_cparams(1, VMEM_LIMIT),
        name="band_attn_sink" if use_sink else "band_attn",
    )(*args)


def _conv_kernel(prev_ref, u_ref, w_ref, cb_ref, g_ref, b_ref, o_ref, up_ref, sh_ref, *, t, tt):
    up_ref[0:CONV_HALO, :] = prev_ref[...]
    up_ref[CONV_HALO:CONV_HALO + t, :] = u_ref[...]
    lead = CONV_HALO - (CONV_W - 1)

    def tile(t0):
        win = up_ref[pl.ds(t0, tt + CONV_HALO), :]
        acc = jnp.zeros((tt, C_CH), F32)
        for phase in range(SUBLANES):
            offs = [j + lead for j in range(CONV_W) if (j + lead) % SUBLANES == phase]
            if not offs:
                continue
            span = max(offs) - phase + tt
            if phase:
                sh_ref[phase, 0:span, :] = win[phase:phase + span, :]
            for off in offs:
                a8 = off - phase
                rows = sh_ref[phase, a8:a8 + tt, :] if phase else win[a8:a8 + tt, :]
                acc = acc + rows * w_ref[off - lead:off - lead + 1, :]
        y = _layer_norm(acc + cb_ref[...], g_ref[...], b_ref[...])
        o_ref[pl.ds(t0, tt), :] = (y * jax.nn.sigmoid(y)).astype(o_ref.dtype)

    if t == tt:
        tile(0)
    else:
        def body(i, carry):
            tile(pl.multiple_of(i * tt, tt))
            return carry
        lax.fori_loop(0, t // tt, body, 0)


def _conv_tail(prev, u, conv_w, conv_b, ln_g, ln_b):
    b, t, c = u.shape
    tt = min(t, 128)
    vec = lambda a: a.reshape(1, c)
    vspec = pl.BlockSpec((1, c), lambda i: (0, 0))
    return pl.pallas_call(
        functools.partial(_conv_kernel, t=t, tt=tt),
        out_shape=jax.ShapeDtypeStruct((b, t, c), BF16),
        grid=(b,),
        in_specs=[pl.BlockSpec((None, CONV_HALO, c), lambda i: (i, 0, 0)),
                  pl.BlockSpec((None, t, c), lambda i: (i, 0, 0)),
                  pl.BlockSpec((CONV_W, c), lambda i: (0, 0)),
                  vspec, vspec, vspec],
        out_specs=pl.BlockSpec((None, t, c), lambda i: (i, 0, 0)),
        scratch_shapes=[pltpu.VMEM((CONV_HALO + t, c), F32),
                        pltpu.VMEM((SUBLANES, tt + CONV_HALO, c), F32)],
        compiler_params=_cparams(1, VMEM_LIMIT),
        name="conv_tail",
    )(prev, u, conv_w, vec(conv_b), vec(ln_g), vec(ln_b))


def _outproj_tail_kernel(*refs, n_blk):
    tail_ref, h2_ref = refs[13], refs[15]
    i = pl.program_id(0)

    @pl.when(i < n_blk)
    def _():
        _outproj_kernel(*refs[:13], *refs[14:])

    @pl.when(i >= n_blk)
    def _():
        h2_ref[...] = tail_ref[...]


def _outproj_kernel(oa_ref, ob_ref, oc_ref, x_ref, g1_ref, sc2_ref, sh2_ref, wo_ref,
                    lng_ref, lnb_ref, wr_hi_ref, wr_lo_ref, br_ref,
                    x1_ref, h2_ref, idx_ref, gate_ref):
    mix = (_dot(oa_ref[...], wo_ref[0:A_WIDTH, :])
           + _dot(ob_ref[...], wo_ref[A_WIDTH:A_WIDTH + B_WIDTH, :])
           + _dot(oc_ref[...], wo_ref[A_WIDTH + B_WIDTH:, :]))
    x1 = _layer_norm(DEEPNORM_ALPHA * x_ref[...] + g1_ref[...] * mix, lng_ref[...], lnb_ref[...])
    x1_ref[...] = x1
    h2 = x1 * (1.0 + sc2_ref[...]) + sh2_ref[...]
    _store_token_tiles(h2_ref, h2)
    h_hi, h_lo = _split_bf16(h2)
    hi_both = _dot(h_hi, jnp.concatenate([wr_hi_ref[...], wr_lo_ref[...]], axis=1))
    logits = (hi_both[:, :LANES] + hi_both[:, LANES:] + _dot(h_lo, wr_hi_ref[...])
              + br_ref[...])
    lane = lax.broadcasted_iota(jnp.int32, logits.shape, 1)
    lane_f = lane.astype(F32)
    cur = jnp.where(lane < N_EXPERTS, logits, NEG_INF)
    vals, idxs = [], []
    for _ in range(TOP_K):
        m = jnp.max(cur, axis=-1, keepdims=True)
        i = jnp.min(jnp.where(cur == m, lane_f, float(LANES)), axis=-1, keepdims=True)
        vals.append(m)
        idxs.append(i)
        cur = jnp.where(lane_f == i, NEG_INF, cur)
    es = [jnp.exp(v - vals[0]) for v in vals]
    den = es[0] + es[1] + es[2] + es[3]
    idx_out = jnp.zeros(logits.shape, jnp.int32)
    gate_out = jnp.zeros(logits.shape, F32)
    for k in range(TOP_K):
        idx_out = jnp.where(lane == k, idxs[k].astype(jnp.int32), idx_out)
        gate_out = jnp.where(lane == k, es[k] / den, gate_out)
    idx_ref[...] = idx_out
    gate_ref[...] = gate_out


def _outproj(oa, ob, oc, x, g1, sc2, sh2, wo_bf, ln_g, ln_b, wr_hi, wr_lo, br,
             *, tm, rows_per_batch, batch_off, h2_tail=None):
    m, d = x.shape
    bpb = rows_per_batch // tm
    n_blk = m // tm
    last = n_blk - 1
    row = lambda w: pl.BlockSpec((tm, w), lambda i: (jnp.minimum(i, last), 0))
    mod_spec = _mod_spec(g1, tm, d, bpb, batch_off, last)
    const = lambda a: pl.BlockSpec(a.shape, lambda i: (0, 0))
    in_specs = [row(A_WIDTH), row(B_WIDTH), row(C_CH), row(d), mod_spec, mod_spec, mod_spec,
                const(wo_bf), const(ln_g), const(ln_b), const(wr_hi), const(wr_lo), const(br)]
    args = [oa, ob, oc, x, g1, sc2, sh2, wo_bf, ln_g, ln_b, wr_hi, wr_lo, br]
    if h2_tail is None:
        kern, steps, h2_rows = _outproj_kernel, n_blk, m
    else:
        tail_tokens = h2_tail.shape[0] // ROW_PIECES
        assert h2_tail.shape[1] == LANES and tail_tokens % tm == 0
        kern = functools.partial(_outproj_tail_kernel, n_blk=n_blk)
        steps, h2_rows = n_blk + tail_tokens // tm, m + tail_tokens
        in_specs.append(pl.BlockSpec((tm * ROW_PIECES, LANES),
                                     lambda i: (jnp.maximum(i - n_blk, 0), 0)))
        args.append(h2_tail)
    return pl.pallas_call(
        kern,
        out_shape=[jax.ShapeDtypeStruct((m, d), F32),
                   jax.ShapeDtypeStruct((h2_rows * ROW_PIECES, LANES), F32),
                   jax.ShapeDtypeStruct((m, LANES), jnp.int32),
                   jax.ShapeDtypeStruct((m, LANES), F32)],
        grid=(steps,),
        in_specs=in_specs,
        out_specs=[row(d), pl.BlockSpec((tm * ROW_PIECES, LANES), lambda i: (i, 0)),
                   row(LANES), row(LANES)],
        compiler_params=_cparams(1, VMEM_LIMIT),
        name="outproj_route",
    )(*args)


def _moe_kernel(be_ref, nu_ref, idx_hbm, h2_hbm, w1_ref, b1_ref, w2_ref, b2_ref,
                picked_hbm, idx_smem, xbuf, obuf, w1b, w2b, sem_idx, sem_g, sem_s, *, tm):
    i = pl.program_id(0)
    nu = nu_ref[0]
    rp = ROW_PIECES

    def table_copy(blk, slot):
        return pltpu.make_async_copy(
            idx_hbm.at[pl.ds(pl.multiple_of(blk * IDX_STRIDE, IDX_STRIDE), IDX_STRIDE)],
            idx_smem.at[pl.ds(pl.multiple_of(slot * IDX_STRIDE, IDX_STRIDE), IDX_STRIDE)],
            sem_idx.at[slot])

    def row_loop(body, static_rows):
        if static_rows:
            for r in range(tm):
                body(r)
        else:
            lax.fori_loop(0, tm, lambda r, c: (body(r), c)[1], 0)

    def start_gathers(tslot, bslot, static_rows=True):
        base = tslot * IDX_STRIDE

        def body(r):
            src = pl.multiple_of(idx_smem[base + r], rp)
            row0 = r * rp if isinstance(r, int) else pl.multiple_of(r * rp, rp)
            pltpu.make_async_copy(h2_hbm.at[pl.ds(src, rp), :],
                                  xbuf.at[bslot, pl.ds(row0, rp), :], sem_g.at[bslot]).start()
        row_loop(body, static_rows)

    def wait_gathers(bslot):
        pltpu.make_async_copy(h2_hbm.at[pl.ds(0, tm * rp), :], xbuf.at[bslot],
                              sem_g.at[bslot]).wait()

    def start_scatters(tslot, bslot):
        base = tslot * IDX_STRIDE + tm

        def body(r):
            dst = pl.multiple_of(idx_smem[base + r], rp)
            pltpu.make_async_copy(obuf.at[bslot, pl.ds(r * rp, rp), :],
                                  picked_hbm.at[pl.ds(dst, rp), :], sem_s.at[bslot]).start()
        row_loop(body, True)

    def wait_scatters(bslot):
        pltpu.make_async_copy(obuf.at[bslot], picked_hbm.at[pl.ds(0, tm * rp), :],
                              sem_s.at[bslot]).wait()

    @pl.when(i < nu)
    def _():
        bslot = i % 2
        tslot = i % MOE_IDX_SLOTS

        @pl.when(i == 0)
        def _():
            first = table_copy(0, 0)
            first.start()
            first.wait()
            start_gathers(0, 0, static_rows=False)

            @pl.when(nu > 1)
            def _():
                table_copy(1, 1).start()

        @pl.when(i + 2 < nu)
        def _():
            table_copy(i + 2, (i + 2) % MOE_IDX_SLOTS).start()

        @pl.when(i + 1 < nu)
        def _():
            nslot = (i + 1) % MOE_IDX_SLOTS
            table_copy(i + 1, nslot).wait()
            start_gathers(nslot, 1 - bslot)

        @pl.when(jnp.logical_or(i == 0, be_ref[i] != be_ref[jnp.maximum(i - 1, 0)]))
        def _():
            w1b[...] = w1_ref[...].astype(BF16)
            w2b[...] = w2_ref[...].astype(BF16)

        wait_gathers(bslot)

        @pl.when(i >= 2)
        def _():
            wait_scatters(bslot)

        x = _load_token_tiles(xbuf.at[bslot], tm).astype(BF16)
        x_glu = jnp.minimum(_dot(x, w1b[:, 0:D_FF]) + b1_ref[:, 0:D_FF], SWIGLU_LIMIT)
        x_lin = jnp.clip(_dot(x, w1b[:, D_FF:]) + b1_ref[:, D_FF:], -SWIGLU_LIMIT, SWIGLU_LIMIT)
        act = x_glu * jax.nn.sigmoid(SWIGLU_ALPHA * x_glu) * (x_lin + 1.0)
        _store_token_tiles(obuf.at[bslot], _dot(act.astype(BF16), w2b[...]) + b2_ref[...])
        start_scatters(tslot, bslot)

        @pl.when(i == nu - 1)
        def _():
            @pl.when(i >= 1)
            def _():
                wait_scatters(1 - bslot)
            wait_scatters(bslot)

    @pl.when(i >= nu)
    def _():
        @pl.when(i == nu)
        def _():
            obuf[0] = jnp.zeros(obuf.shape[1:], F32)
        cp = pltpu.make_async_copy(
            obuf.at[0],
            picked_hbm.at[pl.ds(pl.multiple_of(i * (tm * rp), tm * rp), tm * rp), :], sem_s.at[0])
        cp.start()
        cp.wait()


def _moe(block_e, n_used, idx_flat, h2, w1, b1, w2, b2, *, n_rows, layer):
    tm = MOE_TILE
    n_blocks = block_e.shape[0]
    d = D_MODEL
    tile_rows = tm * ROW_PIECES
    expert = lambda i, be, nu: (layer, be[i], 0, 0)
    grid_spec = pltpu.PrefetchScalarGridSpec(
        num_scalar_prefetch=2,
        grid=(n_blocks,),
        in_specs=[pl.BlockSpec(memory_space=pl.ANY),
                  pl.BlockSpec(memory_space=pl.ANY),
                  pl.BlockSpec((None, None, d, 2 * D_FF), expert),
                  pl.BlockSpec((None, None, 1, 2 * D_FF), expert),
                  pl.BlockSpec((None, None, D_FF, d), expert),
                  pl.BlockSpec((None, None, 1, d), expert)],
        out_specs=pl.BlockSpec(memory_space=pl.ANY),
        scratch_shapes=[pltpu.SMEM((MOE_IDX_SLOTS * IDX_STRIDE,), jnp.int32),
                        pltpu.VMEM((2, tile_rows, LANES), F32),
                        pltpu.VMEM((2, tile_rows, LANES), F32),
                        pltpu.VMEM((d, 2 * D_FF), BF16), pltpu.VMEM((D_FF, d), BF16),
                        pltpu.SemaphoreType.DMA((MOE_IDX_SLOTS,)),
                        pltpu.SemaphoreType.DMA((2,)), pltpu.SemaphoreType.DMA((2,))])
    return pl.pallas_call(
        functools.partial(_moe_kernel, tm=tm),
        out_shape=jax.ShapeDtypeStruct((n_rows * ROW_PIECES, LANES), F32),
        grid_spec=grid_spec,
        compiler_params=_cparams(1, VMEM_LIMIT),
        name="moe_experts",
    )(block_e, n_used, idx_flat, h2, w1, b1.reshape(b1.shape[0], N_EXPERTS, 1, -1), w2,
      b2.reshape(b2.shape[0], N_EXPERTS, 1, -1))


def _routing_tables(top_idx, n_tok):
    tm = MOE_TILE
    n_assign = n_tok * TOP_K
    n_blocks = -(-n_assign // tm) + N_EXPERTS
    n_rows = n_blocks * tm
    i32 = jnp.int32
    flat_e = top_idx.reshape(n_assign)
    key_bits = (n_assign - 1).bit_length()
    assert N_EXPERTS << key_bits < 2 ** 31
    keys = jnp.sort(flat_e * (1 << key_bits) + jnp.arange(n_assign, dtype=i32))
    order = keys & ((1 << key_bits) - 1)
    experts = jnp.arange(N_EXPERTS, dtype=i32)
    counts = jnp.sum((flat_e[:, None] == experts[None, :]).astype(i32), axis=0)
    padded = (counts + tm - 1) // tm * tm
    pad_end = jnp.cumsum(padded)
    pad_start = pad_end - padded
    grp_start = jnp.cumsum(counts) - counts
    n_used = pad_end[-1] // tm
    blk = jnp.arange(n_blocks, dtype=i32)
    used = blk < n_used
    expert_at = lambda start: jnp.minimum(
        jnp.sum((pad_end[None, :] <= start[:, None]).astype(i32), axis=1), N_EXPERTS - 1)
    e_blk = expert_at(blk * tm)
    e_last = expert_at(((n_used - 1) * tm).reshape(1))[0]
    block_e = jnp.where(used, e_blk, e_last)
    row = blk[:, None] * tm + jnp.arange(tm, dtype=i32)[None, :]
    off = row - pad_start[e_blk][:, None]
    cnt = counts[e_blk][:, None]
    grp = grp_start[e_blk][:, None]
    valid = used[:, None] & (off < cnt)
    a = order[jnp.clip(grp + off, 0, n_assign - 1)]
    tok = a // TOP_K
    row_tok = jnp.where(valid, tok, 0)
    real_before = jnp.where(used[:, None], grp + cnt, n_assign)
    row_dst = jnp.where(valid, (a % TOP_K) * n_tok + tok, n_assign + row - real_before)
    idx = jnp.concatenate(
        [row_tok * ROW_PIECES, row_dst * ROW_PIECES,
         jnp.zeros((n_blocks, IDX_STRIDE - 2 * tm), i32)], axis=1)
    return block_e, n_used.reshape(1).astype(i32), idx.reshape(-1), n_rows


def _combine_kernel(p0_ref, p1_ref, p2_ref, p3_ref, gate_ref, x1_ref, g2_ref,
                    lng_ref, lnb_ref, o_ref):
    gate = gate_ref[...]
    n = gate.shape[0]
    y = (gate[:, 0:1] * _load_token_tiles(p0_ref, n) + gate[:, 1:2] * _load_token_tiles(p1_ref, n)
         + gate[:, 2:3] * _load_token_tiles(p2_ref, n) + gate[:, 3:4] * _load_token_tiles(p3_ref, n))
    o_ref[...] = _layer_norm(DEEPNORM_ALPHA * x1_ref[...] + g2_ref[...] * y,
                             lng_ref[...], lnb_ref[...])


def _combine(picked, gate, x1, g2, ln_g, ln_b, *, tm, rows_per_batch, batch_off, row_off, n_tok):
    m, d = x1.shape
    bpb = rows_per_batch // tm
    blk_off = row_off // tm
    k_stride = n_tok // tm
    pspec = lambda k: pl.BlockSpec((tm * ROW_PIECES, LANES),
                                   lambda i: (i + blk_off + k * k_stride, 0))
    row = lambda w: pl.BlockSpec((tm, w), lambda i: (i, 0))
    const = lambda a: pl.BlockSpec(a.shape, lambda i: (0, 0))
    return pl.pallas_call(
        _combine_kernel,
        out_shape=jax.ShapeDtypeStruct((m, d), F32),
        grid=(m // tm,),
        in_specs=[pspec(0), pspec(1), pspec(2), pspec(3), row(LANES), row(d),
                  _mod_spec(g2, tm, d, bpb, batch_off), const(ln_g), const(ln_b)],
        out_specs=row(d),
        compiler_params=_cparams(1, VMEM_LIMIT),
        name="combine_ln",
    )(picked, picked, picked, picked, gate, x1, g2, ln_g, ln_b)


def _clipped_rel_bias(rel, table):
    idx = np.clip(rel, -A_REL_CLIP, A_REL_CLIP) + A_REL_CLIP
    return jnp.moveaxis(table[idx].astype(F32), -1, 0)


def _t5_bucket(rel):
    nb = T5_BUCKETS // 2
    max_exact = nb // 2
    n = np.abs(rel)
    nf = np.maximum(n, 1).astype(np.float32)
    large = max_exact + (np.log(nf / max_exact) / math.log(T5_MAX_DISTANCE / max_exact)
                         * (nb - max_exact)).astype(np.int32)
    large = np.minimum(large, nb - 1)
    return np.where(rel > 0, nb, 0) + np.where(n < max_exact, n, large)


def _t5_rel_bias(rel, table):
    return jnp.moveaxis(table[_t5_bucket(rel)].astype(F32), -1, 0)


def kernel(x_prompt, x_sample, cache_a_k, cache_a_v, cache_b_k, cache_b_v, state_conv,
           c_prompt, c_sample, w_in, w_out, rel_bias_a, t5_bias, sinks, conv_w, conv_b,
           conv_ln_g, conv_ln_b, w_ada, b_ada, ln_g, ln_b, w_router, b_router,
           w_e_in, b_e_in, w_e_out, b_e_out):
    bp, tp, d = x_prompt.shape
    bs, ts, _ = x_sample.shape
    mp, ms = bp * tp, bs * ts
    n_tok = mp + ms
    na, nb = cache_a_k.shape[2], cache_b_k.shape[2]
    assert tp % ROW_TILE == 0 and mp % ts == 0 and ts >= CONV_W - 1 and tp >= A_REACH
    assert ms % OUTPROJ_TILE == 0 and tp % OUTPROJ_TILE == 0
    assert mp % ms == 0 and bs % SAMPLE_SEQS == 0

    xp = x_prompt.reshape(mp, d)
    xs = x_sample.reshape(ms, d)
    c_all = jnp.concatenate([c_prompt, c_sample], axis=0)

    t5_p = _band_bias(lambda rel: _t5_rel_bias(rel, t5_bias), **B_BAND)
    t5_s = _rel_bias_tile(lambda rel: _t5_rel_bias(rel, t5_bias), ts, nb + ts, nb)
    conv_zero = jnp.zeros((bp, CONV_HALO, C_CH), F32)

    states_p, states_s = [], []
    for l in range(DEPTH):
        mod = _adaln(c_all, w_ada, b_ada, l)
        sh1, sc1, g1, sh2, sc2, g2 = (mod[:, j * d:(j + 1) * d].reshape(bp + bs, 1, d)
                                      for j in range(6))
        w_in_bf = w_in[l].astype(BF16)
        w_out_bf = w_out[l].astype(BF16)
        wr = jnp.pad(w_router[l], ((0, 0), (0, LANES - N_EXPERTS)))
        wr_hi = wr.astype(BF16)
        wr_lo = (wr - wr_hi.astype(F32)).astype(BF16)
        br = jnp.pad(b_router[l], (0, LANES - N_EXPERTS)).reshape(1, LANES)
        lng1, lnb1 = ln_g[l, 0].reshape(1, d), ln_b[l, 0].reshape(1, d)
        lng2, lnb2 = ln_g[l, 1].reshape(1, d), ln_b[l, 1].reshape(1, d)
        bias_a_p = _band_bias(lambda rel: _clipped_rel_bias(rel, rel_bias_a[l]), **A_BAND)
        bias_a_s = _rel_bias_tile(lambda rel: _clipped_rel_bias(rel, rel_bias_a[l]),
                                  ts, na + ts, na)
        conv_args = (conv_w[l], conv_b[l], conv_ln_g[l], conv_ln_b[l])

        per_row = lambda a: jnp.broadcast_to(a[bp:], (bs, ts, d)).reshape(ms, d)
        qa, ka, va, qb, kb, vb, u = _inproj(xs, per_row(sc1), per_row(sh1), w_in_bf, tm=ms,
                                            rows_per_batch=ms, batch_off=0)
        s3 = lambda a: a.reshape(bs, ts, a.shape[-1])
        ka_all = jnp.concatenate([cache_a_k[l].reshape(bs, na, A_WIDTH), s3(ka)], axis=1)
        va_all = jnp.concatenate([cache_a_v[l].reshape(bs, na, A_WIDTH), s3(va)], axis=1)
        kb_all = jnp.concatenate([cache_b_k[l].reshape(bs, nb, B_KV_WIDTH), s3(kb)], axis=1)
        vb_all = jnp.concatenate([cache_b_v[l].reshape(bs, nb, B_KV_WIDTH), s3(vb)], axis=1)
        oa = _cache_attention(s3(qa), ka_all, va_all, bias_a_s, None,
                              n_heads=A_HEADS, group=1, n_seq=SAMPLE_SEQS)
        ob = _cache_attention(s3(qb), kb_all, vb_all, t5_s, sinks[l],
                              n_heads=B_HEADS, group=B_GROUP, n_seq=SAMPLE_SEQS)
        prev = jnp.pad(state_conv[l], ((0, 0), (CONV_HALO - (CONV_W - 1), 0), (0, 0)))
        oc = _conv_tail(prev, s3(u), *conv_args)
        x1s, h2s, idxs, gates = _outproj(
            oa.reshape(ms, -1), ob.reshape(ms, -1), oc.reshape(ms, -1), xs,
            per_row(g1), per_row(sc2), per_row(sh2),
            w_out_bf, lng1, lnb1, wr_hi, wr_lo, br, tm=ms, rows_per_batch=ms, batch_off=0)
        u_ext = jnp.concatenate([state_conv[l], s3(u)], axis=1)
        states_s.append((
            ka_all[:, -na:].reshape(bs, na, A_HEADS, HEAD_DIM),
            va_all[:, -na:].reshape(bs, na, A_HEADS, HEAD_DIM),
            kb_all[:, -nb:].reshape(bs, nb, B_KV_HEADS, HEAD_DIM),
            vb_all[:, -nb:].reshape(bs, nb, B_KV_HEADS, HEAD_DIM),
            u_ext[:, -(CONV_W - 1):]))

        qa, ka, va, qb, kb, vb, u = _inproj(xp, sc1, sh1, w_in_bf, tm=ROW_TILE,
                                            rows_per_batch=tp, batch_off=0)
        r3 = lambda a: a.reshape(bp, tp, a.shape[-1])
        oa = _band_attention(r3(qa), r3(ka), r3(va), bias_a_p, None, **A_BAND)
        ob = _band_attention(r3(qb), r3(kb), r3(vb), t5_p, sinks[l], **B_BAND)
        oc = _conv_tail(conv_zero, r3(u), *conv_args)
        x1p, h2, idxp, gatep = _outproj(
            oa.reshape(mp, -1), ob.reshape(mp, -1), oc.reshape(mp, -1), xp, g1, sc2, sh2,
            w_out_bf, lng1, lnb1, wr_hi, wr_lo, br, tm=OUTPROJ_TILE, rows_per_batch=tp, batch_off=0,
            h2_tail=h2s)
        states_p.append((
            r3(ka)[:, tp - A_REACH:].reshape(bp, A_REACH, A_HEADS, HEAD_DIM),
            r3(va)[:, tp - A_REACH:].reshape(bp, A_REACH, A_HEADS, HEAD_DIM),
            r3(kb)[:, tp - B_WINDOW:].reshape(bp, B_WINDOW, B_KV_HEADS, HEAD_DIM),
            r3(vb)[:, tp - B_WINDOW:].reshape(bp, B_WINDOW, B_KV_HEADS, HEAD_DIM),
            r3(u)[:, tp - (CONV_W - 1):]))

        top_idx = jnp.concatenate([idxp[:, :TOP_K], idxs[:, :TOP_K]], axis=0)
        block_e, n_used, idx_flat, n_rows = _routing_tables(top_idx, n_tok)
        picked = _moe(block_e, n_used, idx_flat, h2, w_e_in, b_e_in, w_e_out, b_e_out,
                      n_rows=n_rows, layer=l)
        xp = _combine(picked, gatep, x1p, g2, lng2, lnb2, tm=ROW_TILE, rows_per_batch=tp,
                      batch_off=0, row_off=0, n_tok=n_tok)
        xs = _combine(picked, gates, x1s, per_row(g2), lng2, lnb2, tm=ms, rows_per_batch=ms,
                      batch_off=0, row_off=mp, n_tok=n_tok)

    a_k_p, a_v_p, b_k_p, b_v_p, conv_p = (jnp.stack(z) for z in zip(*states_p))
    a_k_s, a_v_s, b_k_s, b_v_s, conv_s = (jnp.stack(z) for z in zip(*states_s))
    return (xp.reshape(bp, tp, d), xs.reshape(bs, ts, d), a_k_p, a_v_p, b_k_p, b_v_p, conv_p,
            a_k_s, a_v_s, b_k_s, b_v_s, conv_s)
```

```python
import functools
import math

import jax
import jax.numpy as jnp
import numpy as np
from jax import lax
from jax.experimental import pallas as pl
from jax.experimental.pallas import tpu as pltpu

F32 = jnp.float32
BF16 = jnp.bfloat16

D_MODEL = 1024
DEPTH = 2
CHUNK = 64
HEAD_DIM = 64
ATTN_SCALE = HEAD_DIM ** -0.5
A_HEADS = 4
A_WIDTH = A_HEADS * HEAD_DIM
A_PREV_CHUNKS = 8
A_REACH = A_PREV_CHUNKS * CHUNK
A_REL_CLIP = 128
B_HEADS = 8
B_KV_HEADS = 2
B_GROUP = B_HEADS // B_KV_HEADS
B_WIDTH = B_HEADS * HEAD_DIM
B_KV_WIDTH = B_KV_HEADS * HEAD_DIM
B_WINDOW = 128
B_PREV_CHUNKS = B_WINDOW // CHUNK
T5_BUCKETS = 32
T5_MAX_DISTANCE = 128
C_CH = D_MODEL // 4
CONV_W = 31
CONV_HALO = 32
N_EXPERTS = 32
TOP_K = 4
D_FF = D_MODEL
SWIGLU_LIMIT = 7.0
SWIGLU_ALPHA = 1.702
DEEPNORM_ALPHA = (2 * DEPTH) ** 0.25
LN_EPS = 1e-5
NEG_INF = -1e30

LANES = 128
ROW_TILE = 512
OUTPROJ_TILE = 512
SAMPLE_SEQS = 4
MOE_TILE = 256
IDX_STRIDE = 1024
MOE_IDX_SLOTS = 4
A_BAND = dict(g_chunks=4, n_prev=A_PREV_CHUNKS, n_kv=A_HEADS, group=1)
B_BAND = dict(g_chunks=2, n_prev=B_PREV_CHUNKS, n_kv=B_KV_HEADS, group=B_GROUP)
VMEM_LIMIT = 56 * 1024 * 1024


def _cparams(n_axes=1, vmem=None):
    return pltpu.CompilerParams(dimension_semantics=("arbitrary",) * n_axes,
                                vmem_limit_bytes=vmem)


def _dot(a, b):
    return jnp.dot(a, b, preferred_element_type=F32)


def _layer_norm(z, g, b):
    mu = jnp.mean(z, axis=-1, keepdims=True)
    d = z - mu
    var = jnp.mean(d * d, axis=-1, keepdims=True)
    return d * lax.rsqrt(var + LN_EPS) * g + b


SUBLANES = 8
ROW_PIECES = D_MODEL // LANES


def _store_token_tiles(ref, val):
    n = val.shape[0]
    for c in range(ROW_PIECES):
        ref[pl.ds(c, n, stride=ROW_PIECES), :] = val[:, c * LANES:(c + 1) * LANES]


def _load_token_tiles(ref, n):
    return jnp.concatenate(
        [ref[pl.ds(c, n, stride=ROW_PIECES), :] for c in range(ROW_PIECES)], axis=1)


def _split_bf16(a):
    hi = a.astype(BF16)
    lo = (a - hi.astype(F32)).astype(BF16)
    return hi, lo


def _adaln_kernel(c_ref, w_ref, b_ref, o_ref):
    c = c_ref[...]
    a_hi, a_lo = _split_bf16(c * jax.nn.sigmoid(c))
    w_hi, w_lo = _split_bf16(w_ref[...])
    o_ref[...] = _dot(a_hi, w_hi) + _dot(a_lo, w_hi) + _dot(a_hi, w_lo) + b_ref[...]


def _adaln(c_all, w_all, b_all, layer):
    nb, d = c_all.shape
    n = w_all.shape[2]
    tn = 1536
    return pl.pallas_call(
        _adaln_kernel,
        out_shape=jax.ShapeDtypeStruct((nb, n), F32),
        grid=(n // tn,),
        in_specs=[pl.BlockSpec((nb, d), lambda j: (0, 0)),
                  pl.BlockSpec((None, d, tn), lambda j: (layer, 0, j)),
                  pl.BlockSpec((None, 1, tn), lambda j: (layer, 0, j))],
        out_specs=pl.BlockSpec((nb, tn), lambda j: (0, j)),
        compiler_params=_cparams(1, VMEM_LIMIT),
        name="adaln",
    )(c_all, w_all, b_all.reshape(b_all.shape[0], 1, n))


_QA = (0, A_WIDTH)
_KA = (A_WIDTH, 2 * A_WIDTH)
_VA = (2 * A_WIDTH, 3 * A_WIDTH)
_QB = (3 * A_WIDTH, 3 * A_WIDTH + B_WIDTH)
_KB = (_QB[1], _QB[1] + B_KV_WIDTH)
_VB = (_KB[1], _KB[1] + B_KV_WIDTH)
_GA = (_VB[1], _VB[1] + C_CH)
_GG = (_GA[1], _GA[1] + C_CH)


def _inproj_kernel(x_ref, sc_ref, sh_ref, w_ref,
                   qa_ref, ka_ref, va_ref, qb_ref, kb_ref, vb_ref, u_ref):
    h = (x_ref[...] * (1.0 + sc_ref[...]) + sh_ref[...]).astype(BF16)

    def proj(cols):
        return _dot(h, w_ref[:, cols[0]:cols[1]])

    qa_ref[...] = (proj(_QA) * ATTN_SCALE).astype(BF16)
    ka_ref[...] = proj(_KA)
    va_ref[...] = proj(_VA)
    qb_ref[...] = (proj(_QB) * ATTN_SCALE).astype(BF16)
    kvb = proj((_KB[0], _VB[1]))
    kb_ref[...] = kvb[:, :B_KV_WIDTH]
    vb_ref[...] = kvb[:, B_KV_WIDTH:]
    u_ref[...] = proj(_GA) * jax.nn.sigmoid(proj(_GG))


def _mod_spec(mod, tm, d, bpb, batch_off, last=None):
    blk = (lambda i: i) if last is None else (lambda i: jnp.minimum(i, last))
    if mod.ndim == 2:
        return pl.BlockSpec((tm, d), lambda i: (blk(i), 0))
    return pl.BlockSpec((None, 1, d), lambda i: (blk(i) // bpb + batch_off, 0, 0))


def _inproj(x, sc, sh, w_bf, *, tm, rows_per_batch, batch_off):
    m, d = x.shape
    bpb = rows_per_batch // tm
    mod_spec = _mod_spec(sc, tm, d, bpb, batch_off)
    widths = (A_WIDTH, A_WIDTH, A_WIDTH, B_WIDTH, B_KV_WIDTH, B_KV_WIDTH, C_CH)
    dtypes = (BF16, F32, F32, BF16, F32, F32, F32)
    return pl.pallas_call(
        _inproj_kernel,
        out_shape=[jax.ShapeDtypeStruct((m, w), dt) for w, dt in zip(widths, dtypes)],
        grid=(m // tm,),
        in_specs=[pl.BlockSpec((tm, d), lambda i: (i, 0)), mod_spec, mod_spec,
                  pl.BlockSpec(w_bf.shape, lambda i: (0, 0))],
        out_specs=[pl.BlockSpec((tm, w), lambda i: (i, 0)) for w in widths],
        compiler_params=_cparams(1, VMEM_LIMIT),
        name="inproj",
    )(x, sc, sh, w_bf)


def _cache_attn_kernel(*refs, n_seq, n_heads, group, use_sink):
    if use_sink:
        q_ref, k_ref, v_ref, bias_ref, sink_ref, o_ref = refs
    else:
        q_ref, k_ref, v_ref, bias_ref, o_ref = refs
    for b in range(n_seq):
        qt = q_ref[b]
        kt = k_ref[b].astype(BF16)
        vt = v_ref[b].astype(BF16)
        outs = []
        for h in range(n_heads):
            n = h // group
            qh = qt[:, h * HEAD_DIM:(h + 1) * HEAD_DIM]
            kh = kt[:, n * HEAD_DIM:(n + 1) * HEAD_DIM]
            vh = vt[:, n * HEAD_DIM:(n + 1) * HEAD_DIM]
            s = lax.dot_general(qh, kh, (((1,), (1,)), ((), ())),
                                preferred_element_type=F32)
            s = s + bias_ref[h]
            m = jnp.max(s, axis=-1, keepdims=True)
            if use_sink:
                m = jnp.maximum(m, sink_ref[h])
            e = jnp.exp(s - m)
            den = jnp.sum(e, axis=-1, keepdims=True)
            if use_sink:
                den = den + jnp.exp(sink_ref[h] - m)
            outs.append(_dot(e.astype(BF16), vh) / den)
        o_ref[b] = jnp.concatenate(outs, axis=-1).astype(o_ref.dtype)


def _cache_attention(q, k, v, bias, sinks, *, n_heads, group, n_seq):
    b, t, qw = q.shape
    tk, kw = k.shape[1], k.shape[2]
    use_sink = sinks is not None
    kern = functools.partial(_cache_attn_kernel, n_seq=n_seq, n_heads=n_heads, group=group,
                             use_sink=use_sink)
    in_specs = [pl.BlockSpec((n_seq, t, qw), lambda i: (i, 0, 0)),
                pl.BlockSpec((n_seq, tk, kw), lambda i: (i, 0, 0)),
                pl.BlockSpec((n_seq, tk, kw), lambda i: (i, 0, 0)),
                pl.BlockSpec(bias.shape, lambda i: (0, 0, 0))]
    args = [q, k, v, bias]
    if use_sink:
        in_specs.append(pl.BlockSpec(memory_space=pltpu.SMEM))
        args.append(sinks)
    return pl.pallas_call(
        kern,
        out_shape=jax.ShapeDtypeStruct((b, t, qw), BF16),
        grid=(b // n_seq,),
        in_specs=in_specs,
        out_specs=pl.BlockSpec((n_seq, t, qw), lambda i: (i, 0, 0)),
        compiler_params=_cparams(1, VMEM_LIMIT),
        name="cache_attn_sink" if use_sink else "cache_attn",
    )(*args)


def _band_attn_kernel(*refs, t, g_chunks, n_prev, n_kv, group, use_sink):
    if use_sink:
        q_ref, k_ref, v_ref, bias_ref, sink_ref, o_ref, qs_ref, kp_ref, vp_ref = refs
    else:
        q_ref, k_ref, v_ref, bias_ref, o_ref, qs_ref, kp_ref, vp_ref = refs
    pad = n_prev * CHUNK
    gq = g_chunks * CHUNK
    u = pad + gq
    m = group * gq
    hd = HEAD_DIM
    for n in range(n_kv):
        zeros = jnp.zeros((pad, hd), BF16)
        kp_ref[n, 0:pad, :] = zeros
        vp_ref[n, 0:pad, :] = zeros
        kp_ref[n, pad:pad + t, :] = k_ref[:, n * hd:(n + 1) * hd].astype(BF16)
        vp_ref[n, pad:pad + t, :] = v_ref[:, n * hd:(n + 1) * hd].astype(BF16)
    for h in range(n_kv * group):
        qs_ref[h] = q_ref[:, h * hd:(h + 1) * hd]
    key_pos = lax.broadcasted_iota(jnp.int32, (m, u), 1)
    row = lax.broadcasted_iota(jnp.int32, (m, 1), 0)

    def body(g, carry):
        q0 = pl.multiple_of(g * gq, gq)
        valid = key_pos >= pad - q0
        outs = []
        for n in range(n_kv):
            qstk = jnp.concatenate(
                [qs_ref[n * group + j, pl.ds(q0, gq), :] for j in range(group)], axis=0)
            kt = kp_ref[n, pl.ds(q0, u), :]
            vt = vp_ref[n, pl.ds(q0, u), :]
            s = lax.dot_general(qstk, kt, (((1,), (1,)), ((), ())),
                                preferred_element_type=F32)
            s = jnp.where(valid, s + bias_ref[n], NEG_INF)
            mx = jnp.max(s, axis=-1, keepdims=True)
            if use_sink:
                sink = jnp.full((m, 1), sink_ref[n * group], F32)
                for j in range(1, group):
                    sink = jnp.where(row >= j * gq, sink_ref[n * group + j], sink)
                mx = jnp.maximum(mx, sink)
            e = jnp.exp(s - mx)
            den = jnp.sum(e, axis=-1, keepdims=True)
            if use_sink:
                den = den + jnp.exp(sink - mx)
            o = _dot(e.astype(BF16), vt) / den
            outs.extend(o[j * gq:(j + 1) * gq, :] for j in range(group))
        o_ref[pl.ds(q0, gq), :] = jnp.concatenate(outs, axis=-1).astype(o_ref.dtype)
        return carry
    lax.fori_loop(0, t // gq, body, 0)


def _band_bias(head_bias, *, g_chunks, n_prev, n_kv, group):
    pad, gq = n_prev * CHUNK, g_chunks * CHUNK
    u = pad + gq
    r = np.arange(gq)[:, None]
    kk = np.arange(u)[None, :]
    lo = (r // CHUNK) * CHUNK
    in_band = (kk >= lo) & (kk < lo + pad + CHUNK)
    tile = jnp.where(in_band[None], _rel_bias_tile(head_bias, gq, u, pad), NEG_INF)
    return tile.reshape(n_kv, group * gq, u)


def _rel_bias_tile(head_bias, rows, cols, pad):
    n_off = rows + cols - 1
    vec = head_bias(np.arange(n_off) - pad - (rows - 1))
    h = vec.shape[0]
    padded = jnp.concatenate([vec, jnp.zeros((h, 1), vec.dtype)], axis=1)
    skew = jnp.tile(padded, (1, rows))[:, :rows * n_off].reshape(h, rows, n_off)
    return skew[:, :, rows - 1:rows - 1 + cols]


def _band_attention(q, k, v, bias, sinks, *, g_chunks, n_prev, n_kv, group):
    b, t, qw = q.shape
    kw = k.shape[2]
    pad = n_prev * CHUNK
    use_sink = sinks is not None
    kern = functools.partial(_band_attn_kernel, t=t, g_chunks=g_chunks, n_prev=n_prev,
                             n_kv=n_kv, group=group, use_sink=use_sink)
    in_specs = [pl.BlockSpec((None, t, qw), lambda i: (i, 0, 0)),
                pl.BlockSpec((None, t, kw), lambda i: (i, 0, 0)),
                pl.BlockSpec((None, t, kw), lambda i: (i, 0, 0)),
                pl.BlockSpec(bias.shape, lambda i: (0, 0, 0))]
    args = [q, k, v, bias]
    if use_sink:
        in_specs.append(pl.BlockSpec(memory_space=pltpu.SMEM))
        args.append(sinks)
    return pl.pallas_call(
        kern,
        out_shape=jax.ShapeDtypeStruct((b, t, qw), BF16),
        grid=(b,),
        in_specs=in_specs,
        out_specs=pl.BlockSpec((None, t, qw), lambda i: (i, 0, 0)),
        scratch_shapes=[pltpu.VMEM((n_kv * group, t, HEAD_DIM), BF16),
                        pltpu.VMEM((n_kv, pad + t, HEAD_DIM), BF16),
                        pltpu.VMEM((n_kv, pad + t, HEAD_DIM), BF16)],
        compiler_params=_cparams(1, VMEM_LIMIT),
        name="band_attn_sink" if use_sink else "band_attn",
    )(*args)


def _conv_kernel(prev_ref, u_ref, w_ref, cb_ref, g_ref, b_ref, o_ref, up_ref, sh_ref, *, t, tt):
    up_ref[0:CONV_HALO, :] = prev_ref[...]
    up_ref[CONV_HALO:CONV_HALO + t, :] = u_ref[...]
    lead = CONV_HALO - (CONV_W - 1)

    def tile(t0):
        win = up_ref[pl.ds(t0, tt + CONV_HALO), :]
        acc = jnp.zeros((tt, C_CH), F32)
        for phase in range(SUBLANES):
            offs = [j + lead for j in range(CONV_W) if (j + lead) % SUBLANES == phase]
            if not offs:
                continue
            span = max(offs) - phase + tt
            if phase:
                sh_ref[phase, 0:span, :] = win[phase:phase + span, :]
            for off in offs:
                a8 = off - phase
                rows = sh_ref[phase, a8:a8 + tt, :] if phase else win[a8:a8 + tt, :]
                acc = acc + rows * w_ref[off - lead:off - lead + 1, :]
        y = _layer_norm(acc + cb_ref[...], g_ref[...], b_ref[...])
        o_ref[pl.ds(t0, tt), :] = (y * jax.nn.sigmoid(y)).astype(o_ref.dtype)

    if t == tt:
        tile(0)
    else:
        def body(i, carry):
            tile(pl.multiple_of(i * tt, tt))
            return carry
        lax.fori_loop(0, t // tt, body, 0)


def _conv_tail(prev, u, conv_w, conv_b, ln_g, ln_b):
    b, t, c = u.shape
    tt = min(t, 128)
    vec = lambda a: a.reshape(1, c)
    vspec = pl.BlockSpec((1, c), lambda i: (0, 0))
    return pl.pallas_call(
        functools.partial(_conv_kernel, t=t, tt=tt),
        out_shape=jax.ShapeDtypeStruct((b, t, c), BF16),
        grid=(b,),
        in_specs=[pl.BlockSpec((None, CONV_HALO, c), lambda i: (i, 0, 0)),
                  pl.BlockSpec((None, t, c), lambda i: (i, 0, 0)),
                  pl.BlockSpec((CONV_W, c), lambda i: (0, 0)),
                  vspec, vspec, vspec],
        out_specs=pl.BlockSpec((None, t, c), lambda i: (i, 0, 0)),
        scratch_shapes=[pltpu.VMEM((CONV_HALO + t, c), F32),
                        pltpu.VMEM((SUBLANES, tt + CONV_HALO, c), F32)],
        compiler_params=_cparams(1, VMEM_LIMIT),
        name="conv_tail",
    )(prev, u, conv_w, vec(conv_b), vec(ln_g), vec(ln_b))


def _outproj_tail_kernel(*refs, n_blk):
    tail_ref, h2_ref = refs[13], refs[15]
    i = pl.program_id(0)

    @pl.when(i < n_blk)
    def _():
        _outproj_kernel(*refs[:13], *refs[14:])

    @pl.when(i >= n_blk)
    def _():
        h2_ref[...] = tail_ref[...]


def _outproj_kernel(oa_ref, ob_ref, oc_ref, x_ref, g1_ref, sc2_ref, sh2_ref, wo_ref,
                    lng_ref, lnb_ref, wr_hi_ref, wr_lo_ref, br_ref,
                    x1_ref, h2_ref, idx_ref, gate_ref):
    mix = (_dot(oa_ref[...], wo_ref[0:A_WIDTH, :])
           + _dot(ob_ref[...], wo_ref[A_WIDTH:A_WIDTH + B_WIDTH, :])
           + _dot(oc_ref[...], wo_ref[A_WIDTH + B_WIDTH:, :]))
    x1 = _layer_norm(DEEPNORM_ALPHA * x_ref[...] + g1_ref[...] * mix, lng_ref[...], lnb_ref[...])
    x1_ref[...] = x1
    h2 = x1 * (1.0 + sc2_ref[...]) + sh2_ref[...]
    _store_token_tiles(h2_ref, h2)
    h_hi, h_lo = _split_bf16(h2)
    hi_both = _dot(h_hi, jnp.concatenate([wr_hi_ref[...], wr_lo_ref[...]], axis=1))
    logits = (hi_both[:, :LANES] + hi_both[:, LANES:] + _dot(h_lo, wr_hi_ref[...])
              + br_ref[...])
    lane = lax.broadcasted_iota(jnp.int32, logits.shape, 1)
    lane_f = lane.astype(F32)
    cur = jnp.where(lane < N_EXPERTS, logits, NEG_INF)
    vals, idxs = [], []
    for _ in range(TOP_K):
        m = jnp.max(cur, axis=-1, keepdims=True)
        i = jnp.min(jnp.where(cur == m, lane_f, float(LANES)), axis=-1, keepdims=True)
        vals.append(m)
        idxs.append(i)
        cur = jnp.where(lane_f == i, NEG_INF, cur)
    es = [jnp.exp(v - vals[0]) for v in vals]
    den = es[0] + es[1] + es[2] + es[3]
    idx_out = jnp.zeros(logits.shape, jnp.int32)
    gate_out = jnp.zeros(logits.shape, F32)
    for k in range(TOP_K):
        idx_out = jnp.where(lane == k, idxs[k].astype(jnp.int32), idx_out)
        gate_out = jnp.where(lane == k, es[k] / den, gate_out)
    idx_ref[...] = idx_out
    gate_ref[...] = gate_out


def _outproj(oa, ob, oc, x, g1, sc2, sh2, wo_bf, ln_g, ln_b, wr_hi, wr_lo, br,
             *, tm, rows_per_batch, batch_off, h2_tail=None):
    m, d = x.shape
    bpb = rows_per_batch // tm
    n_blk = m // tm
    last = n_blk - 1
    row = lambda w: pl.BlockSpec((tm, w), lambda i: (jnp.minimum(i, last), 0))
    mod_spec = _mod_spec(g1, tm, d, bpb, batch_off, last)
    const = lambda a: pl.BlockSpec(a.shape, lambda i: (0, 0))
    in_specs = [row(A_WIDTH), row(B_WIDTH), row(C_CH), row(d), mod_spec, mod_spec, mod_spec,
                const(wo_bf), const(ln_g), const(ln_b), const(wr_hi), const(wr_lo), const(br)]
    args = [oa, ob, oc, x, g1, sc2, sh2, wo_bf, ln_g, ln_b, wr_hi, wr_lo, br]
    if h2_tail is None:
        kern, steps, h2_rows = _outproj_kernel, n_blk, m
    else:
        tail_tokens = h2_tail.shape[0] // ROW_PIECES
        assert h2_tail.shape[1] == LANES and tail_tokens % tm == 0
        kern = functools.partial(_outproj_tail_kernel, n_blk=n_blk)
        steps, h2_rows = n_blk + tail_tokens // tm, m + tail_tokens
        in_specs.append(pl.BlockSpec((tm * ROW_PIECES, LANES),
                                     lambda i: (jnp.maximum(i - n_blk, 0), 0)))
        args.append(h2_tail)
    return pl.pallas_call(
        kern,
        out_shape=[jax.ShapeDtypeStruct((m, d), F32),
                   jax.ShapeDtypeStruct((h2_rows * ROW_PIECES, LANES), F32),
                   jax.ShapeDtypeStruct((m, LANES), jnp.int32),
                   jax.ShapeDtypeStruct((m, LANES), F32)],
        grid=(steps,),
        in_specs=in_specs,
        out_specs=[row(d), pl.BlockSpec((tm * ROW_PIECES, LANES), lambda i: (i, 0)),
                   row(LANES), row(LANES)],
        compiler_params=_cparams(1, VMEM_LIMIT),
        name="outproj_route",
    )(*args)


def _moe_kernel(be_ref, nu_ref, idx_hbm, h2_hbm, w1_ref, b1_ref, w2_ref, b2_ref,
                picked_hbm, idx_smem, xbuf, obuf, w1b, w2b, sem_idx, sem_g, sem_s, *, tm):
    i = pl.program_id(0)
    nu = nu_ref[0]
    rp = ROW_PIECES

    def table_copy(blk, slot):
        return pltpu.make_async_copy(
            idx_hbm.at[pl.ds(pl.multiple_of(blk * IDX_STRIDE, IDX_STRIDE), IDX_STRIDE)],
            idx_smem.at[pl.ds(pl.multiple_of(slot * IDX_STRIDE, IDX_STRIDE), IDX_STRIDE)],
            sem_idx.at[slot])

    def row_loop(body, static_rows):
        if static_rows:
            for r in range(tm):
                body(r)
        else:
            lax.fori_loop(0, tm, lambda r, c: (body(r), c)[1], 0)

    def start_gathers(tslot, bslot, static_rows=True):
        base = tslot * IDX_STRIDE

        def body(r):
            src = pl.multiple_of(idx_smem[base + r], rp)
            row0 = r * rp if isinstance(r, int) else pl.multiple_of(r * rp, rp)
            pltpu.make_async_copy(h2_hbm.at[pl.ds(src, rp), :],
                                  xbuf.at[bslot, pl.ds(row0, rp), :], sem_g.at[bslot]).start()
        row_loop(body, static_rows)

    def wait_gathers(bslot):
        pltpu.make_async_copy(h2_hbm.at[pl.ds(0, tm * rp), :], xbuf.at[bslot],
                              sem_g.at[bslot]).wait()

    def start_scatters(tslot, bslot):
        base = tslot * IDX_STRIDE + tm

        def body(r):
            dst = pl.multiple_of(idx_smem[base + r], rp)
            pltpu.make_async_copy(obuf.at[bslot, pl.ds(r * rp, rp), :],
                                  picked_hbm.at[pl.ds(dst, rp), :], sem_s.at[bslot]).start()
        row_loop(body, True)

    def wait_scatters(bslot):
        pltpu.make_async_copy(obuf.at[bslot], picked_hbm.at[pl.ds(0, tm * rp), :],
                              sem_s.at[bslot]).wait()

    @pl.when(i < nu)
    def _():
        bslot = i % 2
        tslot = i % MOE_IDX_SLOTS

        @pl.when(i == 0)
        def _():
            first = table_copy(0, 0)
            first.start()
            first.wait()
            start_gathers(0, 0, static_rows=False)

            @pl.when(nu > 1)
            def _():
                table_copy(1, 1).start()

        @pl.when(i + 2 < nu)
        def _():
            table_copy(i + 2, (i + 2) % MOE_IDX_SLOTS).start()

        @pl.when(i + 1 < nu)
        def _():
            nslot = (i + 1) % MOE_IDX_SLOTS
            table_copy(i + 1, nslot).wait()
            start_gathers(nslot, 1 - bslot)

        @pl.when(jnp.logical_or(i == 0, be_ref[i] != be_ref[jnp.maximum(i - 1, 0)]))
        def _():
            w1b[...] = w1_ref[...].astype(BF16)
            w2b[...] = w2_ref[...].astype(BF16)

        wait_gathers(bslot)

        @pl.when(i >= 2)
        def _():
            wait_scatters(bslot)

        x = _load_token_tiles(xbuf.at[bslot], tm).astype(BF16)
        x_glu = jnp.minimum(_dot(x, w1b[:, 0:D_FF]) + b1_ref[:, 0:D_FF], SWIGLU_LIMIT)
        x_lin = jnp.clip(_dot(x, w1b[:, D_FF:]) + b1_ref[:, D_FF:], -SWIGLU_LIMIT, SWIGLU_LIMIT)
        act = x_glu * jax.nn.sigmoid(SWIGLU_ALPHA * x_glu) * (x_lin + 1.0)
        _store_token_tiles(obuf.at[bslot], _dot(act.astype(BF16), w2b[...]) + b2_ref[...])
        start_scatters(tslot, bslot)

        @pl.when(i == nu - 1)
        def _():
            @pl.when(i >= 1)
            def _():
                wait_scatters(1 - bslot)
            wait_scatters(bslot)

    @pl.when(i >= nu)
    def _():
        @pl.when(i == nu)
        def _():
            obuf[0] = jnp.zeros(obuf.shape[1:], F32)
        cp = pltpu.make_async_copy(
            obuf.at[0],
            picked_hbm.at[pl.ds(pl.multiple_of(i * (tm * rp), tm * rp), tm * rp), :], sem_s.at[0])
        cp.start()
        cp.wait()


def _moe(block_e, n_used, idx_flat, h2, w1, b1, w2, b2, *, n_rows, layer):
    tm = MOE_TILE
    n_blocks = block_e.shape[0]
    d = D_MODEL
    tile_rows = tm * ROW_PIECES
    expert = lambda i, be, nu: (layer, be[i], 0, 0)
    grid_spec = pltpu.PrefetchScalarGridSpec(
        num_scalar_prefetch=2,
        grid=(n_blocks,),
        in_specs=[pl.BlockSpec(memory_space=pl.ANY),
                  pl.BlockSpec(memory_space=pl.ANY),
                  pl.BlockSpec((None, None, d, 2 * D_FF), expert),
                  pl.BlockSpec((None, None, 1, 2 * D_FF), expert),
                  pl.BlockSpec((None, None, D_FF, d), expert),
                  pl.BlockSpec((None, None, 1, d), expert)],
        out_specs=pl.BlockSpec(memory_space=pl.ANY),
        scratch_shapes=[pltpu.SMEM((MOE_IDX_SLOTS * IDX_STRIDE,), jnp.int32),
                        pltpu.VMEM((2, tile_rows, LANES), F32),
                        pltpu.VMEM((2, tile_rows, LANES), F32),
                        pltpu.VMEM((d, 2 * D_FF), BF16), pltpu.VMEM((D_FF, d), BF16),
                        pltpu.SemaphoreType.DMA((MOE_IDX_SLOTS,)),
                        pltpu.SemaphoreType.DMA((2,)), pltpu.SemaphoreType.DMA((2,))])
    return pl.pallas_call(
        functools.partial(_moe_kernel, tm=tm),
        out_shape=jax.ShapeDtypeStruct((n_rows * ROW_PIECES, LANES), F32),
        grid_spec=grid_spec,
        compiler_params=_cparams(1, VMEM_LIMIT),
        name="moe_experts",
    )(block_e, n_used, idx_flat, h2, w1, b1.reshape(b1.shape[0], N_EXPERTS, 1, -1), w2,
      b2.reshape(b2.shape[0], N_EXPERTS, 1, -1))


def _routing_tables(top_idx, n_tok):
    tm = MOE_TILE
    n_assign = n_tok * TOP_K
    n_blocks = -(-n_assign // tm) + N_EXPERTS
    n_rows = n_blocks * tm
    i32 = jnp.int32
    flat_e = top_idx.reshape(n_assign)
    key_bits = (n_assign - 1).bit_length()
    assert N_EXPERTS << key_bits < 2 ** 31
    keys = jnp.sort(flat_e * (1 << key_bits) + jnp.arange(n_assign, dtype=i32))
    order = keys & ((1 << key_bits) - 1)
    experts = jnp.arange(N_EXPERTS, dtype=i32)
    counts = jnp.sum((flat_e[:, None] == experts[None, :]).astype(i32), axis=0)
    padded = (counts + tm - 1) // tm * tm
    pad_end = jnp.cumsum(padded)
    pad_start = pad_end - padded
    grp_start = jnp.cumsum(counts) - counts
    n_used = pad_end[-1] // tm
    blk = jnp.arange(n_blocks, dtype=i32)
    used = blk < n_used
    expert_at = lambda start: jnp.minimum(
        jnp.sum((pad_end[None, :] <= start[:, None]).astype(i32), axis=1), N_EXPERTS - 1)
    e_blk = expert_at(blk * tm)
    e_last = expert_at(((n_used - 1) * tm).reshape(1))[0]
    block_e = jnp.where(used, e_blk, e_last)
    row = blk[:, None] * tm + jnp.arange(tm, dtype=i32)[None, :]
    off = row - pad_start[e_blk][:, None]
    cnt = counts[e_blk][:, None]
    grp = grp_start[e_blk][:, None]
    valid = used[:, None] & (off < cnt)
    a = order[jnp.clip(grp + off, 0, n_assign - 1)]
    tok = a // TOP_K
    row_tok = jnp.where(valid, tok, 0)
    real_before = jnp.where(used[:, None], grp + cnt, n_assign)
    row_dst = jnp.where(valid, (a % TOP_K) * n_tok + tok, n_assign + row - real_before)
    idx = jnp.concatenate(
        [row_tok * ROW_PIECES, row_dst * ROW_PIECES,
         jnp.zeros((n_blocks, IDX_STRIDE - 2 * tm), i32)], axis=1)
    return block_e, n_used.reshape(1).astype(i32), idx.reshape(-1), n_rows


def _combine_kernel(p0_ref, p1_ref, p2_ref, p3_ref, gate_ref, x1_ref, g2_ref,
                    lng_ref, lnb_ref, o_ref):
    gate = gate_ref[...]
    n = gate.shape[0]
    y = (gate[:, 0:1] * _load_token_tiles(p0_ref, n) + gate[:, 1:2] * _load_token_tiles(p1_ref, n)
         + gate[:, 2:3] * _load_token_tiles(p2_ref, n) + gate[:, 3:4] * _load_token_tiles(p3_ref, n))
    o_ref[...] = _layer_norm(DEEPNORM_ALPHA * x1_ref[...] + g2_ref[...] * y,
                             lng_ref[...], lnb_ref[...])


def _combine(picked, gate, x1, g2, ln_g, ln_b, *, tm, rows_per_batch, batch_off, row_off, n_tok):
    m, d = x1.shape
    bpb = rows_per_batch // tm
    blk_off = row_off // tm
    k_stride = n_tok // tm
    pspec = lambda k: pl.BlockSpec((tm * ROW_PIECES, LANES),
                                   lambda i: (i + blk_off + k * k_stride, 0))
    row = lambda w: pl.BlockSpec((tm, w), lambda i: (i, 0))
    const = lambda a: pl.BlockSpec(a.shape, lambda i: (0, 0))
    return pl.pallas_call(
        _combine_kernel,
        out_shape=jax.ShapeDtypeStruct((m, d), F32),
        grid=(m // tm,),
        in_specs=[pspec(0), pspec(1), pspec(2), pspec(3), row(LANES), row(d),
                  _mod_spec(g2, tm, d, bpb, batch_off), const(ln_g), const(ln_b)],
        out_specs=row(d),
        compiler_params=_cparams(1, VMEM_LIMIT),
        name="combine_ln",
    )(picked, picked, picked, picked, gate, x1, g2, ln_g, ln_b)


def _clipped_rel_bias(rel, table):
    idx = np.clip(rel, -A_REL_CLIP, A_REL_CLIP) + A_REL_CLIP
    return jnp.moveaxis(table[idx].astype(F32), -1, 0)


def _t5_bucket(rel):
    nb = T5_BUCKETS // 2
    max_exact = nb // 2
    n = np.abs(rel)
    nf = np.maximum(n, 1).astype(np.float32)
    large = max_exact + (np.log(nf / max_exact) / math.log(T5_MAX_DISTANCE / max_exact)
                         * (nb - max_exact)).astype(np.int32)
    large = np.minimum(large, nb - 1)
    return np.where(rel > 0, nb, 0) + np.where(n < max_exact, n, large)


def _t5_rel_bias(rel, table):
    return jnp.moveaxis(table[_t5_bucket(rel)].astype(F32), -1, 0)


def kernel(x_prompt, x_sample, cache_a_k, cache_a_v, cache_b_k, cache_b_v, state_conv,
           c_prompt, c_sample, w_in, w_out, rel_bias_a, t5_bias, sinks, conv_w, conv_b,
           conv_ln_g, conv_ln_b, w_ada, b_ada, ln_g, ln_b, w_router, b_router,
           w_e_in, b_e_in, w_e_out, b_e_out):
    bp, tp, d = x_prompt.shape
    bs, ts, _ = x_sample.shape
    mp, ms = bp * tp, bs * ts
    n_tok = mp + ms
    na, nb = cache_a_k.shape[2], cache_b_k.shape[2]
    assert tp % ROW_TILE == 0 and mp % ts == 0 and ts >= CONV_W - 1 and tp >= A_REACH
    assert ms % OUTPROJ_TILE == 0 and tp % OUTPROJ_TILE == 0
    assert mp % ms == 0 and bs % SAMPLE_SEQS == 0

    xp = x_prompt.reshape(mp, d)
    xs = x_sample.reshape(ms, d)
    c_all = jnp.concatenate([c_prompt, c_sample], axis=0)

    t5_p = _band_bias(lambda rel: _t5_rel_bias(rel, t5_bias), **B_BAND)
    t5_s = _rel_bias_tile(lambda rel: _t5_rel_bias(rel, t5_bias), ts, nb + ts, nb)
    conv_zero = jnp.zeros((bp, CONV_HALO, C_CH), F32)

    states_p, states_s = [], []
    for l in range(DEPTH):
        mod = _adaln(c_all, w_ada, b_ada, l)
        sh1, sc1, g1, sh2, sc2, g2 = (mod[:, j * d:(j + 1) * d].reshape(bp + bs, 1, d)
                                      for j in range(6))
        w_in_bf = w_in[l].astype(BF16)
        w_out_bf = w_out[l].astype(BF16)
        wr = jnp.pad(w_router[l], ((0, 0), (0, LANES - N_EXPERTS)))
        wr_hi = wr.astype(BF16)
        wr_lo = (wr - wr_hi.astype(F32)).astype(BF16)
        br = jnp.pad(b_router[l], (0, LANES - N_EXPERTS)).reshape(1, LANES)
        lng1, lnb1 = ln_g[l, 0].reshape(1, d), ln_b[l, 0].reshape(1, d)
        lng2, lnb2 = ln_g[l, 1].reshape(1, d), ln_b[l, 1].reshape(1, d)
        bias_a_p = _band_bias(lambda rel: _clipped_rel_bias(rel, rel_bias_a[l]), **A_BAND)
        bias_a_s = _rel_bias_tile(lambda rel: _clipped_rel_bias(rel, rel_bias_a[l]),
                                  ts, na + ts, na)
        conv_args = (conv_w[l], conv_b[l], conv_ln_g[l], conv_ln_b[l])

        per_row = lambda a: jnp.broadcast_to(a[bp:], (bs, ts, d)).reshape(ms, d)
        qa, ka, va, qb, kb, vb, u = _inproj(xs, per_row(sc1), per_row(sh1), w_in_bf, tm=ms,
                                            rows_per_batch=ms, batch_off=0)
        s3 = lambda a: a.reshape(bs, ts, a.shape[-1])
        ka_all = jnp.concatenate([cache_a_k[l].reshape(bs, na, A_WIDTH), s3(ka)], axis=1)
        va_all = jnp.concatenate([cache_a_v[l].reshape(bs, na, A_WIDTH), s3(va)], axis=1)
        kb_all = jnp.concatenate([cache_b_k[l].reshape(bs, nb, B_KV_WIDTH), s3(kb)], axis=1)
        vb_all = jnp.concatenate([cache_b_v[l].reshape(bs, nb, B_KV_WIDTH), s3(vb)], axis=1)
        oa = _cache_attention(s3(qa), ka_all, va_all, bias_a_s, None,
                              n_heads=A_HEADS, group=1, n_seq=SAMPLE_SEQS)
        ob = _cache_attention(s3(qb), kb_all, vb_all, t5_s, sinks[l],
                              n_heads=B_HEADS, group=B_GROUP, n_seq=SAMPLE_SEQS)
        prev = jnp.pad(state_conv[l], ((0, 0), (CONV_HALO - (CONV_W - 1), 0), (0, 0)))
        oc = _conv_tail(prev, s3(u), *conv_args)
        x1s, h2s, idxs, gates = _outproj(
            oa.reshape(ms, -1), ob.reshape(ms, -1), oc.reshape(ms, -1), xs,
            per_row(g1), per_row(sc2), per_row(sh2),
            w_out_bf, lng1, lnb1, wr_hi, wr_lo, br, tm=ms, rows_per_batch=ms, batch_off=0)
        u_ext = jnp.concatenate([state_conv[l], s3(u)], axis=1)
        states_s.append((
            ka_all[:, -na:].reshape(bs, na, A_HEADS, HEAD_DIM),
            va_all[:, -na:].reshape(bs, na, A_HEADS, HEAD_DIM),
            kb_all[:, -nb:].reshape(bs, nb, B_KV_HEADS, HEAD_DIM),
            vb_all[:, -nb:].reshape(bs, nb, B_KV_HEADS, HEAD_DIM),
            u_ext[:, -(CONV_W - 1):]))

        qa, ka, va, qb, kb, vb, u = _inproj(xp, sc1, sh1, w_in_bf, tm=ROW_TILE,
                                            rows_per_batch=tp, batch_off=0)
        r3 = lambda a: a.reshape(bp, tp, a.shape[-1])
        oa = _band_attention(r3(qa), r3(ka), r3(va), bias_a_p, None, **A_BAND)
        ob = _band_attention(r3(qb), r3(kb), r3(vb), t5_p, sinks[l], **B_BAND)
        oc = _conv_tail(conv_zero, r3(u), *conv_args)
        x1p, h2, idxp, gatep = _outproj(
            oa.reshape(mp, -1), ob.reshape(mp, -1), oc.reshape(mp, -1), xp, g1, sc2, sh2,
            w_out_bf, lng1, lnb1, wr_hi, wr_lo, br, tm=OUTPROJ_TILE, rows_per_batch=tp, batch_off=0,
            h2_tail=h2s)
        states_p.append((
            r3(ka)[:, tp - A_REACH:].reshape(bp, A_REACH, A_HEADS, HEAD_DIM),
            r3(va)[:, tp - A_REACH:].reshape(bp, A_REACH, A_HEADS, HEAD_DIM),
            r3(kb)[:, tp - B_WINDOW:].reshape(bp, B_WINDOW, B_KV_HEADS, HEAD_DIM),
            r3(vb)[:, tp - B_WINDOW:].reshape(bp, B_WINDOW, B_KV_HEADS, HEAD_DIM),
            r3(u)[:, tp - (CONV_W - 1):]))

        top_idx = jnp.concatenate([idxp[:, :TOP_K], idxs[:, :TOP_K]], axis=0)
        block_e, n_used, idx_flat, n_rows = _routing_tables(top_idx, n_tok)
        picked = _moe(block_e, n_used, idx_flat, h2, w_e_in, b_e_in, w_e_out, b_e_out,
                      n_rows=n_rows, layer=l)
        xp = _combine(picked, gatep, x1p, g2, lng2, lnb2, tm=ROW_TILE, rows_per_batch=tp,
                      batch_off=0, row_off=0, n_tok=n_tok)
        xs = _combine(picked, gates, x1s, per_row(g2), lng2, lnb2, tm=ms, rows_per_batch=ms,
                      batch_off=0, row_off=mp, n_tok=n_tok)

    a_k_p, a_v_p, b_k_p, b_v_p, conv_p = (jnp.stack(z) for z in zip(*states_p))
    a_k_s, a_v_s, b_k_s, b_v_s, conv_s = (jnp.stack(z) for z in zip(*states_s))
    return (xp.reshape(bp, tp, d), xs.reshape(bs, ts, d), a_k_p, a_v_p, b_k_p, b_v_p, conv_p,
            a_k_s, a_v_s, b_k_s, b_v_s, conv_s)
```

```python
import functools
import math

import jax
import jax.numpy as jnp
import numpy as np
from jax import lax
from jax.experimental import pallas as pl
from jax.experimental.pallas import tpu as pltpu

F32 = jnp.float32
BF16 = jnp.bfloat16

D_MODEL = 1024
DEPTH = 2
CHUNK = 64
HEAD_DIM = 64
ATTN_SCALE = HEAD_DIM ** -0.5
A_HEADS = 4
A_WIDTH = A_HEADS * HEAD_DIM
A_PREV_CHUNKS = 8
A_REACH = A_PREV_CHUNKS * CHUNK
A_REL_CLIP = 128
B_HEADS = 8
B_KV_HEADS = 2
B_GROUP = B_HEADS // B_KV_HEADS
B_WIDTH = B_HEADS * HEAD_DIM
B_KV_WIDTH = B_KV_HEADS * HEAD_DIM
B_WINDOW = 128
B_PREV_CHUNKS = B_WINDOW // CHUNK
T5_BUCKETS = 32
T5_MAX_DISTANCE = 128
C_CH = D_MODEL // 4
CONV_W = 31
CONV_HALO = 32
N_EXPERTS = 32
TOP_K = 4
D_FF = D_MODEL
SWIGLU_LIMIT = 7.0
SWIGLU_ALPHA = 1.702
DEEPNORM_ALPHA = (2 * DEPTH) ** 0.25
LN_EPS = 1e-5
NEG_INF = -1e30

LANES = 128
ROW_TILE = 512
OUTPROJ_TILE = 512
SAMPLE_SEQS = 4
COMBINE_BUFFERS = 3
MOE_TILE = 256
IDX_STRIDE = 1024
MOE_IDX_SLOTS = 4
A_BAND = dict(g_chunks=4, n_prev=A_PREV_CHUNKS, n_kv=A_HEADS, group=1)
B_BAND = dict(g_chunks=2, n_prev=B_PREV_CHUNKS, n_kv=B_KV_HEADS, group=B_GROUP)
VMEM_LIMIT = 56 * 1024 * 1024


def _cparams(n_axes=1, vmem=None):
    return pltpu.CompilerParams(dimension_semantics=("arbitrary",) * n_axes,
                                vmem_limit_bytes=vmem)


def _dot(a, b):
    return jnp.dot(a, b, preferred_element_type=F32)


def _layer_norm(z, g, b):
    mu = jnp.mean(z, axis=-1, keepdims=True)
    d = z - mu
    var = jnp.mean(d * d, axis=-1, keepdims=True)
    return d * lax.rsqrt(var + LN_EPS) * g + b


SUBLANES = 8
ROW_PIECES = D_MODEL // LANES


def _store_token_tiles(ref, val):
    n = val.shape[0]
    for c in range(ROW_PIECES):
        ref[pl.ds(c, n, stride=ROW_PIECES), :] = val[:, c * LANES:(c + 1) * LANES]


def _load_token_tiles(ref, n):
    return jnp.concatenate(
        [ref[pl.ds(c, n, stride=ROW_PIECES), :] for c in range(ROW_PIECES)], axis=1)


def _split_bf16(a):
    hi = a.astype(BF16)
    lo = (a - hi.astype(F32)).astype(BF16)
    return hi, lo


def _adaln_kernel(c_ref, w_ref, b_ref, o_ref):
    c = c_ref[...]
    a_hi, a_lo = _split_bf16(c * jax.nn.sigmoid(c))
    w_hi, w_lo = _split_bf16(w_ref[...])
    o_ref[...] = _dot(a_hi, w_hi) + _dot(a_lo, w_hi) + _dot(a_hi, w_lo) + b_ref[...]


def _adaln(c_all, w_all, b_all, layer):
    nb, d = c_all.shape
    n = w_all.shape[2]
    tn = 1536
    return pl.pallas_call(
        _adaln_kernel,
        out_shape=jax.ShapeDtypeStruct((nb, n), F32),
        grid=(n // tn,),
        in_specs=[pl.BlockSpec((nb, d), lambda j: (0, 0)),
                  pl.BlockSpec((None, d, tn), lambda j: (layer, 0, j)),
                  pl.BlockSpec((None, 1, tn), lambda j: (layer, 0, j))],
        out_specs=pl.BlockSpec((nb, tn), lambda j: (0, j)),
        compiler_params=_cparams(1, VMEM_LIMIT),
        name="adaln",
    )(c_all, w_all, b_all.reshape(b_all.shape[0], 1, n))


_QA = (0, A_WIDTH)
_KA = (A_WIDTH, 2 * A_WIDTH)
_VA = (2 * A_WIDTH, 3 * A_WIDTH)
_QB = (3 * A_WIDTH, 3 * A_WIDTH + B_WIDTH)
_KB = (_QB[1], _QB[1] + B_KV_WIDTH)
_VB = (_KB[1], _KB[1] + B_KV_WIDTH)
_GA = (_VB[1], _VB[1] + C_CH)
_GG = (_GA[1], _GA[1] + C_CH)


def _inproj_kernel(x_ref, sc_ref, sh_ref, w_ref,
                   qa_ref, ka_ref, va_ref, qb_ref, kb_ref, vb_ref, u_ref):
    h = (x_ref[...] * (1.0 + sc_ref[...]) + sh_ref[...]).astype(BF16)

    def proj(cols):
        return _dot(h, w_ref[:, cols[0]:cols[1]])

    qa_ref[...] = (proj(_QA) * ATTN_SCALE).astype(BF16)
    ka_ref[...] = proj(_KA)
    va_ref[...] = proj(_VA)
    qb_ref[...] = (proj(_QB) * ATTN_SCALE).astype(BF16)
    kvb = proj((_KB[0], _VB[1]))
    kb_ref[...] = kvb[:, :B_KV_WIDTH]
    vb_ref[...] = kvb[:, B_KV_WIDTH:]
    u_ref[...] = proj(_GA) * jax.nn.sigmoid(proj(_GG))


def _mod_spec(mod, tm, d, bpb, batch_off, last=None):
    blk = (lambda i: i) if last is None else (lambda i: jnp.minimum(i, last))
    if mod.ndim == 2:
        return pl.BlockSpec((tm, d), lambda i: (blk(i), 0))
    return pl.BlockSpec((None, 1, d), lambda i: (blk(i) // bpb + batch_off, 0, 0))


def _inproj(x, sc, sh, w_bf, *, tm, rows_per_batch, batch_off):
    m, d = x.shape
    bpb = rows_per_batch // tm
    mod_spec = _mod_spec(sc, tm, d, bpb, batch_off)
    widths = (A_WIDTH, A_WIDTH, A_WIDTH, B_WIDTH, B_KV_WIDTH, B_KV_WIDTH, C_CH)
    dtypes = (BF16, F32, F32, BF16, F32, F32, F32)
    return pl.pallas_call(
        _inproj_kernel,
        out_shape=[jax.ShapeDtypeStruct((m, w), dt) for w, dt in zip(widths, dtypes)],
        grid=(m // tm,),
        in_specs=[pl.BlockSpec((tm, d), lambda i: (i, 0)), mod_spec, mod_spec,
                  pl.BlockSpec(w_bf.shape, lambda i: (0, 0))],
        out_specs=[pl.BlockSpec((tm, w), lambda i: (i, 0)) for w in widths],
        compiler_params=_cparams(1, VMEM_LIMIT),
        name="inproj",
    )(x, sc, sh, w_bf)


def _cache_attn_kernel(*refs, n_seq, n_heads, group, use_sink):
    if use_sink:
        q_ref, k_ref, v_ref, bias_ref, sink_ref, o_ref = refs
    else:
        q_ref, k_ref, v_ref, bias_ref, o_ref = refs
    for b in range(n_seq):
        qt = q_ref[b]
        kt = k_ref[b].astype(BF16)
        vt = v_ref[b].astype(BF16)
        outs = []
        for h in range(n_heads):
            n = h // group
            qh = qt[:, h * HEAD_DIM:(h + 1) * HEAD_DIM]
            kh = kt[:, n * HEAD_DIM:(n + 1) * HEAD_DIM]
            vh = vt[:, n * HEAD_DIM:(n + 1) * HEAD_DIM]
            s = lax.dot_general(qh, kh, (((1,), (1,)), ((), ())),
                                preferred_element_type=F32)
            s = s + bias_ref[h]
            m = jnp.max(s, axis=-1, keepdims=True)
            if use_sink:
                m = jnp.maximum(m, sink_ref[h])
            e = jnp.exp(s - m)
            den = jnp.sum(e, axis=-1, keepdims=True)
            if use_sink:
                den = den + jnp.exp(sink_ref[h] - m)
            outs.append(_dot(e.astype(BF16), vh) / den)
        o_ref[b] = jnp.concatenate(outs, axis=-1).astype(o_ref.dtype)


def _cache_attention(q, k, v, bias, sinks, *, n_heads, group, n_seq):
    b, t, qw = q.shape
    tk, kw = k.shape[1], k.shape[2]
    use_sink = sinks is not None
    kern = functools.partial(_cache_attn_kernel, n_seq=n_seq, n_heads=n_heads, group=group,
                             use_sink=use_sink)
    in_specs = [pl.BlockSpec((n_seq, t, qw), lambda i: (i, 0, 0)),
                pl.BlockSpec((n_seq, tk, kw), lambda i: (i, 0, 0)),
                pl.BlockSpec((n_seq, tk, kw), lambda i: (i, 0, 0)),
                pl.BlockSpec(bias.shape, lambda i: (0, 0, 0))]
    args = [q, k, v, bias]
    if use_sink:
        in_specs.append(pl.BlockSpec(memory_space=pltpu.SMEM))
        args.append(sinks)
    return pl.pallas_call(
        kern,
        out_shape=jax.ShapeDtypeStruct((b, t, qw), BF16),
        grid=(b // n_seq,),
        in_specs=in_specs,
        out_specs=pl.BlockSpec((n_seq, t, qw), lambda i: (i, 0, 0)),
        compiler_params=_cparams(1, VMEM_LIMIT),
        name="cache_attn_sink" if use_sink else "cache_attn",
    )(*args)


def _band_attn_kernel(*refs, t, g_chunks, n_prev, n_kv, group, use_sink):
    if use_sink:
        q_ref, k_ref, v_ref, bias_ref, sink_ref, o_ref, qs_ref, kp_ref, vp_ref = refs
    else:
        q_ref, k_ref, v_ref, bias_ref, o_ref, qs_ref, kp_ref, vp_ref = refs
    pad = n_prev * CHUNK
    gq = g_chunks * CHUNK
    u = pad + gq
    m = group * gq
    hd = HEAD_DIM
    for n in range(n_kv):
        zeros = jnp.zeros((pad, hd), BF16)
        kp_ref[n, 0:pad, :] = zeros
        vp_ref[n, 0:pad, :] = zeros
        kp_ref[n, pad:pad + t, :] = k_ref[:, n * hd:(n + 1) * hd].astype(BF16)
        vp_ref[n, pad:pad + t, :] = v_ref[:, n * hd:(n + 1) * hd].astype(BF16)
    for h in range(n_kv * group):
        qs_ref[h] = q_ref[:, h * hd:(h + 1) * hd]
    key_pos = lax.broadcasted_iota(jnp.int32, (m, u), 1)
    row = lax.broadcasted_iota(jnp.int32, (m, 1), 0)

    def body(g, carry):
        q0 = pl.multiple_of(g * gq, gq)
        valid = key_pos >= pad - q0
        outs = []
        for n in range(n_kv):
            qstk = jnp.concatenate(
                [qs_ref[n * group + j, pl.ds(q0, gq), :] for j in range(group)], axis=0)
            kt = kp_ref[n, pl.ds(q0, u), :]
            vt = vp_ref[n, pl.ds(q0, u), :]
            s = lax.dot_general(qstk, kt, (((1,), (1,)), ((), ())),
                                preferred_element_type=F32)
            s = jnp.where(valid, s + bias_ref[n], NEG_INF)
            mx = jnp.max(s, axis=-1, keepdims=True)
            if use_sink:
                sink = jnp.full((m, 1), sink_ref[n * group], F32)
                for j in range(1, group):
                    sink = jnp.where(row >= j * gq, sink_ref[n * group + j], sink)
                mx = jnp.maximum(mx, sink)
            e = jnp.exp(s - mx)
            den = jnp.sum(e, axis=-1, keepdims=True)
            if use_sink:
                den = den + jnp.exp(sink - mx)
            o = _dot(e.astype(BF16), vt) / den
            outs.extend(o[j * gq:(j + 1) * gq, :] for j in range(group))
        o_ref[pl.ds(q0, gq), :] = jnp.concatenate(outs, axis=-1).astype(o_ref.dtype)
        return carry
    lax.fori_loop(0, t // gq, body, 0)


def _band_bias(head_bias, *, g_chunks, n_prev, n_kv, group):
    pad, gq = n_prev * CHUNK, g_chunks * CHUNK
    u = pad + gq
    r = np.arange(gq)[:, None]
    kk = np.arange(u)[None, :]
    lo = (r // CHUNK) * CHUNK
    in_band = (kk >= lo) & (kk < lo + pad + CHUNK)
    tile = jnp.where(in_band[None], _rel_bias_tile(head_bias, gq, u, pad), NEG_INF)
    return tile.reshape(n_kv, group * gq, u)


def _rel_bias_tile(head_bias, rows, cols, pad):
    n_off = rows + cols - 1
    vec = head_bias(np.arange(n_off) - pad - (rows - 1))
    h = vec.shape[0]
    padded = jnp.concatenate([vec, jnp.zeros((h, 1), vec.dtype)], axis=1)
    skew = jnp.tile(padded, (1, rows))[:, :rows * n_off].reshape(h, rows, n_off)
    return skew[:, :, rows - 1:rows - 1 + cols]


def _band_attention(q, k, v, bias, sinks, *, g_chunks, n_prev, n_kv, group):
    b, t, qw = q.shape
    kw = k.shape[2]
    pad = n_prev * CHUNK
    use_sink = sinks is not None
    kern = functools.partial(_band_attn_kernel, t=t, g_chunks=g_chunks, n_prev=n_prev,
                             n_kv=n_kv, group=group, use_sink=use_sink)
    in_specs = [pl.BlockSpec((None, t, qw), lambda i: (i, 0, 0)),
                pl.BlockSpec((None, t, kw), lambda i: (i, 0, 0)),
                pl.BlockSpec((None, t, kw), lambda i: (i, 0, 0)),
                pl.BlockSpec(bias.shape, lambda i: (0, 0, 0))]
    args = [q, k, v, bias]
    if use_sink:
        in_specs.append(pl.BlockSpec(memory_space=pltpu.SMEM))
        args.append(sinks)
    return pl.pallas_call(
        kern,
        out_shape=jax.ShapeDtypeStruct((b, t, qw), BF16),
        grid=(b,),
        in_specs=in_specs,
        out_specs=pl.BlockSpec((None, t, qw), lambda i: (i, 0, 0)),
        scratch_shapes=[pltpu.VMEM((n_kv * group, t, HEAD_DIM), BF16),
                        pltpu.VMEM((n_kv, pad + t, HEAD_DIM), BF16),
                        pltpu.VMEM((n_kv, pad + t, HEAD_DIM), BF16)],
        compiler_params=_cparams(1, VMEM_LIMIT),
        name="band_attn_sink" if use_sink else "band_attn",
    )(*args)


def _conv_kernel(prev_ref, u_ref, w_ref, cb_ref, g_ref, b_ref, o_ref, up_ref, sh_ref, *, t, tt):
    up_ref[0:CONV_HALO, :] = prev_ref[...]
    up_ref[CONV_HALO:CONV_HALO + t, :] = u_ref[...]
    lead = CONV_HALO - (CONV_W - 1)

    def tile(t0):
        win = up_ref[pl.ds(t0, tt + CONV_HALO), :]
        acc = jnp.zeros((tt, C_CH), F32)
        for phase in range(SUBLANES):
            offs = [j + lead for j in range(CONV_W) if (j + lead) % SUBLANES == phase]
            if not offs:
                continue
            span = max(offs) - phase + tt
            if phase:
                sh_ref[phase, 0:span, :] = win[phase:phase + span, :]
            for off in offs:
                a8 = off - phase
                rows = sh_ref[phase, a8:a8 + tt, :] if phase else win[a8:a8 + tt, :]
                acc = acc + rows * w_ref[off - lead:off - lead + 1, :]
        y = _layer_norm(acc + cb_ref[...], g_ref[...], b_ref[...])
        o_ref[pl.ds(t0, tt), :] = (y * jax.nn.sigmoid(y)).astype(o_ref.dtype)

    if t == tt:
        tile(0)
    else:
        def body(i, carry):
            tile(pl.multiple_of(i * tt, tt))
            return carry
        lax.fori_loop(0, t // tt, body, 0)


def _conv_tail(prev, u, conv_w, conv_b, ln_g, ln_b):
    b, t, c = u.shape
    tt = min(t, 128)
    vec = lambda a: a.reshape(1, c)
    vspec = pl.BlockSpec((1, c), lambda i: (0, 0))
    return pl.pallas_call(
        functools.partial(_conv_kernel, t=t, tt=tt),
        out_shape=jax.ShapeDtypeStruct((b, t, c), BF16),
        grid=(b,),
        in_specs=[pl.BlockSpec((None, CONV_HALO, c), lambda i: (i, 0, 0)),
                  pl.BlockSpec((None, t, c), lambda i: (i, 0, 0)),
                  pl.BlockSpec((CONV_W, c), lambda i: (0, 0)),
                  vspec, vspec, vspec],
        out_specs=pl.BlockSpec((None, t, c), lambda i: (i, 0, 0)),
        scratch_shapes=[pltpu.VMEM((CONV_HALO + t, c), F32),
                        pltpu.VMEM((SUBLANES, tt + CONV_HALO, c), F32)],
        compiler_params=_cparams(1, VMEM_LIMIT),
        name="conv_tail",
    )(prev, u, conv_w, vec(conv_b), vec(ln_g), vec(ln_b))


def _outproj_tail_kernel(*refs, n_blk):
    tail_ref, h2_ref = refs[13], refs[15]
    i = pl.program_id(0)

    @pl.when(i < n_blk)
    def _():
        _outproj_kernel(*refs[:13], *refs[14:])

    @pl.when(i >= n_blk)
    def _():
        h2_ref[...] = tail_ref[...]


def _outproj_kernel(oa_ref, ob_ref, oc_ref, x_ref, g1_ref, sc2_ref, sh2_ref, wo_ref,
                    lng_ref, lnb_ref, wr_hi_ref, wr_lo_ref, br_ref,
                    x1_ref, h2_ref, idx_ref, gate_ref):
    mix = (_dot(oa_ref[...], wo_ref[0:A_WIDTH, :])
           + _dot(ob_ref[...], wo_ref[A_WIDTH:A_WIDTH + B_WIDTH, :])
           + _dot(oc_ref[...], wo_ref[A_WIDTH + B_WIDTH:, :]))
    x1 = _layer_norm(DEEPNORM_ALPHA * x_ref[...] + g1_ref[...] * mix, lng_ref[...], lnb_ref[...])
    x1_ref[...] = x1
    h2 = x1 * (1.0 + sc2_ref[...]) + sh2_ref[...]
    _store_token_tiles(h2_ref, h2)
    h_hi, h_lo = _split_bf16(h2)
    hi_both = _dot(h_hi, jnp.concatenate([wr_hi_ref[...], wr_lo_ref[...]], axis=1))
    logits = (hi_both[:, :LANES] + hi_both[:, LANES:] + _dot(h_lo, wr_hi_ref[...])
              + br_ref[...])
    lane = lax.broadcasted_iota(jnp.int32, logits.shape, 1)
    lane_f = lane.astype(F32)
    cur = jnp.where(lane < N_EXPERTS, logits, NEG_INF)
    vals, idxs = [], []
    for _ in range(TOP_K):
        m = jnp.max(cur, axis=-1, keepdims=True)
        i = jnp.min(jnp.where(cur == m, lane_f, float(LANES)), axis=-1, keepdims=True)
        vals.append(m)
        idxs.append(i)
        cur = jnp.where(lane_f == i, NEG_INF, cur)
    es = [jnp.exp(v - vals[0]) for v in vals]
    den = es[0] + es[1] + es[2] + es[3]
    idx_out = jnp.zeros(logits.shape, jnp.int32)
    gate_out = jnp.zeros(logits.shape, F32)
    for k in range(TOP_K):
        idx_out = jnp.where(lane == k, idxs[k].astype(jnp.int32), idx_out)
        gate_out = jnp.where(lane == k, es[k] / den, gate_out)
    idx_ref[...] = idx_out
    gate_ref[...] = gate_out


def _outproj(oa, ob, oc, x, g1, sc2, sh2, wo_bf, ln_g, ln_b, wr_hi, wr_lo, br,
             *, tm, rows_per_batch, batch_off, h2_tail=None):
    m, d = x.shape
    bpb = rows_per_batch // tm
    n_blk = m // tm
    last = n_blk - 1
    row = lambda w: pl.BlockSpec((tm, w), lambda i: (jnp.minimum(i, last), 0))
    mod_spec = _mod_spec(g1, tm, d, bpb, batch_off, last)
    const = lambda a: pl.BlockSpec(a.shape, lambda i: (0, 0))
    in_specs = [row(A_WIDTH), row(B_WIDTH), row(C_CH), row(d), mod_spec, mod_spec, mod_spec,
                const(wo_bf), const(ln_g), const(ln_b), const(wr_hi), const(wr_lo), const(br)]
    args = [oa, ob, oc, x, g1, sc2, sh2, wo_bf, ln_g, ln_b, wr_hi, wr_lo, br]
    if h2_tail is None:
        kern, steps, h2_rows = _outproj_kernel, n_blk, m
    else:
        tail_tokens = h2_tail.shape[0] // ROW_PIECES
        assert h2_tail.shape[1] == LANES and tail_tokens % tm == 0
        kern = functools.partial(_outproj_tail_kernel, n_blk=n_blk)
        steps, h2_rows = n_blk + tail_tokens // tm, m + tail_tokens
        in_specs.append(pl.BlockSpec((tm * ROW_PIECES, LANES),
                                     lambda i: (jnp.maximum(i - n_blk, 0), 0)))
        args.append(h2_tail)
    return pl.pallas_call(
        kern,
        out_shape=[jax.ShapeDtypeStruct((m, d), F32),
                   jax.ShapeDtypeStruct((h2_rows * ROW_PIECES, LANES), F32),
                   jax.ShapeDtypeStruct((m, LANES), jnp.int32),
                   jax.ShapeDtypeStruct((m, LANES), F32)],
        grid=(steps,),
        in_specs=in_specs,
        out_specs=[row(d), pl.BlockSpec((tm * ROW_PIECES, LANES), lambda i: (i, 0)),
                   row(LANES), row(LANES)],
        compiler_params=_cparams(1, VMEM_LIMIT),
        name="outproj_route",
    )(*args)


def _moe_kernel(be_ref, nu_ref, idx_hbm, h2_hbm, w1_ref, b1_ref, w2_ref, b2_ref,
                picked_hbm, idx_smem, xbuf, obuf, w1b, w2b, sem_idx, sem_g, sem_s, *, tm):
    i = pl.program_id(0)
    nu = nu_ref[0]
    rp = ROW_PIECES

    def table_copy(blk, slot):
        return pltpu.make_async_copy(
            idx_hbm.at[pl.ds(pl.multiple_of(blk * IDX_STRIDE, IDX_STRIDE), IDX_STRIDE)],
            idx_smem.at[pl.ds(pl.multiple_of(slot * IDX_STRIDE, IDX_STRIDE), IDX_STRIDE)],
            sem_idx.at[slot])

    def row_loop(body, static_rows):
        if static_rows:
            for r in range(tm):
                body(r)
        else:
            lax.fori_loop(0, tm, lambda r, c: (body(r), c)[1], 0)

    def start_gathers(tslot, bslot, static_rows=True):
        base = tslot * IDX_STRIDE

        def body(r):
            src = pl.multiple_of(idx_smem[base + r], rp)
            row0 = r * rp if isinstance(r, int) else pl.multiple_of(r * rp, rp)
            pltpu.make_async_copy(h2_hbm.at[pl.ds(src, rp), :],
                                  xbuf.at[bslot, pl.ds(row0, rp), :], sem_g.at[bslot]).start()
        row_loop(body, static_rows)

    def wait_gathers(bslot):
        pltpu.make_async_copy(h2_hbm.at[pl.ds(0, tm * rp), :], xbuf.at[bslot],
                              sem_g.at[bslot]).wait()

    def start_scatters(tslot, bslot):
        base = tslot * IDX_STRIDE + tm

        def body(r):
            dst = pl.multiple_of(idx_smem[base + r], rp)
            pltpu.make_async_copy(obuf.at[bslot, pl.ds(r * rp, rp), :],
                                  picked_hbm.at[pl.ds(dst, rp), :], sem_s.at[bslot]).start()
        row_loop(body, True)

    def wait_scatters(bslot):
        pltpu.make_async_copy(obuf.at[bslot], picked_hbm.at[pl.ds(0, tm * rp), :],
                              sem_s.at[bslot]).wait()

    @pl.when(i < nu)
    def _():
        bslot = i % 2
        tslot = i % MOE_IDX_SLOTS

        @pl.when(i == 0)
        def _():
            first = table_copy(0, 0)
            first.start()
            first.wait()
            start_gathers(0, 0, static_rows=False)

            @pl.when(nu > 1)
            def _():
                table_copy(1, 1).start()

        @pl.when(i + 2 < nu)
        def _():
            table_copy(i + 2, (i + 2) % MOE_IDX_SLOTS).start()

        @pl.when(i + 1 < nu)
        def _():
            nslot = (i + 1) % MOE_IDX_SLOTS
            table_copy(i + 1, nslot).wait()
            start_gathers(nslot, 1 - bslot)

        @pl.when(jnp.logical_or(i == 0, be_ref[i] != be_ref[jnp.maximum(i - 1, 0)]))
        def _():
            w1b[...] = w1_ref[...].astype(BF16)
            w2b[...] = w2_ref[...].astype(BF16)

        wait_gathers(bslot)

        @pl.when(i >= 2)
        def _():
            wait_scatters(bslot)

        x = _load_token_tiles(xbuf.at[bslot], tm).astype(BF16)
        x_glu = jnp.minimum(_dot(x, w1b[:, 0:D_FF]) + b1_ref[:, 0:D_FF], SWIGLU_LIMIT)
        x_lin = jnp.clip(_dot(x, w1b[:, D_FF:]) + b1_ref[:, D_FF:], -SWIGLU_LIMIT, SWIGLU_LIMIT)
        act = x_glu * jax.nn.sigmoid(SWIGLU_ALPHA * x_glu) * (x_lin + 1.0)
        _store_token_tiles(obuf.at[bslot], _dot(act.astype(BF16), w2b[...]) + b2_ref[...])
        start_scatters(tslot, bslot)

        @pl.when(i == nu - 1)
        def _():
            @pl.when(i >= 1)
            def _():
                wait_scatters(1 - bslot)
            wait_scatters(bslot)

    @pl.when(i >= nu)
    def _():
        @pl.when(i == nu)
        def _():
            obuf[0] = jnp.zeros(obuf.shape[1:], F32)
        cp = pltpu.make_async_copy(
            obuf.at[0],
            picked_hbm.at[pl.ds(pl.multiple_of(i * (tm * rp), tm * rp), tm * rp), :], sem_s.at[0])
        cp.start()
        cp.wait()


def _moe(block_e, n_used, idx_flat, h2, w1, b1, w2, b2, *, n_rows, layer):
    tm = MOE_TILE
    n_blocks = block_e.shape[0]
    d = D_MODEL
    tile_rows = tm * ROW_PIECES
    expert = lambda i, be, nu: (layer, be[i], 0, 0)
    grid_spec = pltpu.PrefetchScalarGridSpec(
        num_scalar_prefetch=2,
        grid=(n_blocks,),
        in_specs=[pl.BlockSpec(memory_space=pl.ANY),
                  pl.BlockSpec(memory_space=pl.ANY),
                  pl.BlockSpec((None, None, d, 2 * D_FF), expert),
                  pl.BlockSpec((None, None, 1, 2 * D_FF), expert),
                  pl.BlockSpec((None, None, D_FF, d), expert),
                  pl.BlockSpec((None, None, 1, d), expert)],
        out_specs=pl.BlockSpec(memory_space=pl.ANY),
        scratch_shapes=[pltpu.SMEM((MOE_IDX_SLOTS * IDX_STRIDE,), jnp.int32),
                        pltpu.VMEM((2, tile_rows, LANES), F32),
                        pltpu.VMEM((2, tile_rows, LANES), F32),
                        pltpu.VMEM((d, 2 * D_FF), BF16), pltpu.VMEM((D_FF, d), BF16),
                        pltpu.SemaphoreType.DMA((MOE_IDX_SLOTS,)),
                        pltpu.SemaphoreType.DMA((2,)), pltpu.SemaphoreType.DMA((2,))])
    return pl.pallas_call(
        functools.partial(_moe_kernel, tm=tm),
        out_shape=jax.ShapeDtypeStruct((n_rows * ROW_PIECES, LANES), F32),
        grid_spec=grid_spec,
        compiler_params=_cparams(1, VMEM_LIMIT),
        name="moe_experts",
    )(block_e, n_used, idx_flat, h2, w1, b1.reshape(b1.shape[0], N_EXPERTS, 1, -1), w2,
      b2.reshape(b2.shape[0], N_EXPERTS, 1, -1))


def _routing_tables(top_idx, n_tok):
    tm = MOE_TILE
    n_assign = n_tok * TOP_K
    n_blocks = -(-n_assign // tm) + N_EXPERTS
    n_rows = n_blocks * tm
    i32 = jnp.int32
    flat_e = top_idx.reshape(n_assign)
    key_bits = (n_assign - 1).bit_length()
    assert N_EXPERTS << key_bits < 2 ** 31
    keys = jnp.sort(flat_e * (1 << key_bits) + jnp.arange(n_assign, dtype=i32))
    order = keys & ((1 << key_bits) - 1)
    experts = jnp.arange(N_EXPERTS, dtype=i32)
    counts = jnp.sum((flat_e[:, None] == experts[None, :]).astype(i32), axis=0)
    padded = (counts + tm - 1) // tm * tm
    pad_end = jnp.cumsum(padded)
    pad_start = pad_end - padded
    grp_start = jnp.cumsum(counts) - counts
    n_used = pad_end[-1] // tm
    blk = jnp.arange(n_blocks, dtype=i32)
    used = blk < n_used
    expert_at = lambda start: jnp.minimum(
        jnp.sum((pad_end[None, :] <= start[:, None]).astype(i32), axis=1), N_EXPERTS - 1)
    e_blk = expert_at(blk * tm)
    e_last = expert_at(((n_used - 1) * tm).reshape(1))[0]
    block_e = jnp.where(used, e_blk, e_last)
    row = blk[:, None] * tm + jnp.arange(tm, dtype=i32)[None, :]
    off = row - pad_start[e_blk][:, None]
    cnt = counts[e_blk][:, None]
    grp = grp_start[e_blk][:, None]
    valid = used[:, None] & (off < cnt)
    a = order[jnp.clip(grp + off, 0, n_assign - 1)]
    tok = a // TOP_K
    row_tok = jnp.where(valid, tok, 0)
    real_before = jnp.where(used[:, None], grp + cnt, n_assign)
    row_dst = jnp.where(valid, (a % TOP_K) * n_tok + tok, n_assign + row - real_before)
    idx = jnp.concatenate(
        [row_tok * ROW_PIECES, row_dst * ROW_PIECES,
         jnp.zeros((n_blocks, IDX_STRIDE - 2 * tm), i32)], axis=1)
    return block_e, n_used.reshape(1).astype(i32), idx.reshape(-1), n_rows


def _combine_kernel(p0_ref, p1_ref, p2_ref, p3_ref, gate_ref, x1_ref, g2_ref,
                    lng_ref, lnb_ref, o_ref):
    gate = gate_ref[...]
    n = gate.shape[0]
    y = (gate[:, 0:1] * _load_token_tiles(p0_ref, n) + gate[:, 1:2] * _load_token_tiles(p1_ref, n)
         + gate[:, 2:3] * _load_token_tiles(p2_ref, n) + gate[:, 3:4] * _load_token_tiles(p3_ref, n))
    o_ref[...] = _layer_norm(DEEPNORM_ALPHA * x1_ref[...] + g2_ref[...] * y,
                             lng_ref[...], lnb_ref[...])


def _combine(picked, gate, x1, g2, ln_g, ln_b, *, tm, rows_per_batch, batch_off, row_off, n_tok):
    m, d = x1.shape
    bpb = rows_per_batch // tm
    blk_off = row_off // tm
    k_stride = n_tok // tm
    deep = dict(pipeline_mode=pl.Buffered(COMBINE_BUFFERS)) if m // tm > 2 else {}
    pspec = lambda k: pl.BlockSpec((tm * ROW_PIECES, LANES),
                                   lambda i: (i + blk_off + k * k_stride, 0), **deep)
    row = lambda w: pl.BlockSpec((tm, w), lambda i: (i, 0))
    const = lambda a: pl.BlockSpec(a.shape, lambda i: (0, 0))
    in_specs = [pspec(0), pspec(1), pspec(2), pspec(3), row(LANES),
                pl.BlockSpec((tm, d), lambda i: (i, 0), **deep),
                _mod_spec(g2, tm, d, bpb, batch_off), const(ln_g), const(ln_b)]
    args = (picked, picked, picked, picked, gate, x1, g2, ln_g, ln_b)
    if not deep:
        return pl.pallas_call(
            _combine_kernel,
            out_shape=jax.ShapeDtypeStruct((m, d), F32),
            grid=(m // tm,),
            in_specs=in_specs,
            out_specs=row(d),
            compiler_params=_cparams(1, VMEM_LIMIT),
            name="combine_ln",
        )(*args)

    def pipelined(*refs):
        pltpu.emit_pipeline(_combine_kernel, grid=(m // tm,), in_specs=in_specs,
                            out_specs=[row(d)])(*refs)

    hbm = pl.BlockSpec(memory_space=pl.ANY)
    return pl.pallas_call(
        pipelined,
        out_shape=jax.ShapeDtypeStruct((m, d), F32),
        in_specs=[hbm] * len(args),
        out_specs=hbm,
        compiler_params=pltpu.CompilerParams(vmem_limit_bytes=VMEM_LIMIT),
        name="combine_ln_pipelined",
    )(*args)


def _clipped_rel_bias(rel, table):
    idx = np.clip(rel, -A_REL_CLIP, A_REL_CLIP) + A_REL_CLIP
    return jnp.moveaxis(table[idx].astype(F32), -1, 0)


def _t5_bucket(rel):
    nb = T5_BUCKETS // 2
    max_exact = nb // 2
    n = np.abs(rel)
    nf = np.maximum(n, 1).astype(np.float32)
    large = max_exact + (np.log(nf / max_exact) / math.log(T5_MAX_DISTANCE / max_exact)
                         * (nb - max_exact)).astype(np.int32)
    large = np.minimum(large, nb - 1)
    return np.where(rel > 0, nb, 0) + np.where(n < max_exact, n, large)


def _t5_rel_bias(rel, table):
    return jnp.moveaxis(table[_t5_bucket(rel)].astype(F32), -1, 0)


def kernel(x_prompt, x_sample, cache_a_k, cache_a_v, cache_b_k, cache_b_v, state_conv,
           c_prompt, c_sample, w_in, w_out, rel_bias_a, t5_bias, sinks, conv_w, conv_b,
           conv_ln_g, conv_ln_b, w_ada, b_ada, ln_g, ln_b, w_router, b_router,
           w_e_in, b_e_in, w_e_out, b_e_out):
    bp, tp, d = x_prompt.shape
    bs, ts, _ = x_sample.shape
    mp, ms = bp * tp, bs * ts
    n_tok = mp + ms
    na, nb = cache_a_k.shape[2], cache_b_k.shape[2]
    assert tp % ROW_TILE == 0 and mp % ts == 0 and ts >= CONV_W - 1 and tp >= A_REACH
    assert ms % OUTPROJ_TILE == 0 and tp % OUTPROJ_TILE == 0
    assert mp % ms == 0 and bs % SAMPLE_SEQS == 0

    xp = x_prompt.reshape(mp, d)
    xs = x_sample.reshape(ms, d)
    c_all = jnp.concatenate([c_prompt, c_sample], axis=0)

    t5_p = _band_bias(lambda rel: _t5_rel_bias(rel, t5_bias), **B_BAND)
    t5_s = _rel_bias_tile(lambda rel: _t5_rel_bias(rel, t5_bias), ts, nb + ts, nb)
    conv_zero = jnp.zeros((bp, CONV_HALO, C_CH), F32)

    states_p, states_s = [], []
    for l in range(DEPTH):
        mod = _adaln(c_all, w_ada, b_ada, l)
        sh1, sc1, g1, sh2, sc2, g2 = (mod[:, j * d:(j + 1) * d].reshape(bp + bs, 1, d)
                                      for j in range(6))
        w_in_bf = w_in[l].astype(BF16)
        w_out_bf = w_out[l].astype(BF16)
        wr = jnp.pad(w_router[l], ((0, 0), (0, LANES - N_EXPERTS)))
        wr_hi = wr.astype(BF16)
        wr_lo = (wr - wr_hi.astype(F32)).astype(BF16)
        br = jnp.pad(b_router[l], (0, LANES - N_EXPERTS)).reshape(1, LANES)
        lng1, lnb1 = ln_g[l, 0].reshape(1, d), ln_b[l, 0].reshape(1, d)
        lng2, lnb2 = ln_g[l, 1].reshape(1, d), ln_b[l, 1].reshape(1, d)
        bias_a_p = _band_bias(lambda rel: _clipped_rel_bias(rel, rel_bias_a[l]), **A_BAND)
        bias_a_s = _rel_bias_tile(lambda rel: _clipped_rel_bias(rel, rel_bias_a[l]),
                                  ts, na + ts, na)
        conv_args = (conv_w[l], conv_b[l], conv_ln_g[l], conv_ln_b[l])

        per_row = lambda a: jnp.broadcast_to(a[bp:], (bs, ts, d)).reshape(ms, d)
        qa, ka, va, qb, kb, vb, u = _inproj(xs, per_row(sc1), per_row(sh1), w_in_bf, tm=ms,
                                            rows_per_batch=ms, batch_off=0)
        s3 = lambda a: a.reshape(bs, ts, a.shape[-1])
        ka_all = jnp.concatenate([cache_a_k[l].reshape(bs, na, A_WIDTH), s3(ka)], axis=1)
        va_all = jnp.concatenate([cache_a_v[l].reshape(bs, na, A_WIDTH), s3(va)], axis=1)
        kb_all = jnp.concatenate([cache_b_k[l].reshape(bs, nb, B_KV_WIDTH), s3(kb)], axis=1)
        vb_all = jnp.concatenate([cache_b_v[l].reshape(bs, nb, B_KV_WIDTH), s3(vb)], axis=1)
        oa = _cache_attention(s3(qa), ka_all, va_all, bias_a_s, None,
                              n_heads=A_HEADS, group=1, n_seq=SAMPLE_SEQS)
        ob = _cache_attention(s3(qb), kb_all, vb_all, t5_s, sinks[l],
                              n_heads=B_HEADS, group=B_GROUP, n_seq=SAMPLE_SEQS)
        prev = jnp.pad(state_conv[l], ((0, 0), (CONV_HALO - (CONV_W - 1), 0), (0, 0)))
        oc = _conv_tail(prev, s3(u), *conv_args)
        x1s, h2s, idxs, gates = _outproj(
            oa.reshape(ms, -1), ob.reshape(ms, -1), oc.reshape(ms, -1), xs,
            per_row(g1), per_row(sc2), per_row(sh2),
            w_out_bf, lng1, lnb1, wr_hi, wr_lo, br, tm=ms, rows_per_batch=ms, batch_off=0)
        u_ext = jnp.concatenate([state_conv[l], s3(u)], axis=1)
        states_s.append((
            ka_all[:, -na:].reshape(bs, na, A_HEADS, HEAD_DIM),
            va_all[:, -na:].reshape(bs, na, A_HEADS, HEAD_DIM),
            kb_all[:, -nb:].reshape(bs, nb, B_KV_HEADS, HEAD_DIM),
            vb_all[:, -nb:].reshape(bs, nb, B_KV_HEADS, HEAD_DIM),
            u_ext[:, -(CONV_W - 1):]))

        qa, ka, va, qb, kb, vb, u = _inproj(xp, sc1, sh1, w_in_bf, tm=ROW_TILE,
                                            rows_per_batch=tp, batch_off=0)
        r3 = lambda a: a.reshape(bp, tp, a.shape[-1])
        oa = _band_attention(r3(qa), r3(ka), r3(va), bias_a_p, None, **A_BAND)
        ob = _band_attention(r3(qb), r3(kb), r3(vb), t5_p, sinks[l], **B_BAND)
        oc = _conv_tail(conv_zero, r3(u), *conv_args)
        x1p, h2, idxp, gatep = _outproj(
            oa.reshape(mp, -1), ob.reshape(mp, -1), oc.reshape(mp, -1), xp, g1, sc2, sh2,
            w_out_bf, lng1, lnb1, wr_hi, wr_lo, br, tm=OUTPROJ_TILE, rows_per_batch=tp, batch_off=0,
            h2_tail=h2s)
        states_p.append((
            r3(ka)[:, tp - A_REACH:].reshape(bp, A_REACH, A_HEADS, HEAD_DIM),
            r3(va)[:, tp - A_REACH:].reshape(bp, A_REACH, A_HEADS, HEAD_DIM),
            r3(kb)[:, tp - B_WINDOW:].reshape(bp, B_WINDOW, B_KV_HEADS, HEAD_DIM),
            r3(vb)[:, tp - B_WINDOW:].reshape(bp, B_WINDOW, B_KV_HEADS, HEAD_DIM),
            r3(u)[:, tp - (CONV_W - 1):]))

        top_idx = jnp.concatenate([idxp[:, :TOP_K], idxs[:, :TOP_K]], axis=0)
        block_e, n_used, idx_flat, n_rows = _routing_tables(top_idx, n_tok)
        picked = _moe(block_e, n_used, idx_flat, h2, w_e_in, b_e_in, w_e_out, b_e_out,
                      n_rows=n_rows, layer=l)
        xp = _combine(picked, gatep, x1p, g2, lng2, lnb2, tm=ROW_TILE, rows_per_batch=tp,
                      batch_off=0, row_off=0, n_tok=n_tok)
        xs = _combine(picked, gates, x1s, per_row(g2), lng2, lnb2, tm=ms, rows_per_batch=ms,
                      batch_off=0, row_off=mp, n_tok=n_tok)

    a_k_p, a_v_p, b_k_p, b_v_p, conv_p = (jnp.stack(z) for z in zip(*states_p))
    a_k_s, a_v_s, b_k_s, b_v_s, conv_s = (jnp.stack(z) for z in zip(*states_s))
    return (xp.reshape(bp, tp, d), xs.reshape(bs, ts, d), a_k_p, a_v_p, b_k_p, b_v_p, conv_p,
            a_k_s, a_v_s, b_k_s, b_v_s, conv_s)
```
